```python
import math
import jax, jax.numpy as jnp
from jax import lax
import numpy as np

D_MODEL = 1024
BATCH = 4
SEQ = 8192
DEPTH = 1

RWKV_HEAD = 64
RWKV_HEADS = D_MODEL // RWKV_HEAD
RWKV_WIDTH = RWKV_HEADS * RWKV_HEAD
LORA_DECAY = 64
LORA_ICLR = 64
LORA_GATE = 160
GN_EPS = 64e-5
ATTN_HEAD = 64
ATTN_HEADS_PER_GROUP = 8
ATTN_GROUPS = ((128, 1), (512, 4), (2048, 16))
N_ATTN_GROUPS = 3
ATTN_WIDTH = N_ATTN_GROUPS * ATTN_HEADS_PER_GROUP * ATTN_HEAD
ATTN_OUT = ATTN_HEADS_PER_GROUP * ATTN_HEAD
BLK = 128
FFN_HIDDEN = -(-8 * D_MODEL // (3 * 256)) * 256
RMS_EPS = 1e-6
PROJ_SIZES = (RWKV_WIDTH, RWKV_WIDTH, RWKV_WIDTH, ATTN_WIDTH, ATTN_WIDTH, ATTN_WIDTH, D_MODEL, D_MODEL)
PROJ_IN = 3 * RWKV_WIDTH + 3 * ATTN_WIDTH + 2 * D_MODEL

kernel_name = 'hybrid_rwkv7_dilated_alibi_block'


def _rmsnorm(t, g):
    t32 = t.astype(jnp.float32)
    n = t32 * lax.rsqrt(jnp.mean(t32 * t32, axis=-1, keepdims=True) + RMS_EPS)
    return n.astype(t.dtype) * g


def _shift(t):
    return jnp.pad(t, ((0, 0), (1, 0), (0, 0)))[:, :-1]


def _split_cols(t, sizes):
    idx, acc = [], 0
    for s in sizes[:-1]:
        acc += s
        idx.append(acc)
    return jnp.split(t, idx, axis=-1)


def _alibi_slopes(n):
    def pow2(m):
        start = 2.0 ** (-8.0 / m)
        return [start ** (i + 1) for i in range(m)]
    if math.log2(n).is_integer():
        s = pow2(n)
    else:
        p = 2 ** int(math.floor(math.log2(n)))
        s = pow2(p) + pow2(2 * p)[0::2][: n - p]
    return sorted(s, reverse=True)


def _rwkv7_recurrence(r, decay, k, v, a_vec, b_vec):
    B, T, H, N = r.shape

    def step(S, inp):
        r_t, w_t, k_t, v_t, a_t, b_t = inp
        Sa = jnp.einsum('bhij,bhj->bhi', S, a_t)
        S = S * w_t[:, :, None, :] + Sa[..., None] * b_t[:, :, None, :] + v_t[..., None] * k_t[:, :, None, :]
        return S, jnp.einsum('bhij,bhj->bhi', S, r_t)

    xs = tuple(jnp.moveaxis(t, 1, 0) for t in (r, decay, k, v, a_vec, b_vec))
    _, y = lax.scan(step, jnp.zeros((B, H, N, N), jnp.float32), xs)
    return jnp.moveaxis(y, 0, 1)


def _rwkv7_branch(h, p_r, p_k, p_v, mu_rkv, mu_lora, w0, w1, w2, a0, a1, a2, g1, g2, k_k, k_a, r_k, ln_x_w, ln_x_b, w_o):
    B, T, _ = h.shape
    f32 = jnp.float32
    r = p_r + (_shift(p_r) - p_r) * mu_rkv[0]
    k = p_k + (_shift(p_k) - p_k) * mu_rkv[1]
    v = p_v + (_shift(p_v) - p_v) * mu_rkv[2]
    dh = _shift(h) - h
    xw = h + dh * mu_lora[0]
    xa = h + dh * mu_lora[1]
    xg = h + dh * mu_lora[2]
    w_log = -jax.nn.softplus(-(w0 + jnp.tanh(xw @ w1) @ w2)) - 0.5
    decay = jnp.exp(-jnp.exp(w_log.astype(f32)))
    a = jax.nn.sigmoid(a0 + (xa @ a1) @ a2)
    g = jax.nn.sigmoid(xg @ g1) @ g2
    heads = lambda t: t.reshape(B, T, RWKV_HEADS, RWKV_HEAD)
    kk = heads(k * k_k).astype(f32)
    kk = kk / jnp.maximum(jnp.sqrt(jnp.sum(kk * kk, axis=-1, keepdims=True)), 1e-12)
    k = k * (1 + (a - 1) * k_a)
    r_h, k_h, v_h, a_h = heads(r), heads(k), heads(v), heads(a)
    y = _rwkv7_recurrence(r_h.astype(f32), heads(decay), k_h.astype(f32), v_h.astype(f32),
                          -kk, kk * a_h.astype(f32))
    mean = jnp.mean(y, axis=-1, keepdims=True)
    var = jnp.mean(jnp.square(y - mean), axis=-1, keepdims=True)
    y = ((y - mean) * lax.rsqrt(var + GN_EPS)).astype(h.dtype).reshape(B, T, RWKV_WIDTH) * ln_x_w + ln_x_b
    bonus = jnp.sum(r_h * k_h * r_k, axis=-1, keepdims=True) * v_h
    y = (y + bonus.reshape(B, T, RWKV_WIDTH)) * g
    return y @ w_o


def _dilated_window_attention(q, k, v, slopes, window, dilation):
    B, T, H, D = q.shape
    span = window // dilation
    L = T // dilation
    Lp = -(-L // BLK) * BLK
    nb = Lp // BLK

    def blocks(t):
        t = t.reshape(B, L, dilation, H, D)
        t = jnp.pad(t, ((0, 0), (0, Lp - L), (0, 0), (0, 0), (0, 0)))
        return t.reshape(B, nb, BLK, dilation, H, D)

    def band(t):
        prev = jnp.pad(t, ((0, 0), (1, 0), (0, 0), (0, 0), (0, 0), (0, 0)))[:, :-1]
        return jnp.concatenate([prev, t], axis=2)

    qb = blocks(q)
    kc, vc = band(blocks(k)), band(blocks(v))
    s = jnp.einsum('bnqrhd,bnkrhd->bnrhqk', qb, kc, preferred_element_type=jnp.float32) * (D ** -0.5)
    iq = jnp.arange(BLK)[:, None]
    jk = jnp.arange(2 * BLK)[None, :]
    delta = iq - jk + BLK
    nidx = jnp.arange(nb)[:, None, None]
    valid = ((delta >= 0) & (delta <= span))[None] & ((nidx > 0) | (jk[None] >= BLK))
    dist = (delta * dilation).astype(jnp.float32)
    s = s - slopes[:, None, None] * dist[None]
    s = jnp.where(valid[None, :, None, None], s, -jnp.inf)
    m = jnp.max(s, axis=-1, keepdims=True)
    p = jnp.exp(s - m)
    den = jnp.sum(p, axis=-1, keepdims=True)
    o = jnp.einsum('bnrhqk,bnkrhd->bnqrhd', p / den, vc.astype(jnp.float32))
    lse = jnp.transpose((m + jnp.log(den))[..., 0], (0, 1, 4, 2, 3))
    o = o.reshape(B, Lp, dilation, H, D)[:, :L].reshape(B, T, H, D)
    lse = lse.reshape(B, Lp, dilation, H)[:, :L].reshape(B, T, H)
    return o, lse


def setup_inputs(seed: int = 0) -> dict:
    key = jax.random.key(seed)
    ks = jax.random.split(key, 32)
    f32 = jnp.float32
    nrm = lambda k, shape, scale: jax.random.normal(k, shape, f32) * scale
    return {
        'x': nrm(ks[0], (BATCH, SEQ, D_MODEL), 1.0),
        'c': nrm(ks[1], (BATCH, D_MODEL), 1.0),
        'w_mod': nrm(ks[2], (DEPTH, D_MODEL, 6 * D_MODEL), 0.5 * D_MODEL ** -0.5),
        'b_mod': nrm(ks[3], (DEPTH, 6 * D_MODEL), 0.01),
        'g_pre_mix': 1.0 + nrm(ks[4], (DEPTH, D_MODEL), 0.05),
        'g_post_mix': 1.0 + nrm(ks[5], (DEPTH, D_MODEL), 0.05),
        'g_pre_ffn': 1.0 + nrm(ks[6], (DEPTH, D_MODEL), 0.05),
        'g_post_ffn': 1.0 + nrm(ks[7], (DEPTH, D_MODEL), 0.05),
        'w_in': nrm(ks[8], (DEPTH, D_MODEL, PROJ_IN), D_MODEL ** -0.5),
        'mu_rkv': jax.random.uniform(ks[9], (DEPTH, 3, RWKV_WIDTH), f32),
        'mu_lora': jax.random.uniform(ks[10], (DEPTH, 3, D_MODEL), f32),
        'w0': jax.random.uniform(ks[11], (DEPTH, RWKV_WIDTH), f32, -4.0, 1.0),
        'w1': nrm(ks[12], (DEPTH, D_MODEL, LORA_DECAY), D_MODEL ** -0.5),
        'w2': nrm(ks[13], (DEPTH, LORA_DECAY, RWKV_WIDTH), 0.5 * LORA_DECAY ** -0.5),
        'a0': nrm(ks[14], (DEPTH, RWKV_WIDTH), 0.1),
        'a1': nrm(ks[15], (DEPTH, D_MODEL, LORA_ICLR), D_MODEL ** -0.5),
        'a2': nrm(ks[16], (DEPTH, LORA_ICLR, RWKV_WIDTH), LORA_ICLR ** -0.5),
        'g1': nrm(ks[17], (DEPTH, D_MODEL, LORA_GATE), D_MODEL ** -0.5),
        'g2': nrm(ks[18], (DEPTH, LORA_GATE, RWKV_WIDTH), LORA_GATE ** -0.5),
        'k_k': 0.85 + nrm(ks[19], (DEPTH, RWKV_WIDTH), 0.05),
        'k_a': 1.0 + nrm(ks[20], (DEPTH, RWKV_WIDTH), 0.05),
        'r_k': nrm(ks[21], (DEPTH, RWKV_HEADS, RWKV_HEAD), 0.1),
        'ln_x_w': 1.0 + nrm(ks[22], (DEPTH, RWKV_WIDTH), 0.05),
        'ln_x_b': nrm(ks[23], (DEPTH, RWKV_WIDTH), 0.01),
        'w_o_rwkv': nrm(ks[24], (DEPTH, RWKV_WIDTH, D_MODEL), RWKV_WIDTH ** -0.5),
        'w_o_attn': nrm(ks[25], (DEPTH, ATTN_OUT, D_MODEL), ATTN_OUT ** -0.5),
        'w_out': nrm(ks[26], (DEPTH, D_MODEL, D_MODEL), D_MODEL ** -0.5),
        'w_ffn_in': nrm(ks[27], (DEPTH, D_MODEL, 2 * FFN_HIDDEN), D_MODEL ** -0.5),
        'w_ffn_out': nrm(ks[28], (DEPTH, FFN_HIDDEN, D_MODEL), FFN_HIDDEN ** -0.5),
    }


def reference(x, c, w_mod, b_mod, g_pre_mix, g_post_mix, g_pre_ffn, g_post_ffn, w_in, mu_rkv, mu_lora,
              w0, w1, w2, a0, a1, a2, g1, g2, k_k, k_a, r_k, ln_x_w, ln_x_b, w_o_rwkv, w_o_attn, w_out,
              w_ffn_in, w_ffn_out):
    B, T, _ = x.shape
    slopes = jnp.asarray(_alibi_slopes(N_ATTN_GROUPS * ATTN_HEADS_PER_GROUP), jnp.float32)
    slopes = slopes.reshape(N_ATTN_GROUPS, ATTN_HEADS_PER_GROUP)
    for l in range(DEPTH):
        mod = (c @ w_mod[l] + b_mod[l])[:, None, :]
        sh_m, sc_m, gt_m, sh_f, sc_f, gt_f = jnp.split(mod, 6, axis=-1)

        h = _rmsnorm(x, g_pre_mix[l]) * (1 + sc_m) + sh_m
        p_r, p_k, p_v, q_att, k_att, v_att, z_a, z_b = _split_cols(h @ w_in[l], PROJ_SIZES)

        y_rwkv = _rwkv7_branch(h, p_r, p_k, p_v, mu_rkv[l], mu_lora[l], w0[l], w1[l], w2[l], a0[l], a1[l],
                               a2[l], g1[l], g2[l], k_k[l], k_a[l], r_k[l], ln_x_w[l], ln_x_b[l], w_o_rwkv[l])

        grp = lambda t: t.reshape(B, T, N_ATTN_GROUPS, ATTN_HEADS_PER_GROUP, ATTN_HEAD)
        qg, kg, vg = grp(q_att), grp(k_att), grp(v_att)
        outs, lses = [], []
        for gi, (window, dilation) in enumerate(ATTN_GROUPS):
            o, lse = _dilated_window_attention(qg[:, :, gi], kg[:, :, gi], vg[:, :, gi], slopes[gi], window, dilation)
            outs.append(o)
            lses.append(lse)
        wts = jax.nn.softmax(jnp.stack(lses), axis=0)
        o = jnp.einsum('gbth,gbthd->bthd', wts, jnp.stack(outs))
        y_attn = o.reshape(B, T, ATTN_OUT).astype(x.dtype) @ w_o_attn[l]

        mixed = (jax.nn.sigmoid(z_a) * y_rwkv + jax.nn.sigmoid(z_b) * y_attn) @ w_out[l]
        x = x + gt_m * _rmsnorm(mixed, g_post_mix[l])

        h = _rmsnorm(x, g_pre_ffn[l]) * (1 + sc_f) + sh_f
        u_gate, u_up = jnp.split(h @ w_ffn_in[l], 2, axis=-1)
        y = (jax.nn.silu(u_gate) * u_up) @ w_ffn_out[l]
        x = x + gt_f * _rmsnorm(y, g_post_ffn[l])
    return x
```

```python
import functools
import math

import jax
import jax.numpy as jnp
from jax import lax
from jax.experimental import pallas as pl
from jax.experimental.pallas import tpu as pltpu

F32 = jnp.float32
BF16 = jnp.bfloat16
HIGHEST = lax.Precision.HIGHEST

D_MODEL = 1024
RWKV_HEAD = 64
GN_EPS = 64e-5
RMS_EPS = 1e-6
ATTN_HEAD = 64
ATTN_HEADS_PER_GROUP = 8
ATTN_GROUPS = ((128, 1), (512, 4), (2048, 16))
ATTN_GROUP_WIDTH = ATTN_HEADS_PER_GROUP * ATTN_HEAD
ATTN_WIDTH = len(ATTN_GROUPS) * ATTN_GROUP_WIDTH
ATTN_BLK = 128
FFN_HIDDEN = 2816
LORA_DECAY_PAD = 128
LORA_ICLR_PAD = 128
LORA_GATE_PAD = 256
CHUNK = 64
PAIR = 2 * RWKV_HEAD
N_PAIRS = D_MODEL // PAIR

COL_R, COL_K, COL_V, COL_ZA, COL_ZB = 0, 1024, 2048, 3072, 4096
COL_AQ, COL_AK, COL_AV = 5120, 6656, 8192
PROJ_COLS = 9728
PROJ_TILE = 512

VMEM_LIMIT = 48 * 1024 * 1024


def _alibi_slopes(n):
    def pow2(m):
        start = 2.0 ** (-8.0 / m)
        return [start ** (i + 1) for i in range(m)]
    if math.log2(n).is_integer():
        s = pow2(n)
    else:
        p = 2 ** int(math.floor(math.log2(n)))
        s = pow2(p) + pow2(2 * p)[0::2][: n - p]
    return sorted(s, reverse=True)


def _dot(a, b):
    return jnp.dot(a.astype(BF16), b.astype(BF16), preferred_element_type=F32)


def _dot_nt(a, b):
    return lax.dot_general(a.astype(BF16), b.astype(BF16), (((1,), (1,)), ((), ())),
                           preferred_element_type=F32)


def _dot_tn(a, b):
    return lax.dot_general(a.astype(BF16), b.astype(BF16), (((0,), (0,)), ((), ())),
                           preferred_element_type=F32)


def _seg_sum(x, e):
    hi = x.astype(BF16)
    lo = (x - hi.astype(F32)).astype(BF16)
    return (jnp.dot(hi, e, preferred_element_type=F32) + jnp.dot(lo, e, preferred_element_type=F32))


def _shift_rows(t, prev_row):
    rows = lax.broadcasted_iota(jnp.int32, t.shape, 0)
    return jnp.where(rows == 0, prev_row, pltpu.roll(t, 1, 0))


def _rms(t):
    return t * lax.rsqrt(jnp.mean(t * t, axis=-1, keepdims=True) + RMS_EPS)


def _params(*semantics):
    return pltpu.CompilerParams(dimension_semantics=semantics, vmem_limit_bytes=VMEM_LIMIT)


def _mod_kernel(c_ref, w_ref, b_ref, o_ref):
    o_ref[...] = jnp.dot(c_ref[...], w_ref[...], preferred_element_type=F32,
                         precision=HIGHEST) + b_ref[...]


def _mod(c, w_mod, b_mod):
    bsz = c.shape[0]
    n_out = w_mod.shape[1]
    return pl.pallas_call(
        _mod_kernel,
        grid=(n_out // D_MODEL,),
        in_specs=[pl.BlockSpec((bsz, D_MODEL), lambda j: (0, 0)),
                  pl.BlockSpec((D_MODEL, D_MODEL), lambda j: (0, j)),
                  pl.BlockSpec((1, D_MODEL), lambda j: (0, j))],
        out_specs=pl.BlockSpec((bsz, D_MODEL), lambda j: (0, j)),
        out_shape=jax.ShapeDtypeStruct((bsz, n_out), F32),
        compiler_params=_params("arbitrary"),
        name="adaln_mod",
    )(c, w_mod, b_mod.reshape(1, n_out))


def _inproj_kernel(x_ref, sc_ref, sh_ref, g_ref, mu_ref, w1_ref, a1_ref, g1_ref, win_ref,
                   p_ref, lw_ref, la_ref, lg_ref, h_scr, last_scr, *, tiles_per_batch):
    i = pl.program_id(0)
    j = pl.program_id(1)
    tm = x_ref.shape[0]

    @pl.when(j == 0)
    def _():
        @pl.when(i % tiles_per_batch == 0)
        def _():
            last_scr[...] = jnp.zeros_like(last_scr)

        h = _rms(x_ref[...]) * g_ref[...] * (1.0 + sc_ref[0]) + sh_ref[0]
        hs = _shift_rows(h, last_scr[...])
        last_scr[...] = h[tm - 1:tm, :]
        dh = hs - h
        lw_ref[...] = _dot(h + dh * mu_ref[0:1, :], w1_ref[...])
        la_ref[...] = _dot(h + dh * mu_ref[1:2, :], a1_ref[...])
        lg_ref[...] = _dot(h + dh * mu_ref[2:3, :], g1_ref[...])
        h_scr[...] = h.astype(BF16)

    p_ref[...] = jnp.dot(h_scr[...], win_ref[...], preferred_element_type=F32)


def _inproj(x2, sc, sh, g_pre, mu_lora, w1p, a1p, g1p, w_in_p, seq, tm):
    n_tok = x2.shape[0]
    tpb = seq // tm
    row = lambda i, j: (i, 0)
    const = lambda i, j: (0, 0)
    modrow = lambda i, j: (i // tpb, 0, 0)
    return pl.pallas_call(
        functools.partial(_inproj_kernel, tiles_per_batch=tpb),
        grid=(n_tok // tm, PROJ_COLS // PROJ_TILE),
        in_specs=[pl.BlockSpec((tm, D_MODEL), row),
                  pl.BlockSpec((1, 1, D_MODEL), modrow),
                  pl.BlockSpec((1, 1, D_MODEL), modrow),
                  pl.BlockSpec((1, D_MODEL), const),
                  pl.BlockSpec((3, D_MODEL), const),
                  pl.BlockSpec((D_MODEL, LORA_DECAY_PAD), const),
                  pl.BlockSpec((D_MODEL, LORA_ICLR_PAD), const),
                  pl.BlockSpec((D_MODEL, LORA_GATE_PAD), const),
                  pl.BlockSpec((D_MODEL, PROJ_TILE), lambda i, j: (0, j))],
        out_specs=[pl.BlockSpec((tm, PROJ_TILE), lambda i, j: (i, j)),
                   pl.BlockSpec((tm, LORA_DECAY_PAD), row),
                   pl.BlockSpec((tm, LORA_ICLR_PAD), row),
                   pl.BlockSpec((tm, LORA_GATE_PAD), row)],
        out_shape=[jax.ShapeDtypeStruct((n_tok, PROJ_COLS), F32),
                   jax.ShapeDtypeStruct((n_tok, LORA_DECAY_PAD), F32),
                   jax.ShapeDtypeStruct((n_tok, LORA_ICLR_PAD), F32),
                   jax.ShapeDtypeStruct((n_tok, LORA_GATE_PAD), F32)],
        scratch_shapes=[pltpu.VMEM((tm, D_MODEL), BF16), pltpu.VMEM((1, D_MODEL), F32)],
        compiler_params=_params("arbitrary", "arbitrary"),
        name="in_proj",
    )(x2, sc, sh, g_pre, mu_lora, w1p, a1p, g1p, w_in_p)


def _prep_kernel(pr_ref, pk_ref, pv_ref, lw_ref, la_ref, lg_ref, mu_ref, w0_ref, a0_ref, kk_ref,
                 ka_ref, rk_ref, w2_ref, a2_ref, g2_ref, e_ref,
                 r_o, lw_o, k_o, v_o, kk_o, kka_o, g_o, bonus_o, last_scr, *, tiles_per_batch):
    i = pl.program_id(0)
    tm = pr_ref.shape[0]

    @pl.when(i % tiles_per_batch == 0)
    def _():
        last_scr[...] = jnp.zeros_like(last_scr)

    def mix(p_ref, idx):
        p = p_ref[...]
        ps = _shift_rows(p, last_scr[idx:idx + 1, :])
        last_scr[idx:idx + 1, :] = p[tm - 1:tm, :]
        return p + (ps - p) * mu_ref[idx:idx + 1, :]

    r = mix(pr_ref, 0)
    k = mix(pk_ref, 1)
    v = mix(pv_ref, 2)
    e = e_ref[...]

    z = w0_ref[...] + _dot(jnp.tanh(lw_ref[...]), w2_ref[...])
    w_log = -(jnp.maximum(-z, 0.0) + jnp.log(1.0 + jnp.exp(-jnp.abs(z)))) - 0.5
    lw_o[...] = -jnp.exp(w_log)
    a = jax.nn.sigmoid(a0_ref[...] + _dot(la_ref[...], a2_ref[...]))
    g_o[...] = _dot(jax.nn.sigmoid(lg_ref[...]), g2_ref[...])

    kk = k * kk_ref[...]
    kk = kk / jnp.maximum(jnp.sqrt(_seg_sum(kk * kk, e)), 1e-12)
    k2 = k * (1.0 + (a - 1.0) * ka_ref[...])
    r_o[...] = r
    k_o[...] = k2
    v_o[...] = v
    kk_o[...] = kk
    kka_o[...] = kk * a
    bonus_o[...] = _seg_sum(r * k2 * rk_ref[...], e) * v


def _prep(proj, lw, la, lg, mu_rkv, w0, a0, k_k, k_a, r_k, w2p, a2p, g2p, e_seg, seq, tm):
    n_tok = proj.shape[0]
    tpb = seq // tm
    row = lambda i: (i, 0)
    const = lambda i: (0, 0)
    vec = pl.BlockSpec((1, D_MODEL), const)
    tok = pl.BlockSpec((tm, D_MODEL), row)
    return pl.pallas_call(
        functools.partial(_prep_kernel, tiles_per_batch=tpb),
        grid=(n_tok // tm,),
        in_specs=[pl.BlockSpec((tm, D_MODEL), lambda i: (i, COL_R // D_MODEL)),
                  pl.BlockSpec((tm, D_MODEL), lambda i: (i, COL_K // D_MODEL)),
                  pl.BlockSpec((tm, D_MODEL), lambda i: (i, COL_V // D_MODEL)),
                  pl.BlockSpec((tm, LORA_DECAY_PAD), row),
                  pl.BlockSpec((tm, LORA_ICLR_PAD), row),
                  pl.BlockSpec((tm, LORA_GATE_PAD), row),
                  pl.BlockSpec((3, D_MODEL), const),
                  vec, vec, vec, vec, vec,
                  pl.BlockSpec((LORA_DECAY_PAD, D_MODEL), const),
                  pl.BlockSpec((LORA_ICLR_PAD, D_MODEL), const),
                  pl.BlockSpec((LORA_GATE_PAD, D_MODEL), const),
                  pl.BlockSpec((D_MODEL, D_MODEL), const)],
        out_specs=[tok] * 8,
        out_shape=[jax.ShapeDtypeStruct((n_tok, D_MODEL), F32)] * 8,
        scratch_shapes=[pltpu.VMEM((3, D_MODEL), F32)],
        compiler_params=_params("arbitrary"),
        name="rwkv_prep",
    )(proj, proj, proj, lw, la, lg, mu_rkv, w0, a0, k_k, k_a, r_k, w2p, a2p, g2p, e_seg)


def _chunk_kernel(r_ref, lw_ref, k_ref, v_ref, kk_ref, kka_ref, y_ref, s_scr):
    @pl.when(pl.program_id(1) == 0)
    def _():
        s_scr[...] = jnp.zeros_like(s_scr)

    c = CHUNK
    lw = lw_ref[0]
    ti = lax.broadcasted_iota(jnp.int32, (c, c), 0)
    tj = lax.broadcasted_iota(jnp.int32, (c, c), 1)
    cum = jnp.dot((ti >= tj).astype(F32), lw, preferred_element_type=F32, precision=HIGHEST)
    cum_prev = cum - lw
    cum_end = cum[c - 1:c, :]
    kk = kk_ref[0]
    kka = kka_ref[0]
    k = k_ref[0]
    e_neg = jnp.exp(-cum)
    e_end = jnp.exp(cum_end - cum)
    at_all = -kk * jnp.exp(cum_prev)
    rt_all = r_ref[0] * jnp.exp(cum)
    bt_all = kka * e_neg
    kt_all = k * e_neg
    bh_all = kka * e_end
    kh_all = k * e_end
    p_end = jnp.exp(cum_end)
    v_all = v_ref[0]

    ri = lax.broadcasted_iota(jnp.int32, (PAIR, PAIR), 0)
    ci = lax.broadcasted_iota(jnp.int32, (PAIR, PAIR), 1)
    same = (ri // RWKV_HEAD) == (ci // RWKV_HEAD)
    m_strict = same & ((ri % c) > (ci % c))
    m_incl = same & ((ri % c) >= (ci % c))
    eye = ri == ci

    def bd(t):
        return jnp.where(same, jnp.concatenate([t, t], axis=0), 0.0)

    for p in range(N_PAIRS):
        sl = slice(PAIR * p, PAIR * (p + 1))
        at, rt, bt, kt = at_all[:, sl], rt_all[:, sl], bt_all[:, sl], kt_all[:, sl]
        at_bd, rt_bd, v_bd = bd(at), bd(rt), bd(v_all[:, sl])
        quad = _dot_nt(jnp.concatenate([at_bd, rt_bd], axis=0),
                       jnp.concatenate([bt, bt, kt, kt], axis=0))
        a_ab = jnp.where(m_strict, quad[:PAIR, :PAIR], 0.0)
        a_ak = jnp.where(m_strict, quad[:PAIR, PAIR:], 0.0)
        a_rb = jnp.where(m_incl, quad[PAIR:, :PAIR], 0.0)
        a_rk = jnp.where(m_incl, quad[PAIR:, PAIR:], 0.0)

        x = jnp.concatenate([at_bd, _dot(a_ak, v_bd)], axis=1)
        pw = a_ab
        x = x + _dot(pw, x)
        for _ in range(int(math.log2(c)) - 1):
            pw = _dot(pw, pw)
            x = x + _dot(pw, x)
        a_p, u0 = x[:, :PAIR], x[:, PAIR:]

        r_hat = rt_bd + _dot(a_rb, a_p)
        y0 = _dot(jnp.concatenate([a_rb, a_rk], axis=1), jnp.concatenate([u0, v_bd], axis=0))
        mg = _dot_tn(bd(bh_all[:, sl]), x)
        g_mat = mg[:, PAIR:] + _dot_tn(bd(kh_all[:, sl]), v_bd)
        m_mat = mg[:, :PAIR] + jnp.where(eye, jnp.broadcast_to(p_end[:, sl], (PAIR, PAIR)), 0.0)

        s = s_scr[p]
        y_bd = _dot(r_hat, s) + y0
        y_ref[0, :, sl] = y_bd[:c] + y_bd[c:]
        s_scr[p] = _dot(m_mat, s) + g_mat


def _chunk(r, lw, k, v, kk, kka):
    bsz, seq, _ = r.shape
    spec = pl.BlockSpec((1, CHUNK, D_MODEL), lambda b, c: (b, c, 0))
    return pl.pallas_call(
        _chunk_kernel,
        grid=(bsz, seq // CHUNK),
        in_specs=[spec] * 6,
        out_specs=spec,
        out_shape=jax.ShapeDtypeStruct((bsz, seq, D_MODEL), F32),
        scratch_shapes=[pltpu.VMEM((N_PAIRS, PAIR, PAIR), F32)],
        compiler_params=_params("arbitrary", "arbitrary"),
        name="rwkv_chunk",
    )(r, lw, k, v, kk, kka)


def _attn_kernel(q_ref, kc_ref, kp_ref, vc_ref, vp_ref, o_ref, l_ref, *, dilation, slopes):
    n = pl.program_id(2)
    blk = ATTN_BLK
    ii = lax.broadcasted_iota(jnp.int32, (blk, blk), 0)
    jj = lax.broadcasted_iota(jnp.int32, (blk, blk), 1)
    d_cur = ii - jj
    valid_c = d_cur >= 0
    valid_p = (d_cur <= 0) & (n > 0)
    dist_c = d_cur.astype(F32) * float(dilation)
    dist_p = (d_cur + blk).astype(F32) * float(dilation)
    lane_head = jj // ATTN_HEAD
    neg_inf = float("-inf")

    q_all = q_ref[0] * (ATTN_HEAD ** -0.5)
    for pp in range(ATTN_GROUP_WIDTH // PAIR):
        sl = slice(PAIR * pp, PAIR * (pp + 1))
        q2 = q_all[:, sl].astype(BF16)
        kc, kp = kc_ref[0, :, sl], kp_ref[0, :, sl]
        vc, vp = vc_ref[0, :, sl], vp_ref[0, :, sl]
        o_pair = jnp.zeros((blk, PAIR), F32)
        l_pair = jnp.zeros((blk, PAIR), F32)
        for hh in range(2):
            slope = float(slopes[2 * pp + hh])
            qm = jnp.where(lane_head == hh, q2, jnp.zeros_like(q2))
            s_c = jnp.where(valid_c, _dot_nt(qm, kc) - slope * dist_c, neg_inf)
            s_p = jnp.where(valid_p, _dot_nt(qm, kp) - slope * dist_p, neg_inf)
            m = jnp.maximum(jnp.max(s_c, axis=-1, keepdims=True), jnp.max(s_p, axis=-1, keepdims=True))
            p_c = jnp.exp(s_c - m)
            p_p = jnp.exp(s_p - m)
            den = jnp.sum(p_c, axis=-1, keepdims=True) + jnp.sum(p_p, axis=-1, keepdims=True)
            o = (_dot(p_c, vc) + _dot(p_p, vp)) / den
            o_pair = jnp.where(lane_head == hh, o, o_pair)
            l_pair = jnp.where(lane_head == hh, m + jnp.log(den), l_pair)
        o_ref[0, :, sl] = o_pair
        l_ref[0, :, sl] = l_pair


def _attn_group(proj3, gi, dilation, slopes, bsz, seq):
    sub = seq // dilation
    nblk_cols = PROJ_COLS // ATTN_GROUP_WIDTH
    qcol = COL_AQ // ATTN_GROUP_WIDTH + gi
    kcol = COL_AK // ATTN_GROUP_WIDTH + gi
    vcol = COL_AV // ATTN_GROUP_WIDTH + gi
    blk = (1, ATTN_BLK, ATTN_GROUP_WIDTH)
    cur = lambda col: pl.BlockSpec(blk, lambda b, r, n: (b, n, r * nblk_cols + col))
    prev = lambda col: pl.BlockSpec(blk, lambda b, r, n: (b, jnp.maximum(n - 1, 0), r * nblk_cols + col))
    out = pl.BlockSpec(blk, lambda b, r, n: (b, n, r))
    shape = jax.ShapeDtypeStruct((bsz, sub, dilation * ATTN_GROUP_WIDTH), F32)
    o, l = pl.pallas_call(
        functools.partial(_attn_kernel, dilation=dilation, slopes=tuple(slopes)),
        grid=(bsz, dilation, sub // ATTN_BLK),
        in_specs=[cur(qcol), cur(kcol), prev(kcol), cur(vcol), prev(vcol)],
        out_specs=[out, out],
        out_shape=[shape, shape],
        compiler_params=_params("arbitrary", "arbitrary", "arbitrary"),
        name=f"dilated_attn_d{dilation}",
    )(proj3, proj3, proj3, proj3, proj3)
    return o.reshape(bsz * seq, ATTN_GROUP_WIDTH), l.reshape(bsz * seq, ATTN_GROUP_WIDTH)


def _mix_kernel(y_ref, bonus_ref, g_ref, za_ref, zb_ref, x_ref, o0_ref, o1_ref, o2_ref,
                l0_ref, l1_ref, l2_ref, gt_ref, lnw_ref, lnb_ref, gpost_ref,
                wor_ref, woa_ref, wout_ref, e_ref, out_ref):
    e = e_ref[...]
    inv_n = 1.0 / RWKV_HEAD
    y = y_ref[...]
    mean = _seg_sum(y, e) * inv_n
    yc = y - mean
    var = _seg_sum(yc * yc, e) * inv_n
    yn = yc * lax.rsqrt(var + GN_EPS) * lnw_ref[...] + lnb_ref[...]
    y_rwkv = _dot((yn + bonus_ref[...]) * g_ref[...], wor_ref[...])

    l0, l1, l2 = l0_ref[...], l1_ref[...], l2_ref[...]
    mx = jnp.maximum(jnp.maximum(l0, l1), l2)
    e0, e1, e2 = jnp.exp(l0 - mx), jnp.exp(l1 - mx), jnp.exp(l2 - mx)
    o = (e0 * o0_ref[...] + e1 * o1_ref[...] + e2 * o2_ref[...]) / (e0 + e1 + e2)
    y_attn = _dot(o, woa_ref[...])

    mixed = _dot(jax.nn.sigmoid(za_ref[...]) * y_rwkv + jax.nn.sigmoid(zb_ref[...]) * y_attn,
                 wout_ref[...])
    out_ref[...] = x_ref[...] + gt_ref[0] * (_rms(mixed) * gpost_ref[...])


def _mix(y, bonus, g, proj, x2, o_l, gt, ln_w, ln_b, g_post, w_or, w_oa, w_out, e_seg, seq, tm):
    n_tok = x2.shape[0]
    tpb = seq // tm
    row = lambda i: (i, 0)
    const = lambda i: (0, 0)
    tok = pl.BlockSpec((tm, D_MODEL), row)
    att = pl.BlockSpec((tm, ATTN_GROUP_WIDTH), row)
    vec = pl.BlockSpec((1, D_MODEL), const)
    (o0, l0), (o1, l1), (o2, l2) = o_l
    return pl.pallas_call(
        _mix_kernel,
        grid=(n_tok // tm,),
        in_specs=[tok, tok, tok,
                  pl.BlockSpec((tm, D_MODEL), lambda i: (i, COL_ZA // D_MODEL)),
                  pl.BlockSpec((tm, D_MODEL), lambda i: (i, COL_ZB // D_MODEL)),
                  tok, att, att, att, att, att, att,
                  pl.BlockSpec((1, 1, D_MODEL), lambda i: (i // tpb, 0, 0)),
                  vec, vec, vec,
                  pl.BlockSpec((D_MODEL, D_MODEL), const),
                  pl.BlockSpec((ATTN_GROUP_WIDTH, D_MODEL), const),
                  pl.BlockSpec((D_MODEL, D_MODEL), const),
                  pl.BlockSpec((D_MODEL, D_MODEL), const)],
        out_specs=tok,
        out_shape=jax.ShapeDtypeStruct((n_tok, D_MODEL), F32),
        compiler_params=_params("arbitrary"),
        name="mix_out",
    )(y, bonus, g, proj, proj, x2, o0, o1, o2, l0, l1, l2, gt, ln_w, ln_b, g_post,
      w_or, w_oa, w_out, e_seg)


def _ffn_kernel(x_ref, sc_ref, sh_ref, gt_ref, gpre_ref, gpost_ref, wg_ref, wu_ref, wo_ref,
                out_ref, h_scr, acc_scr):
    j = pl.program_id(1)

    @pl.when(j == 0)
    def _():
        h = _rms(x_ref[...]) * gpre_ref[...] * (1.0 + sc_ref[0]) + sh_ref[0]
        h_scr[...] = h.astype(BF16)
        acc_scr[...] = jnp.zeros_like(acc_scr)

    h = h_scr[...]
    u_gate = jnp.dot(h, wg_ref[...], preferred_element_type=F32)
    u_up = jnp.dot(h, wu_ref[...], preferred_element_type=F32)
    acc_scr[...] += _dot(u_gate * jax.nn.sigmoid(u_gate) * u_up, wo_ref[...])

    @pl.when(j == pl.num_programs(1) - 1)
    def _():
        out_ref[...] = x_ref[...] + gt_ref[0] * (_rms(acc_scr[...]) * gpost_ref[...])


def _ffn(x1, sc, sh, gt, g_pre, g_post, w_in, w_out, seq, tm, th):
    n_tok = x1.shape[0]
    tpb = seq // tm
    n_h = FFN_HIDDEN // th
    row = lambda i, j: (i, 0)
    const = lambda i, j: (0, 0)
    modrow = lambda i, j: (i // tpb, 0, 0)
    tok = pl.BlockSpec((tm, D_MODEL), row)
    mod = pl.BlockSpec((1, 1, D_MODEL), modrow)
    vec = pl.BlockSpec((1, D_MODEL), const)
    return pl.pallas_call(
        _ffn_kernel,
        grid=(n_tok // tm, n_h),
        in_specs=[tok, mod, mod, mod, vec, vec,
                  pl.BlockSpec((D_MODEL, th), lambda i, j: (0, j)),
                  pl.BlockSpec((D_MODEL, th), lambda i, j: (0, n_h + j)),
                  pl.BlockSpec((th, D_MODEL), lambda i, j: (j, 0))],
        out_specs=tok,
        out_shape=jax.ShapeDtypeStruct((n_tok, D_MODEL), F32),
        scratch_shapes=[pltpu.VMEM((tm, D_MODEL), BF16), pltpu.VMEM((tm, D_MODEL), F32)],
        compiler_params=_params("arbitrary", "arbitrary"),
        name="swiglu_ffn",
    )(x1, sc, sh, gt, g_pre, g_post, w_in, w_in, w_out)


def _pad_cols(w, n):
    return jnp.pad(w, ((0, 0), (0, n - w.shape[1]))).astype(BF16)


def _pad_rows(w, n):
    return jnp.pad(w, ((0, n - w.shape[0]), (0, 0))).astype(BF16)


def kernel(x, c, w_mod, b_mod, g_pre_mix, g_post_mix, g_pre_ffn, g_post_ffn, w_in, mu_rkv, mu_lora,
           w0, w1, w2, a0, a1, a2, g1, g2, k_k, k_a, r_k, ln_x_w, ln_x_b, w_o_rwkv, w_o_attn, w_out,
           w_ffn_in, w_ffn_out):
    bsz, seq, _ = x.shape
    depth = w_mod.shape[0]
    assert seq % (ATTN_BLK * ATTN_GROUPS[-1][1]) == 0
    n_tok = bsz * seq
    tm_proj = min(1024, seq)
    tm_prep = min(256, seq)
    tm_mix = min(256, seq)
    tm_ffn = min(512, seq)
    th_ffn = FFN_HIDDEN // 2

    slopes = _alibi_slopes(len(ATTN_GROUPS) * ATTN_HEADS_PER_GROUP)
    seg = jnp.arange(D_MODEL) // RWKV_HEAD
    e_seg = (seg[:, None] == seg[None, :]).astype(BF16)
    row = lambda t: t.reshape(1, -1)

    x2 = x.reshape(n_tok, D_MODEL)
    for l in range(depth):
        mod = _mod(c, w_mod[l], b_mod[l])
        sh_m, sc_m, gt_m, sh_f, sc_f, gt_f = [mod[:, i * D_MODEL:(i + 1) * D_MODEL].reshape(bsz, 1, D_MODEL)
                                               for i in range(6)]
        wl = w_in[l]
        p_r, p_k, p_v, q_a, k_a_, v_a, z_a, z_b = jnp.split(
            wl, [1024, 2048, 3072, 3072 + ATTN_WIDTH, 3072 + 2 * ATTN_WIDTH, 3072 + 3 * ATTN_WIDTH,
                 3072 + 3 * ATTN_WIDTH + 1024], axis=1)
        w_in_p = jnp.concatenate([p_r, p_k, p_v, z_a, z_b, q_a, k_a_, v_a], axis=1).astype(BF16)

        proj, lw, la, lg = _inproj(
            x2, sc_m, sh_m, row(g_pre_mix[l]), mu_lora[l],
            _pad_cols(w1[l], LORA_DECAY_PAD), _pad_cols(a1[l], LORA_ICLR_PAD),
            _pad_cols(g1[l], LORA_GATE_PAD), w_in_p, seq, tm_proj)

        r, lwd, k2, v, kk, kka, g, bonus = _prep(
            proj, lw, la, lg, mu_rkv[l], row(w0[l]), row(a0[l]), row(k_k[l]), row(k_a[l]), row(r_k[l]),
            _pad_rows(w2[l], LORA_DECAY_PAD), _pad_rows(a2[l], LORA_ICLR_PAD),
            _pad_rows(g2[l], LORA_GATE_PAD), e_seg, seq, tm_prep)

        b3 = lambda t: t.reshape(bsz, seq, D_MODEL)
        y = _chunk(b3(r), b3(lwd), b3(k2), b3(v), b3(kk), b3(kka)).reshape(n_tok, D_MODEL)

        o_l = []
        for gi, (window, dilation) in enumerate(ATTN_GROUPS):
            assert window // dilation == ATTN_BLK
            proj3 = proj.reshape(bsz, seq // dilation, dilation * PROJ_COLS)
            o_l.append(_attn_group(proj3, gi, dilation,
                                   slopes[gi * ATTN_HEADS_PER_GROUP:(gi + 1) * ATTN_HEADS_PER_GROUP],
                                   bsz, seq))

        x2 = _mix(y, bonus, g, proj, x2, o_l, gt_m, row(ln_x_w[l]), row(ln_x_b[l]), row(g_post_mix[l]),
                  w_o_rwkv[l].astype(BF16), w_o_attn[l].astype(BF16), w_out[l].astype(BF16), e_seg,
                  seq, tm_mix)

        x2 = _ffn(x2, sc_f, sh_f, gt_f, row(g_pre_ffn[l]), row(g_post_ffn[l]),
                  w_ffn_in[l].astype(BF16), w_ffn_out[l].astype(BF16), seq, tm_ffn, th_ffn)
    return x2.reshape(bsz, seq, D_MODEL)
```

```python
import functools
import math

import jax
import jax.numpy as jnp
from jax import lax
from jax.experimental import pallas as pl
from jax.experimental.pallas import tpu as pltpu

F32 = jnp.float32
BF16 = jnp.bfloat16
HIGHEST = lax.Precision.HIGHEST

LANES = 128
D_MODEL = 1024
RWKV_HEAD = 64
GN_EPS = 64e-5
RMS_EPS = 1e-6
ATTN_HEAD = 64
ATTN_HEADS_PER_GROUP = 8
ATTN_GROUPS = ((128, 1), (512, 4), (2048, 16))
ATTN_GROUP_WIDTH = ATTN_HEADS_PER_GROUP * ATTN_HEAD
ATTN_WIDTH = len(ATTN_GROUPS) * ATTN_GROUP_WIDTH
ATTN_BLK = 128
FFN_HIDDEN = 2816
LORA_DECAY_PAD = 128
LORA_ICLR_PAD = 128
LORA_GATE_PAD = 256
CHUNK = 64
PAIR = 2 * RWKV_HEAD
N_PAIRS = D_MODEL // PAIR

COL_R, COL_K, COL_V, COL_ZA, COL_ZB = 0, 1024, 2048, 3072, 4096
MAIN_COLS = 5120
PROJ_TILE = 512

VMEM_LIMIT = 48 * 1024 * 1024


def _alibi_slopes(n):
    def pow2(m):
        start = 2.0 ** (-8.0 / m)
        return [start ** (i + 1) for i in range(m)]
    if math.log2(n).is_integer():
        s = pow2(n)
    else:
        p = 2 ** int(math.floor(math.log2(n)))
        s = pow2(p) + pow2(2 * p)[0::2][: n - p]
    return sorted(s, reverse=True)


def _dot(a, b):
    return jnp.dot(a.astype(BF16), b.astype(BF16), preferred_element_type=F32)


def _dot_nt(a, b):
    return lax.dot_general(a.astype(BF16), b.astype(BF16), (((1,), (1,)), ((), ())),
                           preferred_element_type=F32)


def _dot_tn(a, b):
    return lax.dot_general(a.astype(BF16), b.astype(BF16), (((0,), (0,)), ((), ())),
                           preferred_element_type=F32)


def _seg_sum(x, e):
    hi = x.astype(BF16)
    lo = (x - hi.astype(F32)).astype(BF16)
    return (jnp.dot(hi, e, preferred_element_type=F32) + jnp.dot(lo, e, preferred_element_type=F32))


def _shift_rows(t, prev_row):
    rows = lax.broadcasted_iota(jnp.int32, t.shape, 0)
    return jnp.where(rows == 0, prev_row, pltpu.roll(t, 1, 0))


def _rms(t):
    return t * lax.rsqrt(jnp.mean(t * t, axis=-1, keepdims=True) + RMS_EPS)


def _params(*semantics):
    return pltpu.CompilerParams(dimension_semantics=semantics, vmem_limit_bytes=VMEM_LIMIT)


def _mod_kernel(c_ref, w_ref, b_ref, o_ref):
    o_ref[...] = jnp.dot(c_ref[...], w_ref[...], preferred_element_type=F32,
                         precision=HIGHEST) + b_ref[...]


def _mod(c, w_mod, b_mod):
    bsz = c.shape[0]
    n_out = w_mod.shape[1]
    return pl.pallas_call(
        _mod_kernel,
        grid=(n_out // D_MODEL,),
        in_specs=[pl.BlockSpec((bsz, D_MODEL), lambda j: (0, 0)),
                  pl.BlockSpec((D_MODEL, D_MODEL), lambda j: (0, j)),
                  pl.BlockSpec((1, D_MODEL), lambda j: (0, j))],
        out_specs=pl.BlockSpec((bsz, D_MODEL), lambda j: (0, j)),
        out_shape=jax.ShapeDtypeStruct((bsz, n_out), F32),
        compiler_params=_params("arbitrary"),
        name="adaln_mod",
    )(c, w_mod, b_mod.reshape(1, n_out))


def _inproj_kernel(x_ref, sc_ref, sh_ref, g_ref, mu_ref, w1_ref, a1_ref, g1_ref, win_ref,
                   p_ref, h_ref, lw_ref, la_ref, lg_ref, last_scr, *, tiles_per_batch):
    i = pl.program_id(0)
    j = pl.program_id(1)
    tm = x_ref.shape[0]

    @pl.when(j == 0)
    def _():
        @pl.when(i % tiles_per_batch == 0)
        def _():
            last_scr[...] = jnp.zeros_like(last_scr)

        h = _rms(x_ref[...]) * g_ref[...] * (1.0 + sc_ref[0]) + sh_ref[0]
        hs = _shift_rows(h, last_scr[...])
        last_scr[...] = h[tm - 1:tm, :]
        dh = hs - h
        lw_ref[...] = _dot(h + dh * mu_ref[0:1, :], w1_ref[...])
        la_ref[...] = _dot(h + dh * mu_ref[1:2, :], a1_ref[...])
        lg_ref[...] = _dot(h + dh * mu_ref[2:3, :], g1_ref[...])
        h_ref[...] = h.astype(BF16)

    p_ref[...] = jnp.dot(h_ref[...], win_ref[...], preferred_element_type=F32)


def _inproj(x2, sc, sh, g_pre, mu_lora, w1p, a1p, g1p, w_main, seq, tm):
    n_tok = x2.shape[0]
    tpb = seq // tm
    row = lambda i, j: (i, 0)
    const = lambda i, j: (0, 0)
    modrow = lambda i, j: (i // tpb, 0, 0)
    return pl.pallas_call(
        functools.partial(_inproj_kernel, tiles_per_batch=tpb),
        grid=(n_tok // tm, MAIN_COLS // PROJ_TILE),
        in_specs=[pl.BlockSpec((tm, D_MODEL), row),
                  pl.BlockSpec((1, 1, D_MODEL), modrow),
                  pl.BlockSpec((1, 1, D_MODEL), modrow),
                  pl.BlockSpec((1, D_MODEL), const),
                  pl.BlockSpec((3, D_MODEL), const),
                  pl.BlockSpec((D_MODEL, LORA_DECAY_PAD), const),
                  pl.BlockSpec((D_MODEL, LORA_ICLR_PAD), const),
                  pl.BlockSpec((D_MODEL, LORA_GATE_PAD), const),
                  pl.BlockSpec((D_MODEL, PROJ_TILE), lambda i, j: (0, j))],
        out_specs=[pl.BlockSpec((tm, PROJ_TILE), lambda i, j: (i, j)),
                   pl.BlockSpec((tm, D_MODEL), row),
                   pl.BlockSpec((tm, LORA_DECAY_PAD), row),
                   pl.BlockSpec((tm, LORA_ICLR_PAD), row),
                   pl.BlockSpec((tm, LORA_GATE_PAD), row)],
        out_shape=[jax.ShapeDtypeStruct((n_tok, MAIN_COLS), F32),
                   jax.ShapeDtypeStruct((n_tok, D_MODEL), BF16),
                   jax.ShapeDtypeStruct((n_tok, LORA_DECAY_PAD), F32),
                   jax.ShapeDtypeStruct((n_tok, LORA_ICLR_PAD), F32),
                   jax.ShapeDtypeStruct((n_tok, LORA_GATE_PAD), F32)],
        scratch_shapes=[pltpu.VMEM((1, D_MODEL), F32)],
        compiler_params=_params("arbitrary", "arbitrary"),
        name="in_proj",
    )(x2, sc, sh, g_pre, mu_lora, w1p, a1p, g1p, w_main)


def _attn_proj_kernel(h_ref, w_ref, o_ref, *scratch, dilation):
    acc = jnp.dot(h_ref[...], w_ref[...], preferred_element_type=F32)
    if dilation == 1:
        o_ref[0, 0] = acc.astype(BF16)
    else:
        scr, = scratch
        rows = acc.shape[0] // dilation
        for cg in range(scr.shape[0]):
            scr[cg] = acc[:, LANES * cg:LANES * (cg + 1)]
        for r in range(dilation):
            for cg in range(scr.shape[0]):
                o_ref[0, r, :, LANES * cg:LANES * (cg + 1)] = (
                    scr[cg, pl.ds(r, rows, stride=dilation), :].astype(BF16))


def _attn_proj(h, w_qkv, dilation, bsz, seq, tm):
    tpb = seq // tm
    width = 3 * ATTN_GROUP_WIDTH
    return pl.pallas_call(
        functools.partial(_attn_proj_kernel, dilation=dilation),
        grid=(bsz * tpb,),
        in_specs=[pl.BlockSpec((tm, D_MODEL), lambda i: (i, 0)),
                  pl.BlockSpec((D_MODEL, width), lambda i: (0, 0))],
        out_specs=pl.BlockSpec((1, dilation, tm // dilation, width), lambda i: (i // tpb, 0, i % tpb, 0)),
        out_shape=jax.ShapeDtypeStruct((bsz, dilation, seq // dilation, width), BF16),
        scratch_shapes=[] if dilation == 1 else [pltpu.VMEM((width // LANES, tm, LANES), F32)],
        compiler_params=_params("arbitrary"),
        name=f"attn_proj_d{dilation}",
    )(h, w_qkv)


def _prep_kernel(pr_ref, pk_ref, pv_ref, lw_ref, la_ref, lg_ref, mu_ref, w0_ref, a0_ref, kk_ref,
                 ka_ref, rk_ref, w2_ref, a2_ref, g2_ref, e_ref,
                 r_o, lw_o, k_o, v_o, kk_o, kka_o, g_o, bonus_o, last_scr, *, tiles_per_batch):
    i = pl.program_id(0)
    tm = pr_ref.shape[0]

    @pl.when(i % tiles_per_batch == 0)
    def _():
        last_scr[...] = jnp.zeros_like(last_scr)

    def mix(p_ref, idx):
        p = p_ref[...]
        ps = _shift_rows(p, last_scr[idx:idx + 1, :])
        last_scr[idx:idx + 1, :] = p[tm - 1:tm, :]
        return p + (ps - p) * mu_ref[idx:idx + 1, :]

    r = mix(pr_ref, 0)
    k = mix(pk_ref, 1)
    v = mix(pv_ref, 2)
    e = e_ref[...]

    z = w0_ref[...] + _dot(jnp.tanh(lw_ref[...]), w2_ref[...])
    w_log = -(jnp.maximum(-z, 0.0) + jnp.log(1.0 + jnp.exp(-jnp.abs(z)))) - 0.5
    lw_o[...] = -jnp.exp(w_log)
    a = jax.nn.sigmoid(a0_ref[...] + _dot(la_ref[...], a2_ref[...]))
    g_o[...] = _dot(jax.nn.sigmoid(lg_ref[...]), g2_ref[...])

    kk = k * kk_ref[...]
    kk = kk / jnp.maximum(jnp.sqrt(_seg_sum(kk * kk, e)), 1e-12)
    k2 = k * (1.0 + (a - 1.0) * ka_ref[...])
    r_o[...] = r
    k_o[...] = k2
    v_o[...] = v
    kk_o[...] = kk
    kka_o[...] = kk * a
    bonus_o[...] = _seg_sum(r * k2 * rk_ref[...], e) * v


def _prep(proj, lw, la, lg, mu_rkv, w0, a0, k_k, k_a, r_k, w2p, a2p, g2p, e_seg, seq, tm):
    n_tok = proj.shape[0]
    tpb = seq // tm
    row = lambda i: (i, 0)
    const = lambda i: (0, 0)
    vec = pl.BlockSpec((1, D_MODEL), const)
    tok = pl.BlockSpec((tm, D_MODEL), row)
    return pl.pallas_call(
        functools.partial(_prep_kernel, tiles_per_batch=tpb),
        grid=(n_tok // tm,),
        in_specs=[pl.BlockSpec((tm, D_MODEL), lambda i: (i, COL_R // D_MODEL)),
                  pl.BlockSpec((tm, D_MODEL), lambda i: (i, COL_K // D_MODEL)),
                  pl.BlockSpec((tm, D_MODEL), lambda i: (i, COL_V // D_MODEL)),
                  pl.BlockSpec((tm, LORA_DECAY_PAD), row),
                  pl.BlockSpec((tm, LORA_ICLR_PAD), row),
                  pl.BlockSpec((tm, LORA_GATE_PAD), row),
                  pl.BlockSpec((3, D_MODEL), const),
                  vec, vec, vec, vec, vec,
                  pl.BlockSpec((LORA_DECAY_PAD, D_MODEL), const),
                  pl.BlockSpec((LORA_ICLR_PAD, D_MODEL), const),
                  pl.BlockSpec((LORA_GATE_PAD, D_MODEL), const),
                  pl.BlockSpec((D_MODEL, D_MODEL), const)],
        out_specs=[tok] * 8,
        out_shape=[jax.ShapeDtypeStruct((n_tok, D_MODEL), F32)] * 8,
        scratch_shapes=[pltpu.VMEM((3, D_MODEL), F32)],
        compiler_params=_params("arbitrary"),
        name="rwkv_prep",
    )(proj, proj, proj, lw, la, lg, mu_rkv, w0, a0, k_k, k_a, r_k, w2p, a2p, g2p, e_seg)


def _chunk_kernel(r_ref, lw_ref, k_ref, v_ref, kk_ref, kka_ref, y_ref, s_scr):
    @pl.when(pl.program_id(1) == 0)
    def _():
        s_scr[...] = jnp.zeros_like(s_scr)

    c = CHUNK
    lw = lw_ref[0]
    ti = lax.broadcasted_iota(jnp.int32, (c, c), 0)
    tj = lax.broadcasted_iota(jnp.int32, (c, c), 1)
    cum = jnp.dot((ti >= tj).astype(F32), lw, preferred_element_type=F32, precision=HIGHEST)
    cum_prev = cum - lw
    cum_end = cum[c - 1:c, :]
    kk = kk_ref[0]
    kka = kka_ref[0]
    k = k_ref[0]
    e_neg = jnp.exp(-cum)
    e_end = jnp.exp(cum_end - cum)
    at_all = -kk * jnp.exp(cum_prev)
    rt_all = r_ref[0] * jnp.exp(cum)
    bt_all = (kka * e_neg).astype(BF16)
    kt_all = (k * e_neg).astype(BF16)
    bh_all = kka * e_end
    kh_all = k * e_end
    p_end = jnp.exp(cum_end)
    v_all = v_ref[0]

    ri = lax.broadcasted_iota(jnp.int32, (PAIR, PAIR), 0)
    ci = lax.broadcasted_iota(jnp.int32, (PAIR, PAIR), 1)
    same = (ri // RWKV_HEAD) == (ci // RWKV_HEAD)
    m_strict = same & ((ri % c) > (ci % c))
    m_incl = same & ((ri % c) >= (ci % c))
    eye = ri == ci
    zeros = jnp.zeros((PAIR, PAIR), BF16)

    def bd(t):
        return jnp.where(same, jnp.concatenate([t, t], axis=0), 0.0)

    pairs = range(N_PAIRS)
    sls = [slice(PAIR * p, PAIR * (p + 1)) for p in pairs]
    at_bd = [bd(at_all[:, sl]) for sl in sls]
    rt_bd = [bd(rt_all[:, sl]) for sl in sls]
    v_bd = [bd(v_all[:, sl]).astype(BF16) for sl in sls]
    bk_bd = [jnp.concatenate([bd(bh_all[:, sl]), bd(kh_all[:, sl])], axis=0).astype(BF16) for sl in sls]

    quad = [_dot_nt(jnp.concatenate([at_bd[p], rt_bd[p]], axis=0),
                    jnp.concatenate([bt_all[:, sls[p]]] * 2 + [kt_all[:, sls[p]]] * 2, axis=0))
            for p in pairs]
    a_ab = [jnp.where(m_strict, q[:PAIR, :PAIR], 0.0).astype(BF16) for q in quad]
    a_ak = [jnp.where(m_strict, q[:PAIR, PAIR:], 0.0).astype(BF16) for q in quad]
    a_r = [jnp.concatenate([jnp.where(m_incl, q[PAIR:, :PAIR], 0.0),
                            jnp.where(m_incl, q[PAIR:, PAIR:], 0.0)], axis=1).astype(BF16) for q in quad]

    x = [jnp.concatenate([at_bd[p], _dot(a_ak[p], v_bd[p])], axis=1) for p in pairs]
    pw = a_ab
    x = [x[p] + _dot(pw[p], x[p]) for p in pairs]
    for _ in range(int(math.log2(c)) - 1):
        pw = [_dot(pw[p], pw[p]).astype(BF16) for p in pairs]
        x = [x[p] + _dot(pw[p], x[p]) for p in pairs]

    rhs = [jnp.concatenate([x[p].astype(BF16), jnp.concatenate([zeros, v_bd[p]], axis=1)], axis=0)
           for p in pairs]
    top = [_dot(a_r[p], rhs[p]) for p in pairs]
    bot = [_dot_tn(bk_bd[p], rhs[p]) for p in pairs]

    for p in pairs:
        diag = jnp.where(eye, jnp.broadcast_to(p_end[:, sls[p]], (PAIR, PAIR)), 0.0)
        lhs = jnp.concatenate([rt_bd[p] + top[p][:, :PAIR], bot[p][:, :PAIR] + diag], axis=0)
        fin = _dot(lhs, s_scr[p])
        y_bd = fin[:PAIR] + top[p][:, PAIR:]
        y_ref[0, :, sls[p]] = y_bd[:c] + y_bd[c:]
        s_scr[p] = fin[PAIR:] + bot[p][:, PAIR:]


def _chunk(r, lw, k, v, kk, kka):
    bsz, seq, _ = r.shape
    spec = pl.BlockSpec((1, CHUNK, D_MODEL), lambda b, c: (b, c, 0))
    return pl.pallas_call(
        _chunk_kernel,
        grid=(bsz, seq // CHUNK),
        in_specs=[spec] * 6,
        out_specs=spec,
        out_shape=jax.ShapeDtypeStruct((bsz, seq, D_MODEL), F32),
        scratch_shapes=[pltpu.VMEM((N_PAIRS, PAIR, PAIR), F32)],
        compiler_params=_params("arbitrary", "arbitrary"),
        name="rwkv_chunk",
    )(r, lw, k, v, kk, kka)


def _attn_kernel(q_ref, kc_ref, kp_ref, vc_ref, vp_ref, o_ref, l_ref, *, dilation, slopes):
    n = pl.program_id(2)
    blk = ATTN_BLK
    ii = lax.broadcasted_iota(jnp.int32, (blk, blk), 0)
    jj = lax.broadcasted_iota(jnp.int32, (blk, blk), 1)
    d_cur = ii - jj
    valid_c = d_cur >= 0
    valid_p = (d_cur <= 0) & (n > 0)
    dist_c = d_cur.astype(F32) * float(dilation)
    dist_p = (d_cur + blk).astype(F32) * float(dilation)
    lane_head = jj // ATTN_HEAD
    neg_inf = float("-inf")

    for pp in range(ATTN_GROUP_WIDTH // PAIR):
        sl = slice(PAIR * pp, PAIR * (pp + 1))
        q2 = q_ref[0, 0, :, sl] * (ATTN_HEAD ** -0.5)
        kc, kp = kc_ref[0, 0, :, sl], kp_ref[0, 0, :, sl]
        vc, vp = vc_ref[0, 0, :, sl], vp_ref[0, 0, :, sl]
        o_pair = jnp.zeros((blk, PAIR), F32)
        l_pair = jnp.zeros((blk, PAIR), F32)
        for hh in range(2):
            slope = float(slopes[2 * pp + hh])
            qm = jnp.where(lane_head == hh, q2, jnp.zeros_like(q2))
            s_c = jnp.where(valid_c, _dot_nt(qm, kc) - slope * dist_c, neg_inf)
            s_p = jnp.where(valid_p, _dot_nt(qm, kp) - slope * dist_p, neg_inf)
            m = jnp.maximum(jnp.max(s_c, axis=-1, keepdims=True), jnp.max(s_p, axis=-1, keepdims=True))
            p_c = jnp.exp(s_c - m)
            p_p = jnp.exp(s_p - m)
            den = jnp.sum(p_c, axis=-1, keepdims=True) + jnp.sum(p_p, axis=-1, keepdims=True)
            o = (_dot(p_c, vc) + _dot(p_p, vp)) / den
            o_pair = jnp.where(lane_head == hh, o, o_pair)
            l_pair = jnp.where(lane_head == hh, m + jnp.log(den), l_pair)
        o_ref[0, 0, :, sl] = o_pair
        l_ref[0, 0, :, sl] = l_pair


def _attn_group(qkv, dilation, slopes):
    bsz, _, sub, _ = qkv.shape
    blk = (1, 1, ATTN_BLK, ATTN_GROUP_WIDTH)
    cur = lambda col: pl.BlockSpec(blk, lambda b, r, n: (b, r, n, col))
    prev = lambda col: pl.BlockSpec(blk, lambda b, r, n: (b, r, jnp.maximum(n - 1, 0), col))
    out = pl.BlockSpec(blk, lambda b, r, n: (b, r, n, 0))
    shape = jax.ShapeDtypeStruct((bsz, dilation, sub, ATTN_GROUP_WIDTH), F32)
    return pl.pallas_call(
        functools.partial(_attn_kernel, dilation=dilation, slopes=tuple(slopes)),
        grid=(bsz, dilation, sub // ATTN_BLK),
        in_specs=[cur(0), cur(1), prev(1), cur(2), prev(2)],
        out_specs=[out, out],
        out_shape=[shape, shape],
        compiler_params=_params("arbitrary", "arbitrary", "arbitrary"),
        name=f"dilated_attn_d{dilation}",
    )(qkv, qkv, qkv, qkv, qkv)


def _mix_kernel(y_ref, bonus_ref, g_ref, za_ref, zb_ref, x_ref, o0_ref, o1_ref, o2_ref,
                l0_ref, l1_ref, l2_ref, gt_ref, lnw_ref, lnb_ref, gpost_ref,
                wor_ref, woa_ref, wout_ref, e_ref, out_ref, *scratch):
    e = e_ref[...]
    inv_n = 1.0 / RWKV_HEAD
    y = y_ref[...]
    mean = _seg_sum(y, e) * inv_n
    yc = y - mean
    var = _seg_sum(yc * yc, e) * inv_n
    yn = yc * lax.rsqrt(var + GN_EPS) * lnw_ref[...] + lnb_ref[...]
    y_rwkv = _dot((yn + bonus_ref[...]) * g_ref[...], wor_ref[...])

    def token_order(ref, scr):
        dilation, rows = ref.shape[1], ref.shape[2]
        if dilation == 1:
            return ref[0, 0]
        for r in range(dilation):
            for cg in range(scr.shape[0]):
                scr[cg, pl.ds(r, rows, stride=dilation), :] = ref[0, r, :, LANES * cg:LANES * (cg + 1)]
        return jnp.concatenate([scr[cg] for cg in range(scr.shape[0])], axis=1)

    o0, l0 = token_order(o0_ref, None), token_order(l0_ref, None)
    o1, l1 = token_order(o1_ref, scratch[0]), token_order(l1_ref, scratch[1])
    o2, l2 = token_order(o2_ref, scratch[2]), token_order(l2_ref, scratch[3])
    mx = jnp.maximum(jnp.maximum(l0, l1), l2)
    e0, e1, e2 = jnp.exp(l0 - mx), jnp.exp(l1 - mx), jnp.exp(l2 - mx)
    o = (e0 * o0 + e1 * o1 + e2 * o2) / (e0 + e1 + e2)
    y_attn = _dot(o, woa_ref[...])

    mixed = _dot(jax.nn.sigmoid(za_ref[...]) * y_rwkv + jax.nn.sigmoid(zb_ref[...]) * y_attn,
                 wout_ref[...])
    out_ref[...] = x_ref[...] + gt_ref[0] * (_rms(mixed) * gpost_ref[...])


def _mix(y, bonus, g, proj, x2, o_l, gt, ln_w, ln_b, g_post, w_or, w_oa, w_out, e_seg, seq, tm):
    n_tok = x2.shape[0]
    tpb = seq // tm
    row = lambda i: (i, 0)
    const = lambda i: (0, 0)
    tok = pl.BlockSpec((tm, D_MODEL), row)
    vec = pl.BlockSpec((1, D_MODEL), const)

    def att(dilation):
        return pl.BlockSpec((1, dilation, tm // dilation, ATTN_GROUP_WIDTH),
                            lambda i: (i // tpb, 0, i % tpb, 0))

    (o0, l0), (o1, l1), (o2, l2) = o_l
    d0, d1, d2 = [d for _, d in ATTN_GROUPS]
    assert d0 == 1
    return pl.pallas_call(
        _mix_kernel,
        grid=(n_tok // tm,),
        in_specs=[tok, tok, tok,
                  pl.BlockSpec((tm, D_MODEL), lambda i: (i, COL_ZA // D_MODEL)),
                  pl.BlockSpec((tm, D_MODEL), lambda i: (i, COL_ZB // D_MODEL)),
                  tok, att(d0), att(d1), att(d2), att(d0), att(d1), att(d2),
                  pl.BlockSpec((1, 1, D_MODEL), lambda i: (i // tpb, 0, 0)),
                  vec, vec, vec,
                  pl.BlockSpec((D_MODEL, D_MODEL), const),
                  pl.BlockSpec((ATTN_GROUP_WIDTH, D_MODEL), const),
                  pl.BlockSpec((D_MODEL, D_MODEL), const),
                  pl.BlockSpec((D_MODEL, D_MODEL), const)],
        out_specs=tok,
        out_shape=jax.ShapeDtypeStruct((n_tok, D_MODEL), F32),
        scratch_shapes=[pltpu.VMEM((ATTN_GROUP_WIDTH // LANES, tm, LANES), F32)] * 4,
        compiler_params=_params("arbitrary"),
        name="mix_out",
    )(y, bonus, g, proj, proj, x2, o0, o1, o2, l0, l1, l2, gt, ln_w, ln_b, g_post,
      w_or, w_oa, w_out, e_seg)


def _ffn_kernel(x_ref, sc_ref, sh_ref, gt_ref, gpre_ref, gpost_ref, wg_ref, wu_ref, wo_ref,
                out_ref, h_scr, acc_scr):
    j = pl.program_id(1)

    @pl.when(j == 0)
    def _():
        h = _rms(x_ref[...]) * gpre_ref[...] * (1.0 + sc_ref[0]) + sh_ref[0]
        h_scr[...] = h.astype(BF16)
        acc_scr[...] = jnp.zeros_like(acc_scr)

    h = h_scr[...]
    u_gate = jnp.dot(h, wg_ref[...], preferred_element_type=F32)
    u_up = jnp.dot(h, wu_ref[...], preferred_element_type=F32)
    acc_scr[...] += _dot(u_gate * jax.nn.sigmoid(u_gate) * u_up, wo_ref[...])

    @pl.when(j == pl.num_programs(1) - 1)
    def _():
        out_ref[...] = x_ref[...] + gt_ref[0] * (_rms(acc_scr[...]) * gpost_ref[...])


def _ffn(x1, sc, sh, gt, g_pre, g_post, w_in, w_out, seq, tm, th):
    n_tok = x1.shape[0]
    tpb = seq // tm
    n_h = FFN_HIDDEN // th
    row = lambda i, j: (i, 0)
    const = lambda i, j: (0, 0)
    modrow = lambda i, j: (i // tpb, 0, 0)
    tok = pl.BlockSpec((tm, D_MODEL), row)
    mod = pl.BlockSpec((1, 1, D_MODEL), modrow)
    vec = pl.BlockSpec((1, D_MODEL), const)
    return pl.pallas_call(
        _ffn_kernel,
        grid=(n_tok // tm, n_h),
        in_specs=[tok, mod, mod, mod, vec, vec,
                  pl.BlockSpec((D_MODEL, th), lambda i, j: (0, j)),
                  pl.BlockSpec((D_MODEL, th), lambda i, j: (0, n_h + j)),
                  pl.BlockSpec((th, D_MODEL), lambda i, j: (j, 0))],
        out_specs=tok,
        out_shape=jax.ShapeDtypeStruct((n_tok, D_MODEL), F32),
        scratch_shapes=[pltpu.VMEM((tm, D_MODEL), BF16), pltpu.VMEM((tm, D_MODEL), F32)],
        compiler_params=_params("arbitrary", "arbitrary"),
        name="swiglu_ffn",
    )(x1, sc, sh, gt, g_pre, g_post, w_in, w_in, w_out)


def _pad_cols(w, n):
    return jnp.pad(w, ((0, 0), (0, n - w.shape[1]))).astype(BF16)


def _pad_rows(w, n):
    return jnp.pad(w, ((0, n - w.shape[0]), (0, 0))).astype(BF16)


def kernel(x, c, w_mod, b_mod, g_pre_mix, g_post_mix, g_pre_ffn, g_post_ffn, w_in, mu_rkv, mu_lora,
           w0, w1, w2, a0, a1, a2, g1, g2, k_k, k_a, r_k, ln_x_w, ln_x_b, w_o_rwkv, w_o_attn, w_out,
           w_ffn_in, w_ffn_out):
    bsz, seq, _ = x.shape
    depth = w_mod.shape[0]
    assert seq % (ATTN_BLK * ATTN_GROUPS[-1][1]) == 0
    n_tok = bsz * seq
    tm_proj = min(1024, seq)
    tm_aproj = min(512, seq)
    tm_prep = min(256, seq)
    tm_mix = min(256, seq)
    tm_ffn = min(512, seq)
    th_ffn = FFN_HIDDEN // 2

    slopes = _alibi_slopes(len(ATTN_GROUPS) * ATTN_HEADS_PER_GROUP)
    seg = jnp.arange(D_MODEL) // RWKV_HEAD
    e_seg = (seg[:, None] == seg[None, :]).astype(BF16)
    row = lambda t: t.reshape(1, -1)

    x2 = x.reshape(n_tok, D_MODEL)
    for l in range(depth):
        mod = _mod(c, w_mod[l], b_mod[l])
        sh_m, sc_m, gt_m, sh_f, sc_f, gt_f = [mod[:, i * D_MODEL:(i + 1) * D_MODEL].reshape(bsz, 1, D_MODEL)
                                               for i in range(6)]
        wl = w_in[l]
        rkv_cols = 3 * D_MODEL
        att_cols = 3 * ATTN_WIDTH
        w_main = jnp.concatenate([wl[:, :rkv_cols], wl[:, rkv_cols + att_cols:]], axis=1).astype(BF16)

        proj, h, lw, la, lg = _inproj(
            x2, sc_m, sh_m, row(g_pre_mix[l]), mu_lora[l],
            _pad_cols(w1[l], LORA_DECAY_PAD), _pad_cols(a1[l], LORA_ICLR_PAD),
            _pad_cols(g1[l], LORA_GATE_PAD), w_main, seq, tm_proj)

        r, lwd, k2, v, kk, kka, g, bonus = _prep(
            proj, lw, la, lg, mu_rkv[l], row(w0[l]), row(a0[l]), row(k_k[l]), row(k_a[l]), row(r_k[l]),
            _pad_rows(w2[l], LORA_DECAY_PAD), _pad_rows(a2[l], LORA_ICLR_PAD),
            _pad_rows(g2[l], LORA_GATE_PAD), e_seg, seq, tm_prep)

        b3 = lambda t: t.reshape(bsz, seq, D_MODEL)
        y = _chunk(b3(r), b3(lwd), b3(k2), b3(v), b3(kk), b3(kka)).reshape(n_tok, D_MODEL)

        o_l = []
        for gi, (window, dilation) in enumerate(ATTN_GROUPS):
            assert window // dilation == ATTN_BLK
            gsl = lambda which: wl[:, rkv_cols + which * ATTN_WIDTH + gi * ATTN_GROUP_WIDTH:
                                   rkv_cols + which * ATTN_WIDTH + (gi + 1) * ATTN_GROUP_WIDTH]
            w_qkv = jnp.concatenate([gsl(0), gsl(1), gsl(2)], axis=1).astype(BF16)
            qkv = _attn_proj(h, w_qkv, dilation, bsz, seq, tm_aproj)
            o_l.append(_attn_group(
                qkv, dilation, slopes[gi * ATTN_HEADS_PER_GROUP:(gi + 1) * ATTN_HEADS_PER_GROUP]))

        x2 = _mix(y, bonus, g, proj, x2, o_l, gt_m, row(ln_x_w[l]), row(ln_x_b[l]), row(g_post_mix[l]),
                  w_o_rwkv[l].astype(BF16), w_o_attn[l].astype(BF16), w_out[l].astype(BF16), e_seg,
                  seq, tm_mix)

        x2 = _ffn(x2, sc_f, sh_f, gt_f, row(g_pre_ffn[l]), row(g_post_ffn[l]),
                  w_ffn_in[l].astype(BF16), w_ffn_out[l].astype(BF16), seq, tm_ffn, th_ffn)
    return x2.reshape(bsz, seq, D_MODEL)
```

```python
import functools
import math

import jax
import jax.numpy as jnp
from jax import lax
from jax.experimental import pallas as pl
from jax.experimental.pallas import tpu as pltpu

F32 = jnp.float32
BF16 = jnp.bfloat16
HIGHEST = lax.Precision.HIGHEST

LANES = 128
SEG_TILE = 256
D_MODEL = 1024
RWKV_HEAD = 64
GN_EPS = 64e-5
RMS_EPS = 1e-6
ATTN_HEAD = 64
ATTN_HEADS_PER_GROUP = 8
ATTN_GROUPS = ((128, 1), (512, 4), (2048, 16))
ATTN_GROUP_WIDTH = ATTN_HEADS_PER_GROUP * ATTN_HEAD
ATTN_WIDTH = len(ATTN_GROUPS) * ATTN_GROUP_WIDTH
ATTN_BLK = 128
FFN_HIDDEN = 2816
LORA_DECAY_PAD = 128
LORA_ICLR_PAD = 128
LORA_GATE_PAD = 256
CHUNK = 64
PAIR = 2 * RWKV_HEAD
N_PAIRS = D_MODEL // PAIR

COL_R, COL_K, COL_V, COL_ZA, COL_ZB = 0, 1024, 2048, 3072, 4096
MAIN_COLS = 5120
PROJ_TILE = 512

VMEM_LIMIT = 48 * 1024 * 1024


def _alibi_slopes(n):
    def pow2(m):
        start = 2.0 ** (-8.0 / m)
        return [start ** (i + 1) for i in range(m)]
    if math.log2(n).is_integer():
        s = pow2(n)
    else:
        p = 2 ** int(math.floor(math.log2(n)))
        s = pow2(p) + pow2(2 * p)[0::2][: n - p]
    return sorted(s, reverse=True)


def _dot(a, b):
    return jnp.dot(a.astype(BF16), b.astype(BF16), preferred_element_type=F32)


def _dot_nt(a, b):
    return lax.dot_general(a.astype(BF16), b.astype(BF16), (((1,), (1,)), ((), ())),
                           preferred_element_type=F32)


def _dot_tn(a, b):
    return lax.dot_general(a.astype(BF16), b.astype(BF16), (((0,), (0,)), ((), ())),
                           preferred_element_type=F32)


def _seg_sum(x, e):
    hi = x.astype(BF16)
    lo = (x - hi.astype(F32)).astype(BF16)
    outs = []
    for t in range(x.shape[1] // SEG_TILE):
        sl = slice(SEG_TILE * t, SEG_TILE * (t + 1))
        outs.append(jnp.dot(hi[:, sl], e, preferred_element_type=F32)
                    + jnp.dot(lo[:, sl], e, preferred_element_type=F32))
    return jnp.concatenate(outs, axis=1)


def _shift_rows(t, prev_row):
    rows = lax.broadcasted_iota(jnp.int32, t.shape, 0)
    return jnp.where(rows == 0, prev_row, pltpu.roll(t, 1, 0))


def _rms(t):
    return t * lax.rsqrt(jnp.mean(t * t, axis=-1, keepdims=True) + RMS_EPS)


def _params(*semantics):
    return pltpu.CompilerParams(dimension_semantics=semantics, vmem_limit_bytes=VMEM_LIMIT)


def _mod_kernel(c_ref, w_ref, b_ref, o_ref):
    o_ref[...] = jnp.dot(c_ref[...], w_ref[...], preferred_element_type=F32,
                         precision=HIGHEST) + b_ref[...]


def _mod(c, w_mod, b_mod):
    bsz = c.shape[0]
    n_out = w_mod.shape[1]
    return pl.pallas_call(
        _mod_kernel,
        grid=(n_out // D_MODEL,),
        in_specs=[pl.BlockSpec((bsz, D_MODEL), lambda j: (0, 0)),
                  pl.BlockSpec((D_MODEL, D_MODEL), lambda j: (0, j)),
                  pl.BlockSpec((1, D_MODEL), lambda j: (0, j))],
        out_specs=pl.BlockSpec((bsz, D_MODEL), lambda j: (0, j)),
        out_shape=jax.ShapeDtypeStruct((bsz, n_out), F32),
        compiler_params=_params("arbitrary"),
        name="adaln_mod",
    )(c, w_mod, b_mod.reshape(1, n_out))


def _inproj_kernel(x_ref, sc_ref, sh_ref, g_ref, mu_ref, w1_ref, a1_ref, g1_ref, win_ref,
                   p_ref, h_ref, lw_ref, la_ref, lg_ref, last_scr, *, tiles_per_batch):
    i = pl.program_id(0)
    j = pl.program_id(1)
    tm = x_ref.shape[0]

    @pl.when(j == 0)
    def _():
        @pl.when(i % tiles_per_batch == 0)
        def _():
            last_scr[...] = jnp.zeros_like(last_scr)

        h = _rms(x_ref[...]) * g_ref[...] * (1.0 + sc_ref[0]) + sh_ref[0]
        hs = _shift_rows(h, last_scr[...])
        last_scr[...] = h[tm - 1:tm, :]
        dh = hs - h
        lw_ref[...] = _dot(h + dh * mu_ref[0:1, :], w1_ref[...])
        la_ref[...] = _dot(h + dh * mu_ref[1:2, :], a1_ref[...])
        lg_ref[...] = _dot(h + dh * mu_ref[2:3, :], g1_ref[...])
        h_ref[...] = h.astype(BF16)

    p_ref[...] = jnp.dot(h_ref[...], win_ref[...], preferred_element_type=F32)


def _inproj(x2, sc, sh, g_pre, mu_lora, w1p, a1p, g1p, w_main, seq, tm):
    n_tok = x2.shape[0]
    tpb = seq // tm
    row = lambda i, j: (i, 0)
    const = lambda i, j: (0, 0)
    modrow = lambda i, j: (i // tpb, 0, 0)
    return pl.pallas_call(
        functools.partial(_inproj_kernel, tiles_per_batch=tpb),
        grid=(n_tok // tm, MAIN_COLS // PROJ_TILE),
        in_specs=[pl.BlockSpec((tm, D_MODEL), row),
                  pl.BlockSpec((1, 1, D_MODEL), modrow),
                  pl.BlockSpec((1, 1, D_MODEL), modrow),
                  pl.BlockSpec((1, D_MODEL), const),
                  pl.BlockSpec((3, D_MODEL), const),
                  pl.BlockSpec((D_MODEL, LORA_DECAY_PAD), const),
                  pl.BlockSpec((D_MODEL, LORA_ICLR_PAD), const),
                  pl.BlockSpec((D_MODEL, LORA_GATE_PAD), const),
                  pl.BlockSpec((D_MODEL, PROJ_TILE), lambda i, j: (0, j))],
        out_specs=[pl.BlockSpec((tm, PROJ_TILE), lambda i, j: (i, j)),
                   pl.BlockSpec((tm, D_MODEL), row),
                   pl.BlockSpec((tm, LORA_DECAY_PAD), row),
                   pl.BlockSpec((tm, LORA_ICLR_PAD), row),
                   pl.BlockSpec((tm, LORA_GATE_PAD), row)],
        out_shape=[jax.ShapeDtypeStruct((n_tok, MAIN_COLS), F32),
                   jax.ShapeDtypeStruct((n_tok, D_MODEL), BF16),
                   jax.ShapeDtypeStruct((n_tok, LORA_DECAY_PAD), F32),
                   jax.ShapeDtypeStruct((n_tok, LORA_ICLR_PAD), F32),
                   jax.ShapeDtypeStruct((n_tok, LORA_GATE_PAD), F32)],
        scratch_shapes=[pltpu.VMEM((1, D_MODEL), F32)],
        compiler_params=_params("arbitrary", "arbitrary"),
        name="in_proj",
    )(x2, sc, sh, g_pre, mu_lora, w1p, a1p, g1p, w_main)


def _attn_proj_kernel(h_ref, w_ref, o_ref, *scratch, dilation):
    acc = jnp.dot(h_ref[...], w_ref[...], preferred_element_type=F32)
    if dilation == 1:
        o_ref[0, 0] = acc.astype(BF16)
    else:
        scr, = scratch
        rows = acc.shape[0] // dilation
        for cg in range(scr.shape[0]):
            scr[cg] = acc[:, LANES * cg:LANES * (cg + 1)]
        for r in range(dilation):
            for cg in range(scr.shape[0]):
                o_ref[0, r, :, LANES * cg:LANES * (cg + 1)] = (
                    scr[cg, pl.ds(r, rows, stride=dilation), :].astype(BF16))


def _attn_proj(h, w_qkv, dilation, bsz, seq, tm):
    tpb = seq // tm
    width = 3 * ATTN_GROUP_WIDTH
    return pl.pallas_call(
        functools.partial(_attn_proj_kernel, dilation=dilation),
        grid=(bsz * tpb,),
        in_specs=[pl.BlockSpec((tm, D_MODEL), lambda i: (i, 0)),
                  pl.BlockSpec((D_MODEL, width), lambda i: (0, 0))],
        out_specs=pl.BlockSpec((1, dilation, tm // dilation, width), lambda i: (i // tpb, 0, i % tpb, 0)),
        out_shape=jax.ShapeDtypeStruct((bsz, dilation, seq // dilation, width), BF16),
        scratch_shapes=[] if dilation == 1 else [pltpu.VMEM((width // LANES, tm, LANES), F32)],
        compiler_params=_params("arbitrary"),
        name=f"attn_proj_d{dilation}",
    )(h, w_qkv)


def _prep_kernel(pr_ref, pk_ref, pv_ref, lw_ref, la_ref, lg_ref, mu_ref, w0_ref, a0_ref, kk_ref,
                 ka_ref, rk_ref, w2_ref, a2_ref, g2_ref, e_ref,
                 r_o, lw_o, k_o, v_o, kk_o, kka_o, g_o, bonus_o, last_scr, *, tiles_per_batch):
    i = pl.program_id(0)
    tm = pr_ref.shape[0]

    @pl.when(i % tiles_per_batch == 0)
    def _():
        last_scr[...] = jnp.zeros_like(last_scr)

    def mix(p_ref, idx):
        p = p_ref[...]
        ps = _shift_rows(p, last_scr[idx:idx + 1, :])
        last_scr[idx:idx + 1, :] = p[tm - 1:tm, :]
        return p + (ps - p) * mu_ref[idx:idx + 1, :]

    r = mix(pr_ref, 0)
    k = mix(pk_ref, 1)
    v = mix(pv_ref, 2)
    e = e_ref[...]

    z = w0_ref[...] + _dot(jnp.tanh(lw_ref[...]), w2_ref[...])
    w_log = -(jnp.maximum(-z, 0.0) + jnp.log(1.0 + jnp.exp(-jnp.abs(z)))) - 0.5
    lw_o[...] = -jnp.exp(w_log)
    a = jax.nn.sigmoid(a0_ref[...] + _dot(la_ref[...], a2_ref[...]))
    g_o[...] = _dot(jax.nn.sigmoid(lg_ref[...]), g2_ref[...])

    kk = k * kk_ref[...]
    kk = kk / jnp.maximum(jnp.sqrt(_seg_sum(kk * kk, e)), 1e-12)
    k2 = k * (1.0 + (a - 1.0) * ka_ref[...])
    r_o[...] = r
    k_o[...] = k2
    v_o[...] = v
    kk_o[...] = kk
    kka_o[...] = kk * a
    bonus_o[...] = _seg_sum(r * k2 * rk_ref[...], e) * v


def _prep(proj, lw, la, lg, mu_rkv, w0, a0, k_k, k_a, r_k, w2p, a2p, g2p, e_seg, seq, tm):
    n_tok = proj.shape[0]
    tpb = seq // tm
    row = lambda i: (i, 0)
    const = lambda i: (0, 0)
    vec = pl.BlockSpec((1, D_MODEL), const)
    tok = pl.BlockSpec((tm, D_MODEL), row)
    return pl.pallas_call(
        functools.partial(_prep_kernel, tiles_per_batch=tpb),
        grid=(n_tok // tm,),
        in_specs=[pl.BlockSpec((tm, D_MODEL), lambda i: (i, COL_R // D_MODEL)),
                  pl.BlockSpec((tm, D_MODEL), lambda i: (i, COL_K // D_MODEL)),
                  pl.BlockSpec((tm, D_MODEL), lambda i: (i, COL_V // D_MODEL)),
                  pl.BlockSpec((tm, LORA_DECAY_PAD), row),
                  pl.BlockSpec((tm, LORA_ICLR_PAD), row),
                  pl.BlockSpec((tm, LORA_GATE_PAD), row),
                  pl.BlockSpec((3, D_MODEL), const),
                  vec, vec, vec, vec, vec,
                  pl.BlockSpec((LORA_DECAY_PAD, D_MODEL), const),
                  pl.BlockSpec((LORA_ICLR_PAD, D_MODEL), const),
                  pl.BlockSpec((LORA_GATE_PAD, D_MODEL), const),
                  pl.BlockSpec((SEG_TILE, SEG_TILE), const)],
        out_specs=[tok] * 8,
        out_shape=[jax.ShapeDtypeStruct((n_tok, D_MODEL), F32)] * 8,
        scratch_shapes=[pltpu.VMEM((3, D_MODEL), F32)],
        compiler_params=_params("arbitrary"),
        name="rwkv_prep",
    )(proj, proj, proj, lw, la, lg, mu_rkv, w0, a0, k_k, k_a, r_k, w2p, a2p, g2p, e_seg)


def _chunk_kernel(r_ref, lw_ref, k_ref, v_ref, kk_ref, kka_ref, y_ref, s_scr):
    @pl.when(pl.program_id(1) == 0)
    def _():
        s_scr[...] = jnp.zeros_like(s_scr)

    c = CHUNK
    lw = lw_ref[0]
    ti = lax.broadcasted_iota(jnp.int32, (c, c), 0)
    tj = lax.broadcasted_iota(jnp.int32, (c, c), 1)
    cum = jnp.dot((ti >= tj).astype(F32), lw, preferred_element_type=F32, precision=HIGHEST)
    cum_prev = cum - lw
    cum_end = cum[c - 1:c, :]
    kk = kk_ref[0]
    kka = kka_ref[0]
    k = k_ref[0]
    e_neg = jnp.exp(-cum)
    e_end = jnp.exp(cum_end - cum)
    at_all = -kk * jnp.exp(cum_prev)
    rt_all = r_ref[0] * jnp.exp(cum)
    bt_all = kka * e_neg
    kt_all = k * e_neg
    bh_all = kka * e_end
    kh_all = k * e_end
    p_end = jnp.exp(cum_end)
    v_all = v_ref[0]

    ri = lax.broadcasted_iota(jnp.int32, (PAIR, PAIR), 0)
    ci = lax.broadcasted_iota(jnp.int32, (PAIR, PAIR), 1)
    same = (ri // RWKV_HEAD) == (ci // RWKV_HEAD)
    same2 = jnp.concatenate([same, same], axis=1)
    eye = ri == ci
    tt = lax.broadcasted_iota(jnp.int32, (c, PAIR), 0)
    ss = lax.broadcasted_iota(jnp.int32, (c, PAIR), 1) % c
    m_strict = tt > ss
    m_incl = tt >= ss
    eye_rows = (tt == ss).astype(F32)
    zeros_bd = jnp.zeros((PAIR, PAIR), BF16)
    zeros_c = jnp.zeros((c, PAIR), F32)

    def bd(t):
        mask = same if t.shape[1] == PAIR else same2
        return jnp.where(mask, jnp.concatenate([t, t], axis=0), 0.0).astype(BF16)

    pairs = range(N_PAIRS)
    sls = [slice(PAIR * p, PAIR * (p + 1)) for p in pairs]
    at = [at_all[:, sl] for sl in sls]
    rt = [rt_all[:, sl] for sl in sls]
    v_bd = [bd(v_all[:, sl]) for sl in sls]

    quad = [_dot_nt(jnp.concatenate([at[p], rt[p]], axis=0),
                    jnp.concatenate([bd(bt_all[:, sls[p]]), bd(kt_all[:, sls[p]])], axis=0))
            for p in pairs]
    a_ab = [jnp.where(m_strict, q[:c, :PAIR], 0.0) for q in quad]
    a_ak = [jnp.where(m_strict, q[:c, PAIR:], 0.0) for q in quad]
    a_r = [jnp.concatenate([jnp.where(m_incl, q[c:, :PAIR], 0.0),
                            jnp.where(m_incl, q[c:, PAIR:], 0.0)], axis=1).astype(BF16) for q in quad]
    x0 = [jnp.concatenate([at[p], _dot(a_ak[p], v_bd[p])], axis=1) for p in pairs]

    t_inv = [eye_rows + a_ab[p] for p in pairs]
    pw = a_ab
    for _ in range(int(math.log2(c)) - 1):
        pw = [_dot(pw[p], bd(pw[p])) for p in pairs]
        t_inv = [t_inv[p] + _dot(pw[p], bd(t_inv[p])) for p in pairs]
    x = [_dot(t_inv[p], bd(x0[p])) for p in pairs]

    top = [_dot(a_r[p], jnp.concatenate([bd(x[p]), jnp.concatenate([zeros_bd, v_bd[p]], axis=1)], axis=0))
           for p in pairs]
    bot = [_dot_tn(jnp.concatenate([bh_all[:, sls[p]], kh_all[:, sls[p]]], axis=0),
                   jnp.concatenate([x[p], jnp.concatenate([zeros_c, v_all[:, sls[p]]], axis=1)], axis=0))
           for p in pairs]

    for p in pairs:
        m_mat = jnp.where(same, bot[p][:, :PAIR], 0.0) + jnp.where(
            eye, jnp.broadcast_to(p_end[:, sls[p]], (PAIR, PAIR)), 0.0)
        lhs = jnp.concatenate([rt[p] + top[p][:, :PAIR], m_mat], axis=0)
        fin = _dot(lhs, s_scr[p])
        y_ref[0, :, sls[p]] = fin[:c] + top[p][:, PAIR:]
        s_scr[p] = fin[c:] + jnp.where(same, bot[p][:, PAIR:], 0.0)


def _chunk(r, lw, k, v, kk, kka):
    bsz, seq, _ = r.shape
    spec = pl.BlockSpec((1, CHUNK, D_MODEL), lambda b, c: (b, c, 0))
    return pl.pallas_call(
        _chunk_kernel,
        grid=(bsz, seq // CHUNK),
        in_specs=[spec] * 6,
        out_specs=spec,
        out_shape=jax.ShapeDtypeStruct((bsz, seq, D_MODEL), F32),
        scratch_shapes=[pltpu.VMEM((N_PAIRS, PAIR, PAIR), F32)],
        compiler_params=_params("arbitrary", "arbitrary"),
        name="rwkv_chunk",
    )(r, lw, k, v, kk, kka)


def _attn_kernel(q_ref, kc_ref, kp_ref, vc_ref, vp_ref, o_ref, l_ref, *, dilation, slopes):
    n = pl.program_id(2)
    blk = ATTN_BLK
    ii = lax.broadcasted_iota(jnp.int32, (blk, 2 * blk), 0)
    jj = lax.broadcasted_iota(jnp.int32, (blk, 2 * blk), 1)
    delta = ii - jj + blk
    valid = (delta >= 0) & (delta <= blk) & ((n > 0) | (jj >= blk))
    dist = delta.astype(F32) * float(dilation)
    lane_head = lax.broadcasted_iota(jnp.int32, (blk, PAIR), 1) // ATTN_HEAD
    neg_inf = float("-inf")

    pairs = range(ATTN_GROUP_WIDTH // PAIR)
    heads = [(pp, hh) for pp in pairs for hh in range(2)]
    sls = [slice(PAIR * pp, PAIR * (pp + 1)) for pp in pairs]
    q2 = [q_ref[0, 0, :, sl] * (ATTN_HEAD ** -0.5) for sl in sls]
    kcat = [jnp.concatenate([kp_ref[0, 0, :, sl], kc_ref[0, 0, :, sl]], axis=0) for sl in sls]
    vcat = [jnp.concatenate([vp_ref[0, 0, :, sl], vc_ref[0, 0, :, sl]], axis=0) for sl in sls]
    s = [jnp.where(valid,
                   _dot_nt(jnp.where(lane_head == hh, q2[pp], jnp.zeros_like(q2[pp])), kcat[pp])
                   - float(slopes[2 * pp + hh]) * dist, neg_inf) for pp, hh in heads]
    m = [jnp.max(t, axis=-1, keepdims=True) for t in s]
    p = [jnp.exp(s[i] - m[i]) for i in range(len(heads))]
    den = [jnp.sum(t, axis=-1, keepdims=True) for t in p]
    o = [_dot(p[i], vcat[pp]) * (1.0 / den[i]) for i, (pp, _) in enumerate(heads)]
    lse = [m[i] + jnp.log(den[i]) for i in range(len(heads))]
    for pp in pairs:
        o_ref[0, 0, :, sls[pp]] = jnp.where(lane_head == 0, o[2 * pp], o[2 * pp + 1])
        l_ref[0, 0, :, sls[pp]] = jnp.where(lane_head == 0, lse[2 * pp], lse[2 * pp + 1])


def _attn_group(qkv, dilation, slopes):
    bsz, _, sub, _ = qkv.shape
    blk = (1, 1, ATTN_BLK, ATTN_GROUP_WIDTH)
    cur = lambda col: pl.BlockSpec(blk, lambda b, r, n: (b, r, n, col))
    prev = lambda col: pl.BlockSpec(blk, lambda b, r, n: (b, r, jnp.maximum(n - 1, 0), col))
    out = pl.BlockSpec(blk, lambda b, r, n: (b, r, n, 0))
    shape = jax.ShapeDtypeStruct((bsz, dilation, sub, ATTN_GROUP_WIDTH), F32)
    return pl.pallas_call(
        functools.partial(_attn_kernel, dilation=dilation, slopes=tuple(slopes)),
        grid=(bsz, dilation, sub // ATTN_BLK),
        in_specs=[cur(0), cur(1), prev(1), cur(2), prev(2)],
        out_specs=[out, out],
        out_shape=[shape, shape],
        compiler_params=_params("arbitrary", "arbitrary", "arbitrary"),
        name=f"dilated_attn_d{dilation}",
    )(qkv, qkv, qkv, qkv, qkv)


def _mix_kernel(y_ref, bonus_ref, g_ref, za_ref, zb_ref, x_ref, o0_ref, o1_ref, o2_ref,
                l0_ref, l1_ref, l2_ref, gt_ref, lnw_ref, lnb_ref, gpost_ref,
                wor_ref, woa_ref, wout_ref, e_ref, out_ref, *scratch):
    e = e_ref[...]
    inv_n = 1.0 / RWKV_HEAD
    y = y_ref[...]
    mean = _seg_sum(y, e) * inv_n
    yc = y - mean
    var = _seg_sum(yc * yc, e) * inv_n
    yn = yc * lax.rsqrt(var + GN_EPS) * lnw_ref[...] + lnb_ref[...]
    y_rwkv = _dot((yn + bonus_ref[...]) * g_ref[...], wor_ref[...])

    def token_order(ref, scr):
        dilation, rows = ref.shape[1], ref.shape[2]
        if dilation == 1:
            return ref[0, 0]
        for r in range(dilation):
            for cg in range(scr.shape[0]):
                scr[cg, pl.ds(r, rows, stride=dilation), :] = ref[0, r, :, LANES * cg:LANES * (cg + 1)]
        return jnp.concatenate([scr[cg] for cg in range(scr.shape[0])], axis=1)

    o0, l0 = token_order(o0_ref, None), token_order(l0_ref, None)
    o1, l1 = token_order(o1_ref, scratch[0]), token_order(l1_ref, scratch[1])
    o2, l2 = token_order(o2_ref, scratch[2]), token_order(l2_ref, scratch[3])
    mx = jnp.maximum(jnp.maximum(l0, l1), l2)
    e0, e1, e2 = jnp.exp(l0 - mx), jnp.exp(l1 - mx), jnp.exp(l2 - mx)
    o = (e0 * o0 + e1 * o1 + e2 * o2) / (e0 + e1 + e2)
    y_attn = _dot(o, woa_ref[...])

    mixed = _dot(jax.nn.sigmoid(za_ref[...]) * y_rwkv + jax.nn.sigmoid(zb_ref[...]) * y_attn,
                 wout_ref[...])
    out_ref[...] = x_ref[...] + gt_ref[0] * (_rms(mixed) * gpost_ref[...])


def _mix(y, bonus, g, proj, x2, o_l, gt, ln_w, ln_b, g_post, w_or, w_oa, w_out, e_seg, seq, tm):
    n_tok = x2.shape[0]
    tpb = seq // tm
    row = lambda i: (i, 0)
    const = lambda i: (0, 0)
    tok = pl.BlockSpec((tm, D_MODEL), row)
    vec = pl.BlockSpec((1, D_MODEL), const)

    def att(dilation):
        return pl.BlockSpec((1, dilation, tm // dilation, ATTN_GROUP_WIDTH),
                            lambda i: (i // tpb, 0, i % tpb, 0))

    (o0, l0), (o1, l1), (o2, l2) = o_l
    d0, d1, d2 = [d for _, d in ATTN_GROUPS]
    assert d0 == 1
    return pl.pallas_call(
        _mix_kernel,
        grid=(n_tok // tm,),
        in_specs=[tok, tok, tok,
                  pl.BlockSpec((tm, D_MODEL), lambda i: (i, COL_ZA // D_MODEL)),
                  pl.BlockSpec((tm, D_MODEL), lambda i: (i, COL_ZB // D_MODEL)),
                  tok, att(d0), att(d1), att(d2), att(d0), att(d1), att(d2),
                  pl.BlockSpec((1, 1, D_MODEL), lambda i: (i // tpb, 0, 0)),
                  vec, vec, vec,
                  pl.BlockSpec((D_MODEL, D_MODEL), const),
                  pl.BlockSpec((ATTN_GROUP_WIDTH, D_MODEL), const),
                  pl.BlockSpec((D_MODEL, D_MODEL), const),
                  pl.BlockSpec((SEG_TILE, SEG_TILE), const)],
        out_specs=tok,
        out_shape=jax.ShapeDtypeStruct((n_tok, D_MODEL), F32),
        scratch_shapes=[pltpu.VMEM((ATTN_GROUP_WIDTH // LANES, tm, LANES), F32)] * 4,
        compiler_params=_params("arbitrary"),
        name="mix_out",
    )(y, bonus, g, proj, proj, x2, o0, o1, o2, l0, l1, l2, gt, ln_w, ln_b, g_post,
      w_or, w_oa, w_out, e_seg)


def _ffn_kernel(x_ref, sc_ref, sh_ref, gt_ref, gpre_ref, gpost_ref, wg_ref, wu_ref, wo_ref,
                out_ref, h_scr, acc_scr):
    j = pl.program_id(1)

    @pl.when(j == 0)
    def _():
        h = _rms(x_ref[...]) * gpre_ref[...] * (1.0 + sc_ref[0]) + sh_ref[0]
        h_scr[...] = h.astype(BF16)
        acc_scr[...] = jnp.zeros_like(acc_scr)

    h = h_scr[...]
    u_gate = jnp.dot(h, wg_ref[...], preferred_element_type=F32)
    u_up = jnp.dot(h, wu_ref[...], preferred_element_type=F32)
    acc_scr[...] += _dot(u_gate * jax.nn.sigmoid(u_gate) * u_up, wo_ref[...])

    @pl.when(j == pl.num_programs(1) - 1)
    def _():
        out_ref[...] = x_ref[...] + gt_ref[0] * (_rms(acc_scr[...]) * gpost_ref[...])


def _ffn(x1, sc, sh, gt, g_pre, g_post, w_in, w_out, seq, tm, th):
    n_tok = x1.shape[0]
    tpb = seq // tm
    n_h = FFN_HIDDEN // th
    row = lambda i, j: (i, 0)
    const = lambda i, j: (0, 0)
    modrow = lambda i, j: (i // tpb, 0, 0)
    tok = pl.BlockSpec((tm, D_MODEL), row)
    mod = pl.BlockSpec((1, 1, D_MODEL), modrow)
    vec = pl.BlockSpec((1, D_MODEL), const)
    return pl.pallas_call(
        _ffn_kernel,
        grid=(n_tok // tm, n_h),
        in_specs=[tok, mod, mod, mod, vec, vec,
                  pl.BlockSpec((D_MODEL, th), lambda i, j: (0, j)),
                  pl.BlockSpec((D_MODEL, th), lambda i, j: (0, n_h + j)),
                  pl.BlockSpec((th, D_MODEL), lambda i, j: (j, 0))],
        out_specs=tok,
        out_shape=jax.ShapeDtypeStruct((n_tok, D_MODEL), F32),
        scratch_shapes=[pltpu.VMEM((tm, D_MODEL), BF16), pltpu.VMEM((tm, D_MODEL), F32)],
        compiler_params=_params("arbitrary", "arbitrary"),
        name="swiglu_ffn",
    )(x1, sc, sh, gt, g_pre, g_post, w_in, w_in, w_out)


def _pad_cols(w, n):
    return jnp.pad(w, ((0, 0), (0, n - w.shape[1]))).astype(BF16)


def _pad_rows(w, n):
    return jnp.pad(w, ((0, n - w.shape[0]), (0, 0))).astype(BF16)


def kernel(x, c, w_mod, b_mod, g_pre_mix, g_post_mix, g_pre_ffn, g_post_ffn, w_in, mu_rkv, mu_lora,
           w0, w1, w2, a0, a1, a2, g1, g2, k_k, k_a, r_k, ln_x_w, ln_x_b, w_o_rwkv, w_o_attn, w_out,
           w_ffn_in, w_ffn_out):
    bsz, seq, _ = x.shape
    depth = w_mod.shape[0]
    assert seq % (ATTN_BLK * ATTN_GROUPS[-1][1]) == 0
    n_tok = bsz * seq
    tm_proj = min(1024, seq)
    tm_aproj = min(512, seq)
    tm_prep = min(256, seq)
    tm_mix = min(256, seq)
    tm_ffn = min(512, seq)
    th_ffn = FFN_HIDDEN // 2

    slopes = _alibi_slopes(len(ATTN_GROUPS) * ATTN_HEADS_PER_GROUP)
    seg = jnp.arange(SEG_TILE) // RWKV_HEAD
    e_seg = (seg[:, None] == seg[None, :]).astype(BF16)
    row = lambda t: t.reshape(1, -1)

    x2 = x.reshape(n_tok, D_MODEL)
    for l in range(depth):
        mod = _mod(c, w_mod[l], b_mod[l])
        sh_m, sc_m, gt_m, sh_f, sc_f, gt_f = [mod[:, i * D_MODEL:(i + 1) * D_MODEL].reshape(bsz, 1, D_MODEL)
                                               for i in range(6)]
        wl = w_in[l]
        rkv_cols = 3 * D_MODEL
        att_cols = 3 * ATTN_WIDTH
        w_main = jnp.concatenate([wl[:, :rkv_cols], wl[:, rkv_cols + att_cols:]], axis=1).astype(BF16)

        proj, h, lw, la, lg = _inproj(
            x2, sc_m, sh_m, row(g_pre_mix[l]), mu_lora[l],
            _pad_cols(w1[l], LORA_DECAY_PAD), _pad_cols(a1[l], LORA_ICLR_PAD),
            _pad_cols(g1[l], LORA_GATE_PAD), w_main, seq, tm_proj)

        r, lwd, k2, v, kk, kka, g, bonus = _prep(
            proj, lw, la, lg, mu_rkv[l], row(w0[l]), row(a0[l]), row(k_k[l]), row(k_a[l]), row(r_k[l]),
            _pad_rows(w2[l], LORA_DECAY_PAD), _pad_rows(a2[l], LORA_ICLR_PAD),
            _pad_rows(g2[l], LORA_GATE_PAD), e_seg, seq, tm_prep)

        b3 = lambda t: t.reshape(bsz, seq, D_MODEL)
        y = _chunk(b3(r), b3(lwd), b3(k2), b3(v), b3(kk), b3(kka)).reshape(n_tok, D_MODEL)

        o_l = []
        for gi, (window, dilation) in enumerate(ATTN_GROUPS):
            assert window // dilation == ATTN_BLK
            gsl = lambda which: wl[:, rkv_cols + which * ATTN_WIDTH + gi * ATTN_GROUP_WIDTH:
                                   rkv_cols + which * ATTN_WIDTH + (gi + 1) * ATTN_GROUP_WIDTH]
            w_qkv = jnp.concatenate([gsl(0), gsl(1), gsl(2)], axis=1).astype(BF16)
            qkv = _attn_proj(h, w_qkv, dilation, bsz, seq, tm_aproj)
            o_l.append(_attn_group(
                qkv, dilation, slopes[gi * ATTN_HEADS_PER_GROUP:(gi + 1) * ATTN_HEADS_PER_GROUP]))

        x2 = _mix(y, bonus, g, proj, x2, o_l, gt_m, row(ln_x_w[l]), row(ln_x_b[l]), row(g_post_mix[l]),
                  w_o_rwkv[l].astype(BF16), w_o_attn[l].astype(BF16), w_out[l].astype(BF16), e_seg,
                  seq, tm_mix)

        x2 = _ffn(x2, sc_f, sh_f, gt_f, row(g_pre_ffn[l]), row(g_post_ffn[l]),
                  w_ffn_in[l].astype(BF16), w_ffn_out[l].astype(BF16), seq, tm_ffn, th_ffn)
    return x2.reshape(bsz, seq, D_MODEL)
```

```python
import functools
import math

import jax
import jax.numpy as jnp
from jax import lax
from jax.experimental import pallas as pl
from jax.experimental.pallas import tpu as pltpu

F32 = jnp.float32
BF16 = jnp.bfloat16
HIGHEST = lax.Precision.HIGHEST

LANES = 128
SEG_TILE = 256
D_MODEL = 1024
RWKV_HEAD = 64
GN_EPS = 64e-5
RMS_EPS = 1e-6
ATTN_HEAD = 64
ATTN_HEADS_PER_GROUP = 8
ATTN_GROUPS = ((128, 1), (512, 4), (2048, 16))
ATTN_GROUP_WIDTH = ATTN_HEADS_PER_GROUP * ATTN_HEAD
ATTN_WIDTH = len(ATTN_GROUPS) * ATTN_GROUP_WIDTH
ATTN_BLK = 128
FFN_HIDDEN = 2816
LORA_DECAY_PAD = 128
LORA_ICLR_PAD = 128
LORA_GATE_PAD = 256
CHUNK = 64
PAIR = 2 * RWKV_HEAD
N_PAIRS = D_MODEL // PAIR

COL_R, COL_K, COL_V, COL_ZA, COL_ZB = 0, 1024, 2048, 3072, 4096
MAIN_COLS = 5120
PROJ_TILE = 1280

VMEM_LIMIT = 48 * 1024 * 1024


def _alibi_slopes(n):
    def pow2(m):
        start = 2.0 ** (-8.0 / m)
        return [start ** (i + 1) for i in range(m)]
    if math.log2(n).is_integer():
        s = pow2(n)
    else:
        p = 2 ** int(math.floor(math.log2(n)))
        s = pow2(p) + pow2(2 * p)[0::2][: n - p]
    return sorted(s, reverse=True)


def _dot(a, b):
    return jnp.dot(a.astype(BF16), b.astype(BF16), preferred_element_type=F32)


def _dot_nt(a, b):
    return lax.dot_general(a.astype(BF16), b.astype(BF16), (((1,), (1,)), ((), ())),
                           preferred_element_type=F32)


def _dot_tn(a, b):
    return lax.dot_general(a.astype(BF16), b.astype(BF16), (((0,), (0,)), ((), ())),
                           preferred_element_type=F32)


def _seg_sum(x, e):
    hi = x.astype(BF16)
    lo = (x - hi.astype(F32)).astype(BF16)
    outs = []
    for t in range(x.shape[1] // SEG_TILE):
        sl = slice(SEG_TILE * t, SEG_TILE * (t + 1))
        outs.append(jnp.dot(hi[:, sl], e, preferred_element_type=F32)
                    + jnp.dot(lo[:, sl], e, preferred_element_type=F32))
    return jnp.concatenate(outs, axis=1)


def _shift_rows(t, prev_row):
    rows = lax.broadcasted_iota(jnp.int32, t.shape, 0)
    return jnp.where(rows == 0, prev_row, pltpu.roll(t, 1, 0))


def _rms(t):
    return t * lax.rsqrt(jnp.mean(t * t, axis=-1, keepdims=True) + RMS_EPS)


def _params(*semantics):
    return pltpu.CompilerParams(dimension_semantics=semantics, vmem_limit_bytes=VMEM_LIMIT)


def _mod_kernel(c_ref, w_ref, b_ref, o_ref):
    o_ref[...] = jnp.dot(c_ref[...], w_ref[...], preferred_element_type=F32,
                         precision=HIGHEST) + b_ref[...]


def _mod(c, w_mod, b_mod):
    bsz = c.shape[0]
    n_out = w_mod.shape[1]
    return pl.pallas_call(
        _mod_kernel,
        grid=(n_out // D_MODEL,),
        in_specs=[pl.BlockSpec((bsz, D_MODEL), lambda j: (0, 0)),
                  pl.BlockSpec((D_MODEL, D_MODEL), lambda j: (0, j)),
                  pl.BlockSpec((1, D_MODEL), lambda j: (0, j))],
        out_specs=pl.BlockSpec((bsz, D_MODEL), lambda j: (0, j)),
        out_shape=jax.ShapeDtypeStruct((bsz, n_out), F32),
        compiler_params=_params("arbitrary"),
        name="adaln_mod",
    )(c, w_mod, b_mod.reshape(1, n_out))


def _inproj_kernel(x_ref, sc_ref, sh_ref, g_ref, mu_ref, w1_ref, a1_ref, g1_ref, win_ref,
                   p_ref, h_ref, lw_ref, la_ref, lg_ref, last_scr, *, tiles_per_batch):
    i = pl.program_id(0)
    j = pl.program_id(1)
    tm = x_ref.shape[0]

    @pl.when(j == 0)
    def _():
        @pl.when(i % tiles_per_batch == 0)
        def _():
            last_scr[...] = jnp.zeros_like(last_scr)

        h = _rms(x_ref[...]) * g_ref[...] * (1.0 + sc_ref[0]) + sh_ref[0]
        hs = _shift_rows(h, last_scr[...])
        last_scr[...] = h[tm - 1:tm, :]
        dh = hs - h
        lw_ref[...] = _dot(h + dh * mu_ref[0:1, :], w1_ref[...])
        la_ref[...] = _dot(h + dh * mu_ref[1:2, :], a1_ref[...])
        lg_ref[...] = _dot(h + dh * mu_ref[2:3, :], g1_ref[...])
        h_ref[...] = h.astype(BF16)

    p_ref[...] = jnp.dot(h_ref[...], win_ref[...], preferred_element_type=F32)


def _inproj(x2, sc, sh, g_pre, mu_lora, w1p, a1p, g1p, w_main, seq, tm):
    n_tok = x2.shape[0]
    tpb = seq // tm
    row = lambda i, j: (i, 0)
    const = lambda i, j: (0, 0)
    modrow = lambda i, j: (i // tpb, 0, 0)
    return pl.pallas_call(
        functools.partial(_inproj_kernel, tiles_per_batch=tpb),
        grid=(n_tok // tm, MAIN_COLS // PROJ_TILE),
        in_specs=[pl.BlockSpec((tm, D_MODEL), row),
                  pl.BlockSpec((1, 1, D_MODEL), modrow),
                  pl.BlockSpec((1, 1, D_MODEL), modrow),
                  pl.BlockSpec((1, D_MODEL), const),
                  pl.BlockSpec((3, D_MODEL), const),
                  pl.BlockSpec((D_MODEL, LORA_DECAY_PAD), const),
                  pl.BlockSpec((D_MODEL, LORA_ICLR_PAD), const),
                  pl.BlockSpec((D_MODEL, LORA_GATE_PAD), const),
                  pl.BlockSpec((D_MODEL, PROJ_TILE), lambda i, j: (0, j))],
        out_specs=[pl.BlockSpec((tm, PROJ_TILE), lambda i, j: (i, j)),
                   pl.BlockSpec((tm, D_MODEL), row),
                   pl.BlockSpec((tm, LORA_DECAY_PAD), row),
                   pl.BlockSpec((tm, LORA_ICLR_PAD), row),
                   pl.BlockSpec((tm, LORA_GATE_PAD), row)],
        out_shape=[jax.ShapeDtypeStruct((n_tok, MAIN_COLS), F32),
                   jax.ShapeDtypeStruct((n_tok, D_MODEL), BF16),
                   jax.ShapeDtypeStruct((n_tok, LORA_DECAY_PAD), F32),
                   jax.ShapeDtypeStruct((n_tok, LORA_ICLR_PAD), F32),
                   jax.ShapeDtypeStruct((n_tok, LORA_GATE_PAD), F32)],
        scratch_shapes=[pltpu.VMEM((1, D_MODEL), F32)],
        compiler_params=_params("arbitrary", "arbitrary"),
        name="in_proj",
    )(x2, sc, sh, g_pre, mu_lora, w1p, a1p, g1p, w_main)


def _attn_proj_kernel(h_ref, w_ref, o_ref, *scratch, dilation):
    acc = jnp.dot(h_ref[...], w_ref[...], preferred_element_type=F32)
    if dilation == 1:
        o_ref[0, 0] = acc.astype(BF16)
    else:
        scr, = scratch
        rows = acc.shape[0] // dilation
        for cg in range(scr.shape[0]):
            scr[cg] = acc[:, LANES * cg:LANES * (cg + 1)]
        for r in range(dilation):
            for cg in range(scr.shape[0]):
                o_ref[0, r, :, LANES * cg:LANES * (cg + 1)] = (
                    scr[cg, pl.ds(r, rows, stride=dilation), :].astype(BF16))


def _attn_proj(h, w_qkv, dilation, bsz, seq, tm):
    tpb = seq // tm
    width = 3 * ATTN_GROUP_WIDTH
    return pl.pallas_call(
        functools.partial(_attn_proj_kernel, dilation=dilation),
        grid=(bsz * tpb,),
        in_specs=[pl.BlockSpec((tm, D_MODEL), lambda i: (i, 0)),
                  pl.BlockSpec((D_MODEL, width), lambda i: (0, 0))],
        out_specs=pl.BlockSpec((1, dilation, tm // dilation, width), lambda i: (i // tpb, 0, i % tpb, 0)),
        out_shape=jax.ShapeDtypeStruct((bsz, dilation, seq // dilation, width), BF16),
        scratch_shapes=[] if dilation == 1 else [pltpu.VMEM((width // LANES, tm, LANES), F32)],
        compiler_params=_params("arbitrary"),
        name=f"attn_proj_d{dilation}",
    )(h, w_qkv)


def _prep_kernel(pr_ref, pk_ref, pv_ref, lw_ref, la_ref, lg_ref, mu_ref, w0_ref, a0_ref, kk_ref,
                 ka_ref, rk_ref, w2_ref, a2_ref, g2_ref, e_ref,
                 r_o, lw_o, k_o, v_o, kk_o, kka_o, g_o, bonus_o, last_scr, *, tiles_per_batch):
    i = pl.program_id(0)
    tm = pr_ref.shape[0]

    @pl.when(i % tiles_per_batch == 0)
    def _():
        last_scr[...] = jnp.zeros_like(last_scr)

    def mix(p_ref, idx):
        p = p_ref[...]
        ps = _shift_rows(p, last_scr[idx:idx + 1, :])
        last_scr[idx:idx + 1, :] = p[tm - 1:tm, :]
        return p + (ps - p) * mu_ref[idx:idx + 1, :]

    r = mix(pr_ref, 0)
    k = mix(pk_ref, 1)
    v = mix(pv_ref, 2)
    e = e_ref[...]

    z = w0_ref[...] + _dot(jnp.tanh(lw_ref[...]), w2_ref[...])
    w_log = -(jnp.maximum(-z, 0.0) + jnp.log(1.0 + jnp.exp(-jnp.abs(z)))) - 0.5
    lw_o[...] = -jnp.exp(w_log)
    a = jax.nn.sigmoid(a0_ref[...] + _dot(la_ref[...], a2_ref[...]))
    g_o[...] = _dot(jax.nn.sigmoid(lg_ref[...]), g2_ref[...]).astype(BF16)

    kk = k * kk_ref[...]
    kk = kk / jnp.maximum(jnp.sqrt(_seg_sum(kk * kk, e)), 1e-12)
    k2 = k * (1.0 + (a - 1.0) * ka_ref[...])
    r_o[...] = r.astype(BF16)
    k_o[...] = k2.astype(BF16)
    v_o[...] = v.astype(BF16)
    kk_o[...] = kk.astype(BF16)
    kka_o[...] = (kk * a).astype(BF16)
    bonus_o[...] = (_seg_sum(r * k2 * rk_ref[...], e) * v).astype(BF16)


def _prep(proj, lw, la, lg, mu_rkv, w0, a0, k_k, k_a, r_k, w2p, a2p, g2p, e_seg, seq, tm):
    n_tok = proj.shape[0]
    tpb = seq // tm
    row = lambda i: (i, 0)
    const = lambda i: (0, 0)
    vec = pl.BlockSpec((1, D_MODEL), const)
    tok = pl.BlockSpec((tm, D_MODEL), row)
    return pl.pallas_call(
        functools.partial(_prep_kernel, tiles_per_batch=tpb),
        grid=(n_tok // tm,),
        in_specs=[pl.BlockSpec((tm, D_MODEL), lambda i: (i, COL_R // D_MODEL)),
                  pl.BlockSpec((tm, D_MODEL), lambda i: (i, COL_K // D_MODEL)),
                  pl.BlockSpec((tm, D_MODEL), lambda i: (i, COL_V // D_MODEL)),
                  pl.BlockSpec((tm, LORA_DECAY_PAD), row),
                  pl.BlockSpec((tm, LORA_ICLR_PAD), row),
                  pl.BlockSpec((tm, LORA_GATE_PAD), row),
                  pl.BlockSpec((3, D_MODEL), const),
                  vec, vec, vec, vec, vec,
                  pl.BlockSpec((LORA_DECAY_PAD, D_MODEL), const),
                  pl.BlockSpec((LORA_ICLR_PAD, D_MODEL), const),
                  pl.BlockSpec((LORA_GATE_PAD, D_MODEL), const),
                  pl.BlockSpec((SEG_TILE, SEG_TILE), const)],
        out_specs=[tok] * 8,
        out_shape=[jax.ShapeDtypeStruct((n_tok, D_MODEL), F32 if name == "lw" else BF16)
                   for name in ("r", "lw", "k", "v", "kk", "kka", "g", "bonus")],
        scratch_shapes=[pltpu.VMEM((3, D_MODEL), F32)],
        compiler_params=_params("arbitrary"),
        name="rwkv_prep",
    )(proj, proj, proj, lw, la, lg, mu_rkv, w0, a0, k_k, k_a, r_k, w2p, a2p, g2p, e_seg)


def _chunk_kernel(r_ref, lw_ref, k_ref, v_ref, kk_ref, kka_ref, y_ref, s_scr):
    @pl.when(pl.program_id(1) == 0)
    def _():
        s_scr[...] = jnp.zeros_like(s_scr)

    c = CHUNK
    lw = lw_ref[0]
    ti = lax.broadcasted_iota(jnp.int32, (c, c), 0)
    tj = lax.broadcasted_iota(jnp.int32, (c, c), 1)
    cum = jnp.dot((ti >= tj).astype(F32), lw, preferred_element_type=F32, precision=HIGHEST)
    cum_prev = cum - lw
    cum_end = cum[c - 1:c, :]
    kk = kk_ref[0].astype(F32)
    kka = kka_ref[0].astype(F32)
    k = k_ref[0].astype(F32)
    e_neg = jnp.exp(-cum)
    e_end = jnp.exp(cum_end - cum)
    at_all = -kk * jnp.exp(cum_prev)
    rt_all = r_ref[0].astype(F32) * jnp.exp(cum)
    bt_all = kka * e_neg
    kt_all = k * e_neg
    bh_all = kka * e_end
    kh_all = k * e_end
    p_end = jnp.exp(cum_end)
    v_all = v_ref[0].astype(F32)

    ri = lax.broadcasted_iota(jnp.int32, (PAIR, PAIR), 0)
    ci = lax.broadcasted_iota(jnp.int32, (PAIR, PAIR), 1)
    same = (ri // RWKV_HEAD) == (ci // RWKV_HEAD)
    same2 = jnp.concatenate([same, same], axis=1)
    eye = ri == ci
    tt = lax.broadcasted_iota(jnp.int32, (c, PAIR), 0)
    ss = lax.broadcasted_iota(jnp.int32, (c, PAIR), 1) % c
    m_strict = tt > ss
    m_incl = tt >= ss
    eye_rows = (tt == ss).astype(F32)
    zeros_bd = jnp.zeros((PAIR, PAIR), BF16)
    zeros_c = jnp.zeros((c, PAIR), F32)

    def bd(t):
        mask = same if t.shape[1] == PAIR else same2
        return jnp.where(mask, jnp.concatenate([t, t], axis=0), 0.0).astype(BF16)

    pairs = range(N_PAIRS)
    sls = [slice(PAIR * p, PAIR * (p + 1)) for p in pairs]
    at = [at_all[:, sl] for sl in sls]
    rt = [rt_all[:, sl] for sl in sls]
    v_bd = [bd(v_all[:, sl]) for sl in sls]

    quad = [_dot_nt(jnp.concatenate([at[p], rt[p]], axis=0),
                    jnp.concatenate([bd(bt_all[:, sls[p]]), bd(kt_all[:, sls[p]])], axis=0))
            for p in pairs]
    a_ab = [jnp.where(m_strict, q[:c, :PAIR], 0.0) for q in quad]
    a_ak = [jnp.where(m_strict, q[:c, PAIR:], 0.0) for q in quad]
    a_r = [jnp.concatenate([jnp.where(m_incl, q[c:, :PAIR], 0.0),
                            jnp.where(m_incl, q[c:, PAIR:], 0.0)], axis=1).astype(BF16) for q in quad]
    x0 = [jnp.concatenate([at[p], _dot(a_ak[p], v_bd[p])], axis=1) for p in pairs]

    t_inv = [eye_rows + a_ab[p] for p in pairs]
    pw = a_ab
    for _ in range(int(math.log2(c)) - 1):
        pw = [_dot(pw[p], bd(pw[p])) for p in pairs]
        t_inv = [t_inv[p] + _dot(pw[p], bd(t_inv[p])) for p in pairs]
    x = [_dot(t_inv[p], bd(x0[p])) for p in pairs]

    top = [_dot(a_r[p], jnp.concatenate([bd(x[p]), jnp.concatenate([zeros_bd, v_bd[p]], axis=1)], axis=0))
           for p in pairs]
    bot = [_dot_tn(jnp.concatenate([bh_all[:, sls[p]], kh_all[:, sls[p]]], axis=0),
                   jnp.concatenate([x[p], jnp.concatenate([zeros_c, v_all[:, sls[p]]], axis=1)], axis=0))
           for p in pairs]

    for p in pairs:
        m_mat = jnp.where(same, bot[p][:, :PAIR], 0.0) + jnp.where(
            eye, jnp.broadcast_to(p_end[:, sls[p]], (PAIR, PAIR)), 0.0)
        lhs = jnp.concatenate([rt[p] + top[p][:, :PAIR], m_mat], axis=0)
        fin = _dot(lhs, s_scr[p])
        y_ref[0, :, sls[p]] = fin[:c] + top[p][:, PAIR:]
        s_scr[p] = fin[c:] + jnp.where(same, bot[p][:, PAIR:], 0.0)


def _chunk(r, lw, k, v, kk, kka):
    bsz, seq, _ = r.shape
    spec = pl.BlockSpec((1, CHUNK, D_MODEL), lambda b, c: (b, c, 0))
    return pl.pallas_call(
        _chunk_kernel,
        grid=(bsz, seq // CHUNK),
        in_specs=[spec] * 6,
        out_specs=spec,
        out_shape=jax.ShapeDtypeStruct((bsz, seq, D_MODEL), F32),
        scratch_shapes=[pltpu.VMEM((N_PAIRS, PAIR, PAIR), F32)],
        compiler_params=_params("arbitrary", "arbitrary"),
        name="rwkv_chunk",
    )(r, lw, k, v, kk, kka)


def _attn_kernel(cur_ref, prev_ref, o_ref, l_ref, *, dilation, slopes):
    n = pl.program_id(2)
    blk = ATTN_BLK
    gw = ATTN_GROUP_WIDTH
    ii = lax.broadcasted_iota(jnp.int32, (blk, 2 * blk), 0)
    jj = lax.broadcasted_iota(jnp.int32, (blk, 2 * blk), 1)
    delta = ii - jj + blk
    in_window = (delta >= 0) & (delta <= blk)
    dist = delta.astype(F32) * float(dilation)
    lane_head = lax.broadcasted_iota(jnp.int32, (blk, PAIR), 1) // ATTN_HEAD
    lane = lax.broadcasted_iota(jnp.int32, (blk, LANES), 1)
    neg_inf = float("-inf")
    pairs = range(gw // PAIR)
    heads = [(pp, hh) for pp in pairs for hh in range(2)]
    sls = [slice(PAIR * pp, PAIR * (pp + 1)) for pp in pairs]

    for sb in range(2):
        rows = slice(blk * sb, blk * (sb + 1))
        if sb == 0:
            valid = in_window & ((n > 0) | (jj >= blk))
            kcat = [jnp.concatenate([prev_ref[0, 0, :, sl], cur_ref[0, 0, rows, sl]], axis=0) for sl in sls]
            vcat = [jnp.concatenate([prev_ref[0, 0, :, gw + PAIR * pp:gw + PAIR * (pp + 1)],
                                     cur_ref[0, 0, rows, gw + PAIR * pp:gw + PAIR * (pp + 1)]], axis=0)
                    for pp in pairs]
        else:
            valid = in_window
            kcat = [cur_ref[0, 0, :, sl] for sl in sls]
            vcat = [cur_ref[0, 0, :, gw + PAIR * pp:gw + PAIR * (pp + 1)] for pp in pairs]
        q2 = [cur_ref[0, 0, rows, 2 * gw + PAIR * pp:2 * gw + PAIR * (pp + 1)] * (ATTN_HEAD ** -0.5)
              for pp in pairs]
        s = [jnp.where(valid,
                       _dot_nt(jnp.where(lane_head == hh, q2[pp], jnp.zeros_like(q2[pp])), kcat[pp])
                       - float(slopes[2 * pp + hh]) * dist, neg_inf) for pp, hh in heads]
        m = [jnp.max(t, axis=-1, keepdims=True) for t in s]
        p = [jnp.exp(s[i] - m[i]) for i in range(len(heads))]
        den = [jnp.sum(t, axis=-1, keepdims=True) for t in p]
        o = [_dot(p[i], vcat[pp]) * (1.0 / den[i]) for i, (pp, _) in enumerate(heads)]
        for pp in pairs:
            o_ref[0, 0, rows, sls[pp]] = jnp.where(lane_head == 0, o[2 * pp], o[2 * pp + 1]).astype(BF16)
        lse = jnp.zeros((blk, LANES), F32)
        for i in range(len(heads)):
            lse = jnp.where(lane == i, m[i] + jnp.log(den[i]), lse)
        l_ref[0, 0, rows, :] = lse


def _attn_group(kvq, dilation, slopes):
    bsz, _, sub, _ = kvq.shape
    gw = ATTN_GROUP_WIDTH
    step = 2 * ATTN_BLK
    return pl.pallas_call(
        functools.partial(_attn_kernel, dilation=dilation, slopes=tuple(slopes)),
        grid=(bsz, dilation, sub // step),
        in_specs=[pl.BlockSpec((1, 1, step, 3 * gw), lambda b, r, n: (b, r, n, 0)),
                  pl.BlockSpec((1, 1, ATTN_BLK, 2 * gw), lambda b, r, n: (b, r, jnp.maximum(2 * n - 1, 0), 0))],
        out_specs=[pl.BlockSpec((1, 1, step, gw), lambda b, r, n: (b, r, n, 0)),
                   pl.BlockSpec((1, 1, step, LANES), lambda b, r, n: (b, r, n, 0))],
        out_shape=[jax.ShapeDtypeStruct((bsz, dilation, sub, gw), BF16),
                   jax.ShapeDtypeStruct((bsz, dilation, sub, LANES), F32)],
        compiler_params=_params("arbitrary", "arbitrary", "arbitrary"),
        name=f"dilated_attn_d{dilation}",
    )(kvq, kvq)


def _mix_kernel(y_ref, bonus_ref, g_ref, za_ref, zb_ref, x_ref, o0_ref, o1_ref, o2_ref,
                l0_ref, l1_ref, l2_ref, gt_ref, lnw_ref, lnb_ref, gpost_ref,
                wor_ref, woa_ref, wout_ref, e_ref, spread_ref, out_ref, *scratch):
    e = e_ref[...]
    inv_n = 1.0 / RWKV_HEAD
    y = y_ref[...]
    mean = _seg_sum(y, e) * inv_n
    yc = y - mean
    var = _seg_sum(yc * yc, e) * inv_n
    yn = yc * lax.rsqrt(var + GN_EPS) * lnw_ref[...] + lnb_ref[...]
    y_rwkv = _dot((yn + bonus_ref[...].astype(F32)) * g_ref[...].astype(F32), wor_ref[...])

    def token_order(ref, scr):
        dilation, rows = ref.shape[1], ref.shape[2]
        if dilation == 1:
            return ref[0, 0].astype(F32)
        for r in range(dilation):
            for cg in range(ref.shape[3] // LANES):
                scr[cg, pl.ds(r, rows, stride=dilation), :] = (
                    ref[0, r, :, LANES * cg:LANES * (cg + 1)].astype(F32))
        return jnp.concatenate([scr[cg] for cg in range(ref.shape[3] // LANES)], axis=1)

    def spread(l):
        hi = l.astype(BF16)
        lo = (l - hi.astype(F32)).astype(BF16)
        return (jnp.dot(hi, spread_ref[...], preferred_element_type=F32)
                + jnp.dot(lo, spread_ref[...], preferred_element_type=F32))

    o0, l0 = token_order(o0_ref, None), spread(token_order(l0_ref, None))
    o1, l1 = token_order(o1_ref, scratch[0]), spread(token_order(l1_ref, scratch[1]))
    o2, l2 = token_order(o2_ref, scratch[2]), spread(token_order(l2_ref, scratch[3]))
    mx = jnp.maximum(jnp.maximum(l0, l1), l2)
    e0, e1, e2 = jnp.exp(l0 - mx), jnp.exp(l1 - mx), jnp.exp(l2 - mx)
    o = (e0 * o0 + e1 * o1 + e2 * o2) / (e0 + e1 + e2)
    y_attn = _dot(o, woa_ref[...])

    mixed = _dot(jax.nn.sigmoid(za_ref[...]) * y_rwkv + jax.nn.sigmoid(zb_ref[...]) * y_attn,
                 wout_ref[...])
    out_ref[...] = x_ref[...] + gt_ref[0] * (_rms(mixed) * gpost_ref[...])


def _mix(y, bonus, g, proj, x2, o_l, gt, ln_w, ln_b, g_post, w_or, w_oa, w_out, e_seg, seq, tm):
    n_tok = x2.shape[0]
    tpb = seq // tm
    row = lambda i: (i, 0)
    const = lambda i: (0, 0)
    tok = pl.BlockSpec((tm, D_MODEL), row)
    vec = pl.BlockSpec((1, D_MODEL), const)

    def att(dilation, width):
        return pl.BlockSpec((1, dilation, tm // dilation, width), lambda i: (i // tpb, 0, i % tpb, 0))

    (o0, l0), (o1, l1), (o2, l2) = o_l
    d0, d1, d2 = [d for _, d in ATTN_GROUPS]
    assert d0 == 1
    gw = ATTN_GROUP_WIDTH
    head_of_lane = jnp.arange(gw) // ATTN_HEAD
    spread = (jnp.arange(LANES)[:, None] == head_of_lane[None, :]).astype(BF16)
    return pl.pallas_call(
        _mix_kernel,
        grid=(n_tok // tm,),
        in_specs=[tok, tok, tok,
                  pl.BlockSpec((tm, D_MODEL), lambda i: (i, COL_ZA // D_MODEL)),
                  pl.BlockSpec((tm, D_MODEL), lambda i: (i, COL_ZB // D_MODEL)),
                  tok, att(d0, gw), att(d1, gw), att(d2, gw), att(d0, LANES), att(d1, LANES), att(d2, LANES),
                  pl.BlockSpec((1, 1, D_MODEL), lambda i: (i // tpb, 0, 0)),
                  vec, vec, vec,
                  pl.BlockSpec((D_MODEL, D_MODEL), const),
                  pl.BlockSpec((gw, D_MODEL), const),
                  pl.BlockSpec((D_MODEL, D_MODEL), const),
                  pl.BlockSpec((SEG_TILE, SEG_TILE), const),
                  pl.BlockSpec((LANES, gw), const)],
        out_specs=tok,
        out_shape=jax.ShapeDtypeStruct((n_tok, D_MODEL), F32),
        scratch_shapes=[pltpu.VMEM((gw // LANES, tm, LANES), F32), pltpu.VMEM((1, tm, LANES), F32)] * 2,
        compiler_params=_params("arbitrary"),
        name="mix_out",
    )(y, bonus, g, proj, proj, x2, o0, o1, o2, l0, l1, l2, gt, ln_w, ln_b, g_post,
      w_or, w_oa, w_out, e_seg, spread)


def _ffn_kernel(x_ref, sc_ref, sh_ref, gt_ref, gpre_ref, gpost_ref, wg_ref, wu_ref, wo_ref,
                out_ref, h_scr, acc_scr):
    j = pl.program_id(1)

    @pl.when(j == 0)
    def _():
        h = _rms(x_ref[...]) * gpre_ref[...] * (1.0 + sc_ref[0]) + sh_ref[0]
        h_scr[...] = h.astype(BF16)
        acc_scr[...] = jnp.zeros_like(acc_scr)

    h = h_scr[...]
    u_gate = jnp.dot(h, wg_ref[...], preferred_element_type=F32)
    u_up = jnp.dot(h, wu_ref[...], preferred_element_type=F32)
    acc_scr[...] += _dot(u_gate * jax.nn.sigmoid(u_gate) * u_up, wo_ref[...])

    @pl.when(j == pl.num_programs(1) - 1)
    def _():
        out_ref[...] = x_ref[...] + gt_ref[0] * (_rms(acc_scr[...]) * gpost_ref[...])


def _ffn(x1, sc, sh, gt, g_pre, g_post, w_in, w_out, seq, tm, th):
    n_tok = x1.shape[0]
    tpb = seq // tm
    n_h = FFN_HIDDEN // th
    row = lambda i, j: (i, 0)
    const = lambda i, j: (0, 0)
    modrow = lambda i, j: (i // tpb, 0, 0)
    tok = pl.BlockSpec((tm, D_MODEL), row)
    mod = pl.BlockSpec((1, 1, D_MODEL), modrow)
    vec = pl.BlockSpec((1, D_MODEL), const)
    return pl.pallas_call(
        _ffn_kernel,
        grid=(n_tok // tm, n_h),
        in_specs=[tok, mod, mod, mod, vec, vec,
                  pl.BlockSpec((D_MODEL, th), lambda i, j: (0, j)),
                  pl.BlockSpec((D_MODEL, th), lambda i, j: (0, n_h + j)),
                  pl.BlockSpec((th, D_MODEL), lambda i, j: (j, 0))],
        out_specs=tok,
        out_shape=jax.ShapeDtypeStruct((n_tok, D_MODEL), F32),
        scratch_shapes=[pltpu.VMEM((tm, D_MODEL), BF16), pltpu.VMEM((tm, D_MODEL), F32)],
        compiler_params=_params("arbitrary", "arbitrary"),
        name="swiglu_ffn",
    )(x1, sc, sh, gt, g_pre, g_post, w_in, w_in, w_out)


def _pad_cols(w, n):
    return jnp.pad(w, ((0, 0), (0, n - w.shape[1]))).astype(BF16)


def _pad_rows(w, n):
    return jnp.pad(w, ((0, n - w.shape[0]), (0, 0))).astype(BF16)


def kernel(x, c, w_mod, b_mod, g_pre_mix, g_post_mix, g_pre_ffn, g_post_ffn, w_in, mu_rkv, mu_lora,
           w0, w1, w2, a0, a1, a2, g1, g2, k_k, k_a, r_k, ln_x_w, ln_x_b, w_o_rwkv, w_o_attn, w_out,
           w_ffn_in, w_ffn_out):
    bsz, seq, _ = x.shape
    depth = w_mod.shape[0]
    assert seq % (2 * ATTN_BLK * ATTN_GROUPS[-1][1]) == 0
    n_tok = bsz * seq
    tm_proj = min(1024, seq)
    tm_aproj = min(512, seq)
    tm_prep = min(256, seq)
    tm_mix = min(256, seq)
    tm_ffn = min(512, seq)
    th_ffn = FFN_HIDDEN // 2

    slopes = _alibi_slopes(len(ATTN_GROUPS) * ATTN_HEADS_PER_GROUP)
    seg = jnp.arange(SEG_TILE) // RWKV_HEAD
    e_seg = (seg[:, None] == seg[None, :]).astype(BF16)
    row = lambda t: t.reshape(1, -1)

    x2 = x.reshape(n_tok, D_MODEL)
    for l in range(depth):
        mod = _mod(c, w_mod[l], b_mod[l])
        sh_m, sc_m, gt_m, sh_f, sc_f, gt_f = [mod[:, i * D_MODEL:(i + 1) * D_MODEL].reshape(bsz, 1, D_MODEL)
                                               for i in range(6)]
        wl = w_in[l]
        rkv_cols = 3 * D_MODEL
        att_cols = 3 * ATTN_WIDTH
        w_main = jnp.concatenate([wl[:, :rkv_cols], wl[:, rkv_cols + att_cols:]], axis=1).astype(BF16)

        proj, h, lw, la, lg = _inproj(
            x2, sc_m, sh_m, row(g_pre_mix[l]), mu_lora[l],
            _pad_cols(w1[l], LORA_DECAY_PAD), _pad_cols(a1[l], LORA_ICLR_PAD),
            _pad_cols(g1[l], LORA_GATE_PAD), w_main, seq, tm_proj)

        r, lwd, k2, v, kk, kka, g, bonus = _prep(
            proj, lw, la, lg, mu_rkv[l], row(w0[l]), row(a0[l]), row(k_k[l]), row(k_a[l]), row(r_k[l]),
            _pad_rows(w2[l], LORA_DECAY_PAD), _pad_rows(a2[l], LORA_ICLR_PAD),
            _pad_rows(g2[l], LORA_GATE_PAD), e_seg, seq, tm_prep)

        b3 = lambda t: t.reshape(bsz, seq, D_MODEL)
        y = _chunk(b3(r), b3(lwd), b3(k2), b3(v), b3(kk), b3(kka)).reshape(n_tok, D_MODEL)

        o_l = []
        for gi, (window, dilation) in enumerate(ATTN_GROUPS):
            assert window // dilation == ATTN_BLK
            gsl = lambda which: wl[:, rkv_cols + which * ATTN_WIDTH + gi * ATTN_GROUP_WIDTH:
                                   rkv_cols + which * ATTN_WIDTH + (gi + 1) * ATTN_GROUP_WIDTH]
            w_qkv = jnp.concatenate([gsl(1), gsl(2), gsl(0)], axis=1).astype(BF16)
            qkv = _attn_proj(h, w_qkv, dilation, bsz, seq, tm_aproj)
            o_l.append(_attn_group(
                qkv, dilation, slopes[gi * ATTN_HEADS_PER_GROUP:(gi + 1) * ATTN_HEADS_PER_GROUP]))

        x2 = _mix(y, bonus, g, proj, x2, o_l, gt_m, row(ln_x_w[l]), row(ln_x_b[l]), row(g_post_mix[l]),
                  w_o_rwkv[l].astype(BF16), w_o_attn[l].astype(BF16), w_out[l].astype(BF16), e_seg,
                  seq, tm_mix)

        x2 = _ffn(x2, sc_f, sh_f, gt_f, row(g_pre_ffn[l]), row(g_post_ffn[l]),
                  w_ffn_in[l].astype(BF16), w_ffn_out[l].astype(BF16), seq, tm_ffn, th_ffn)
    return x2.reshape(bsz, seq, D_MODEL)
```

```python
import functools
import math

import jax
import jax.numpy as jnp
from jax import lax
from jax.experimental import pallas as pl
from jax.experimental.pallas import tpu as pltpu

F32 = jnp.float32
BF16 = jnp.bfloat16
HIGHEST = lax.Precision.HIGHEST

LANES = 128
SEG_TILE = 256
D_MODEL = 1024
RWKV_HEAD = 64
GN_EPS = 64e-5
RMS_EPS = 1e-6
ATTN_HEAD = 64
ATTN_HEADS_PER_GROUP = 8
ATTN_GROUPS = ((128, 1), (512, 4), (2048, 16))
ATTN_GROUP_WIDTH = ATTN_HEADS_PER_GROUP * ATTN_HEAD
ATTN_WIDTH = len(ATTN_GROUPS) * ATTN_GROUP_WIDTH
ATTN_BLK = 128
FFN_HIDDEN = 2816
LORA_DECAY_PAD = 128
LORA_ICLR_PAD = 128
LORA_GATE_PAD = 256
CHUNK = 64
CHUNKS_PER_STEP = 2
PAIR = 2 * RWKV_HEAD
N_PAIRS = D_MODEL // PAIR

COL_R, COL_K, COL_V, COL_ZA, COL_ZB = 0, 1024, 2048, 3072, 4096
MAIN_COLS = 5120
PROJ_TILE = 1280

VMEM_LIMIT = 48 * 1024 * 1024


def _alibi_slopes(n):
    def pow2(m):
        start = 2.0 ** (-8.0 / m)
        return [start ** (i + 1) for i in range(m)]
    if math.log2(n).is_integer():
        s = pow2(n)
    else:
        p = 2 ** int(math.floor(math.log2(n)))
        s = pow2(p) + pow2(2 * p)[0::2][: n - p]
    return sorted(s, reverse=True)


def _dot(a, b):
    return jnp.dot(a.astype(BF16), b.astype(BF16), preferred_element_type=F32)


def _dot_nt(a, b):
    return lax.dot_general(a.astype(BF16), b.astype(BF16), (((1,), (1,)), ((), ())),
                           preferred_element_type=F32)


def _dot_tn(a, b):
    return lax.dot_general(a.astype(BF16), b.astype(BF16), (((0,), (0,)), ((), ())),
                           preferred_element_type=F32)


def _seg_sum(x, e):
    hi = x.astype(BF16)
    lo = (x - hi.astype(F32)).astype(BF16)
    outs = []
    for t in range(x.shape[1] // SEG_TILE):
        sl = slice(SEG_TILE * t, SEG_TILE * (t + 1))
        outs.append(jnp.dot(hi[:, sl], e, preferred_element_type=F32)
                    + jnp.dot(lo[:, sl], e, preferred_element_type=F32))
    return jnp.concatenate(outs, axis=1)


def _shift_rows(t, prev_row):
    rows = lax.broadcasted_iota(jnp.int32, t.shape, 0)
    return jnp.where(rows == 0, prev_row, pltpu.roll(t, 1, 0))


def _rms(t):
    return t * lax.rsqrt(jnp.mean(t * t, axis=-1, keepdims=True) + RMS_EPS)


def _params(*semantics):
    return pltpu.CompilerParams(dimension_semantics=semantics, vmem_limit_bytes=VMEM_LIMIT)


def _mod_kernel(c_ref, w_ref, b_ref, o_ref):
    o_ref[...] = jnp.dot(c_ref[...], w_ref[...], preferred_element_type=F32,
                         precision=HIGHEST) + b_ref[...]


def _mod(c, w_mod, b_mod):
    bsz = c.shape[0]
    n_out = w_mod.shape[1]
    return pl.pallas_call(
        _mod_kernel,
        grid=(n_out // D_MODEL,),
        in_specs=[pl.BlockSpec((bsz, D_MODEL), lambda j: (0, 0)),
                  pl.BlockSpec((D_MODEL, D_MODEL), lambda j: (0, j)),
                  pl.BlockSpec((1, D_MODEL), lambda j: (0, j))],
        out_specs=pl.BlockSpec((bsz, D_MODEL), lambda j: (0, j)),
        out_shape=jax.ShapeDtypeStruct((bsz, n_out), F32),
        compiler_params=_params("arbitrary"),
        name="adaln_mod",
    )(c, w_mod, b_mod.reshape(1, n_out))


def _inproj_kernel(x_ref, sc_ref, sh_ref, g_ref, mu_ref, w1_ref, a1_ref, g1_ref, win_ref,
                   p_ref, h_ref, lw_ref, la_ref, lg_ref, last_scr, *, tiles_per_batch):
    i = pl.program_id(0)
    j = pl.program_id(1)
    tm = x_ref.shape[0]

    @pl.when(j == 0)
    def _():
        @pl.when(i % tiles_per_batch == 0)
        def _():
            last_scr[...] = jnp.zeros_like(last_scr)

        h = _rms(x_ref[...]) * g_ref[...] * (1.0 + sc_ref[0]) + sh_ref[0]
        hs = _shift_rows(h, last_scr[...])
        last_scr[...] = h[tm - 1:tm, :]
        dh = hs - h
        lw_ref[...] = _dot(h + dh * mu_ref[0:1, :], w1_ref[...])
        la_ref[...] = _dot(h + dh * mu_ref[1:2, :], a1_ref[...])
        lg_ref[...] = _dot(h + dh * mu_ref[2:3, :], g1_ref[...])
        h_ref[...] = h.astype(BF16)

    p_ref[...] = jnp.dot(h_ref[...], win_ref[...], preferred_element_type=F32)


def _inproj(x2, sc, sh, g_pre, mu_lora, w1p, a1p, g1p, w_main, seq, tm):
    n_tok = x2.shape[0]
    tpb = seq // tm
    row = lambda i, j: (i, 0)
    const = lambda i, j: (0, 0)
    modrow = lambda i, j: (i // tpb, 0, 0)
    return pl.pallas_call(
        functools.partial(_inproj_kernel, tiles_per_batch=tpb),
        grid=(n_tok // tm, MAIN_COLS // PROJ_TILE),
        in_specs=[pl.BlockSpec((tm, D_MODEL), row),
                  pl.BlockSpec((1, 1, D_MODEL), modrow),
                  pl.BlockSpec((1, 1, D_MODEL), modrow),
                  pl.BlockSpec((1, D_MODEL), const),
                  pl.BlockSpec((3, D_MODEL), const),
                  pl.BlockSpec((D_MODEL, LORA_DECAY_PAD), const),
                  pl.BlockSpec((D_MODEL, LORA_ICLR_PAD), const),
                  pl.BlockSpec((D_MODEL, LORA_GATE_PAD), const),
                  pl.BlockSpec((D_MODEL, PROJ_TILE), lambda i, j: (0, j))],
        out_specs=[pl.BlockSpec((tm, PROJ_TILE), lambda i, j: (i, j)),
                   pl.BlockSpec((tm, D_MODEL), row),
                   pl.BlockSpec((tm, LORA_DECAY_PAD), row),
                   pl.BlockSpec((tm, LORA_ICLR_PAD), row),
                   pl.BlockSpec((tm, LORA_GATE_PAD), row)],
        out_shape=[jax.ShapeDtypeStruct((n_tok, MAIN_COLS), F32),
                   jax.ShapeDtypeStruct((n_tok, D_MODEL), BF16),
                   jax.ShapeDtypeStruct((n_tok, LORA_DECAY_PAD), F32),
                   jax.ShapeDtypeStruct((n_tok, LORA_ICLR_PAD), F32),
                   jax.ShapeDtypeStruct((n_tok, LORA_GATE_PAD), F32)],
        scratch_shapes=[pltpu.VMEM((1, D_MODEL), F32)],
        compiler_params=_params("arbitrary", "arbitrary"),
        name="in_proj",
    )(x2, sc, sh, g_pre, mu_lora, w1p, a1p, g1p, w_main)


def _attn_proj_kernel(h_ref, w_ref, o_ref, *scratch, dilation):
    acc = jnp.dot(h_ref[...], w_ref[...], preferred_element_type=F32)
    if dilation == 1:
        o_ref[0, 0] = acc.astype(BF16)
    else:
        scr, = scratch
        rows = acc.shape[0] // dilation
        for cg in range(scr.shape[0]):
            scr[cg] = acc[:, LANES * cg:LANES * (cg + 1)]
        for r in range(dilation):
            for cg in range(scr.shape[0]):
                o_ref[0, r, :, LANES * cg:LANES * (cg + 1)] = (
                    scr[cg, pl.ds(r, rows, stride=dilation), :].astype(BF16))


def _attn_proj(h, w_qkv, dilation, bsz, seq, tm):
    tpb = seq // tm
    width = 3 * ATTN_GROUP_WIDTH
    return pl.pallas_call(
        functools.partial(_attn_proj_kernel, dilation=dilation),
        grid=(bsz * tpb,),
        in_specs=[pl.BlockSpec((tm, D_MODEL), lambda i: (i, 0)),
                  pl.BlockSpec((D_MODEL, width), lambda i: (0, 0))],
        out_specs=pl.BlockSpec((1, dilation, tm // dilation, width), lambda i: (i // tpb, 0, i % tpb, 0)),
        out_shape=jax.ShapeDtypeStruct((bsz, dilation, seq // dilation, width), BF16),
        scratch_shapes=[] if dilation == 1 else [pltpu.VMEM((width // LANES, tm, LANES), F32)],
        compiler_params=_params("arbitrary"),
        name=f"attn_proj_d{dilation}",
    )(h, w_qkv)


def _prep_kernel(pr_ref, pk_ref, pv_ref, lw_ref, la_ref, lg_ref, mu_ref, w0_ref, a0_ref, kk_ref,
                 ka_ref, rk_ref, w2_ref, a2_ref, g2_ref, e_ref,
                 r_o, lw_o, k_o, v_o, kk_o, kka_o, g_o, bonus_o, last_scr, *, tiles_per_batch):
    i = pl.program_id(0)
    tm = pr_ref.shape[0]

    @pl.when(i % tiles_per_batch == 0)
    def _():
        last_scr[...] = jnp.zeros_like(last_scr)

    def mix(p_ref, idx):
        p = p_ref[...]
        ps = _shift_rows(p, last_scr[idx:idx + 1, :])
        last_scr[idx:idx + 1, :] = p[tm - 1:tm, :]
        return p + (ps - p) * mu_ref[idx:idx + 1, :]

    r = mix(pr_ref, 0)
    k = mix(pk_ref, 1)
    v = mix(pv_ref, 2)
    e = e_ref[...]

    z = w0_ref[...] + _dot(jnp.tanh(lw_ref[...]), w2_ref[...])
    lw_o[...] = -math.exp(-0.5) * jax.nn.sigmoid(z)
    a = jax.nn.sigmoid(a0_ref[...] + _dot(la_ref[...], a2_ref[...]))
    g_o[...] = _dot(jax.nn.sigmoid(lg_ref[...]), g2_ref[...]).astype(BF16)

    kk = k * kk_ref[...]
    kk = kk * lax.rsqrt(jnp.maximum(_seg_sum(kk * kk, e), 1e-24))
    k2 = k * (1.0 + (a - 1.0) * ka_ref[...])
    r_o[...] = r.astype(BF16)
    k_o[...] = k2.astype(BF16)
    v_o[...] = v.astype(BF16)
    kk_o[...] = kk.astype(BF16)
    kka_o[...] = (kk * a).astype(BF16)
    bonus_o[...] = (_seg_sum(r * k2 * rk_ref[...], e) * v).astype(BF16)


def _prep(proj, lw, la, lg, mu_rkv, w0, a0, k_k, k_a, r_k, w2p, a2p, g2p, e_seg, seq, tm):
    n_tok = proj.shape[0]
    tpb = seq // tm
    row = lambda i: (i, 0)
    const = lambda i: (0, 0)
    vec = pl.BlockSpec((1, D_MODEL), const)
    tok = pl.BlockSpec((tm, D_MODEL), row)
    return pl.pallas_call(
        functools.partial(_prep_kernel, tiles_per_batch=tpb),
        grid=(n_tok // tm,),
        in_specs=[pl.BlockSpec((tm, D_MODEL), lambda i: (i, COL_R // D_MODEL)),
                  pl.BlockSpec((tm, D_MODEL), lambda i: (i, COL_K // D_MODEL)),
                  pl.BlockSpec((tm, D_MODEL), lambda i: (i, COL_V // D_MODEL)),
                  pl.BlockSpec((tm, LORA_DECAY_PAD), row),
                  pl.BlockSpec((tm, LORA_ICLR_PAD), row),
                  pl.BlockSpec((tm, LORA_GATE_PAD), row),
                  pl.BlockSpec((3, D_MODEL), const),
                  vec, vec, vec, vec, vec,
                  pl.BlockSpec((LORA_DECAY_PAD, D_MODEL), const),
                  pl.BlockSpec((LORA_ICLR_PAD, D_MODEL), const),
                  pl.BlockSpec((LORA_GATE_PAD, D_MODEL), const),
                  pl.BlockSpec((SEG_TILE, SEG_TILE), const)],
        out_specs=[tok] * 8,
        out_shape=[jax.ShapeDtypeStruct((n_tok, D_MODEL), F32 if name == "lw" else BF16)
                   for name in ("r", "lw", "k", "v", "kk", "kka", "g", "bonus")],
        scratch_shapes=[pltpu.VMEM((3, D_MODEL), F32)],
        compiler_params=_params("arbitrary"),
        name="rwkv_prep",
    )(proj, proj, proj, lw, la, lg, mu_rkv, w0, a0, k_k, k_a, r_k, w2p, a2p, g2p, e_seg)


def _chunk_kernel(r_ref, lw_ref, k_ref, v_ref, kk_ref, kka_ref, y_ref, s_scr):
    @pl.when(pl.program_id(1) == 0)
    def _():
        s_scr[...] = jnp.zeros_like(s_scr)

    c = CHUNK
    rows = lw_ref.shape[1]
    n_chunks = rows // c
    lw = lw_ref[0]
    ti = lax.broadcasted_iota(jnp.int32, (rows, rows), 0)
    tj = lax.broadcasted_iota(jnp.int32, (rows, rows), 1)
    tril = ((ti >= tj) & (ti // c == tj // c)).astype(F32)
    cum = jnp.dot(tril, lw, preferred_element_type=F32, precision=HIGHEST)
    cum_prev = cum - lw
    cum_end = jnp.concatenate(
        [jnp.broadcast_to(cum[c * (ck + 1) - 1:c * (ck + 1), :], (c, D_MODEL)) for ck in range(n_chunks)], axis=0)
    kk = kk_ref[0].astype(F32)
    kka = kka_ref[0].astype(F32)
    k = k_ref[0].astype(F32)
    e_neg = jnp.exp(-cum)
    e_end = jnp.exp(cum_end - cum)
    at_all = -kk * jnp.exp(cum_prev)
    rt_all = r_ref[0].astype(F32) * jnp.exp(cum)
    bt_all = kka * e_neg
    kt_all = k * e_neg
    bh_all = kka * e_end
    kh_all = k * e_end
    p_end = jnp.exp(cum_end)
    v_all = v_ref[0].astype(F32)

    ri = lax.broadcasted_iota(jnp.int32, (PAIR, PAIR), 0)
    ci = lax.broadcasted_iota(jnp.int32, (PAIR, PAIR), 1)
    same = (ri // RWKV_HEAD) == (ci // RWKV_HEAD)
    same2 = jnp.concatenate([same, same], axis=1)
    eye = ri == ci
    tt = lax.broadcasted_iota(jnp.int32, (c, PAIR), 0)
    ss = lax.broadcasted_iota(jnp.int32, (c, PAIR), 1) % c
    m_strict = tt > ss
    m_incl = tt >= ss
    eye_rows = (tt == ss).astype(F32)
    zeros_bd = jnp.zeros((PAIR, PAIR), BF16)
    zeros_c = jnp.zeros((c, PAIR), F32)

    def bd(t):
        mask = same if t.shape[1] == PAIR else same2
        return jnp.where(mask, jnp.concatenate([t, t], axis=0), 0.0).astype(BF16)

    items = [(slice(c * ck, c * (ck + 1)), slice(PAIR * p, PAIR * (p + 1)))
             for ck in range(n_chunks) for p in range(N_PAIRS)]
    pairs = range(len(items))
    at = [at_all[rs, sl] for rs, sl in items]
    rt = [rt_all[rs, sl] for rs, sl in items]
    v_bd = [bd(v_all[rs, sl]) for rs, sl in items]

    quad = [_dot_nt(jnp.concatenate([at[p], rt[p]], axis=0),
                    jnp.concatenate([bd(bt_all[items[p]]), bd(kt_all[items[p]])], axis=0))
            for p in pairs]
    a_ab = [jnp.where(m_strict, q[:c, :PAIR], 0.0) for q in quad]
    a_ak = [jnp.where(m_strict, q[:c, PAIR:], 0.0) for q in quad]
    a_r = [jnp.concatenate([jnp.where(m_incl, q[c:, :PAIR], 0.0),
                            jnp.where(m_incl, q[c:, PAIR:], 0.0)], axis=1).astype(BF16) for q in quad]
    x0 = [jnp.concatenate([at[p], _dot(a_ak[p], v_bd[p])], axis=1) for p in pairs]

    t_inv = [eye_rows + a_ab[p] for p in pairs]
    pw = a_ab
    for _ in range(int(math.log2(c)) - 1):
        pw = [_dot(pw[p], bd(pw[p])) for p in pairs]
        t_inv = [t_inv[p] + _dot(pw[p], bd(t_inv[p])) for p in pairs]
    x = [_dot(t_inv[p], bd(x0[p])) for p in pairs]

    top = [_dot(a_r[p], jnp.concatenate([bd(x[p]), jnp.concatenate([zeros_bd, v_bd[p]], axis=1)], axis=0))
           for p in pairs]
    bot = [_dot_tn(jnp.concatenate([bh_all[items[p]], kh_all[items[p]]], axis=0),
                   jnp.concatenate([x[p], jnp.concatenate([zeros_c, v_all[items[p]]], axis=1)], axis=0))
           for p in pairs]

    state = [s_scr[hp] for hp in range(N_PAIRS)]
    for p in pairs:
        rs, sl = items[p]
        hp = p % N_PAIRS
        p_end_row = p_end[rs, sl][c - 1:c, :]
        m_mat = jnp.where(same, bot[p][:, :PAIR], 0.0) + jnp.where(
            eye, jnp.broadcast_to(p_end_row, (PAIR, PAIR)), 0.0)
        lhs = jnp.concatenate([rt[p] + top[p][:, :PAIR], m_mat], axis=0)
        fin = _dot(lhs, state[hp])
        y_ref[0, rs, sl] = fin[:c] + top[p][:, PAIR:]
        state[hp] = fin[c:] + jnp.where(same, bot[p][:, PAIR:], 0.0)
    for hp in range(N_PAIRS):
        s_scr[hp] = state[hp]


def _chunk(r, lw, k, v, kk, kka):
    bsz, seq, _ = r.shape
    spec = pl.BlockSpec((1, CHUNKS_PER_STEP * CHUNK, D_MODEL), lambda b, c: (b, c, 0))
    return pl.pallas_call(
        _chunk_kernel,
        grid=(bsz, seq // (CHUNKS_PER_STEP * CHUNK)),
        in_specs=[spec] * 6,
        out_specs=spec,
        out_shape=jax.ShapeDtypeStruct((bsz, seq, D_MODEL), F32),
        scratch_shapes=[pltpu.VMEM((N_PAIRS, PAIR, PAIR), F32)],
        compiler_params=_params("arbitrary", "arbitrary"),
        name="rwkv_chunk",
    )(r, lw, k, v, kk, kka)


def _attn_kernel(cur_ref, prev_ref, o_ref, l_ref, *, dilation, slopes):
    n = pl.program_id(2)
    blk = ATTN_BLK
    gw = ATTN_GROUP_WIDTH
    ii = lax.broadcasted_iota(jnp.int32, (blk, 2 * blk), 0)
    jj = lax.broadcasted_iota(jnp.int32, (blk, 2 * blk), 1)
    delta = ii - jj + blk
    in_window = (delta >= 0) & (delta <= blk)
    dist = delta.astype(F32) * float(dilation)
    lane_head = lax.broadcasted_iota(jnp.int32, (blk, PAIR), 1) // ATTN_HEAD
    lane = lax.broadcasted_iota(jnp.int32, (blk, LANES), 1)
    neg_inf = float("-inf")
    pairs = range(gw // PAIR)
    heads = [(pp, hh) for pp in pairs for hh in range(2)]
    sls = [slice(PAIR * pp, PAIR * (pp + 1)) for pp in pairs]

    for sb in range(2):
        rows = slice(blk * sb, blk * (sb + 1))
        if sb == 0:
            valid = in_window & ((n > 0) | (jj >= blk))
            kcat = [jnp.concatenate([prev_ref[0, 0, :, sl], cur_ref[0, 0, rows, sl]], axis=0) for sl in sls]
            vcat = [jnp.concatenate([prev_ref[0, 0, :, gw + PAIR * pp:gw + PAIR * (pp + 1)],
                                     cur_ref[0, 0, rows, gw + PAIR * pp:gw + PAIR * (pp + 1)]], axis=0)
                    for pp in pairs]
        else:
            valid = in_window
            kcat = [cur_ref[0, 0, :, sl] for sl in sls]
            vcat = [cur_ref[0, 0, :, gw + PAIR * pp:gw + PAIR * (pp + 1)] for pp in pairs]
        q2 = [cur_ref[0, 0, rows, 2 * gw + PAIR * pp:2 * gw + PAIR * (pp + 1)] * (ATTN_HEAD ** -0.5)
              for pp in pairs]
        s = [jnp.where(valid,
                       _dot_nt(jnp.where(lane_head == hh, q2[pp], jnp.zeros_like(q2[pp])), kcat[pp])
                       - float(slopes[2 * pp + hh]) * dist, neg_inf) for pp, hh in heads]
        m = [jnp.max(t, axis=-1, keepdims=True) for t in s]
        p = [jnp.exp(s[i] - m[i]) for i in range(len(heads))]
        den = [jnp.sum(t, axis=-1, keepdims=True) for t in p]
        o = [_dot(p[i], vcat[pp]) * (1.0 / den[i]) for i, (pp, _) in enumerate(heads)]
        for pp in pairs:
            o_ref[0, 0, rows, sls[pp]] = jnp.where(lane_head == 0, o[2 * pp], o[2 * pp + 1]).astype(BF16)
        lse = jnp.zeros((blk, LANES), F32)
        for i in range(len(heads)):
            lse = jnp.where(lane == i, m[i] + jnp.log(den[i]), lse)
        l_ref[0, 0, rows, :] = lse


def _attn_group(kvq, dilation, slopes):
    bsz, _, sub, _ = kvq.shape
    gw = ATTN_GROUP_WIDTH
    step = 2 * ATTN_BLK
    return pl.pallas_call(
        functools.partial(_attn_kernel, dilation=dilation, slopes=tuple(slopes)),
        grid=(bsz, dilation, sub // step),
        in_specs=[pl.BlockSpec((1, 1, step, 3 * gw), lambda b, r, n: (b, r, n, 0)),
                  pl.BlockSpec((1, 1, ATTN_BLK, 2 * gw), lambda b, r, n: (b, r, jnp.maximum(2 * n - 1, 0), 0))],
        out_specs=[pl.BlockSpec((1, 1, step, gw), lambda b, r, n: (b, r, n, 0)),
                   pl.BlockSpec((1, 1, step, LANES), lambda b, r, n: (b, r, n, 0))],
        out_shape=[jax.ShapeDtypeStruct((bsz, dilation, sub, gw), BF16),
                   jax.ShapeDtypeStruct((bsz, dilation, sub, LANES), F32)],
        compiler_params=_params("arbitrary", "arbitrary", "arbitrary"),
        name=f"dilated_attn_d{dilation}",
    )(kvq, kvq)


def _mix_kernel(y_ref, bonus_ref, g_ref, za_ref, zb_ref, x_ref, o0_ref, o1_ref, o2_ref,
                l0_ref, l1_ref, l2_ref, gt_ref, lnw_ref, lnb_ref, gpost_ref,
                wor_ref, woa_ref, wout_ref, e_ref, spread_ref, out_ref, *scratch):
    e = e_ref[...]
    inv_n = 1.0 / RWKV_HEAD
    y = y_ref[...]
    mean = _seg_sum(y, e) * inv_n
    yc = y - mean
    var = _seg_sum(yc * yc, e) * inv_n
    yn = yc * lax.rsqrt(var + GN_EPS) * lnw_ref[...] + lnb_ref[...]
    y_rwkv = _dot((yn + bonus_ref[...].astype(F32)) * g_ref[...].astype(F32), wor_ref[...])

    def token_order(ref, scr):
        dilation, rows = ref.shape[1], ref.shape[2]
        if dilation == 1:
            return ref[0, 0].astype(F32)
        for r in range(dilation):
            for cg in range(ref.shape[3] // LANES):
                scr[cg, pl.ds(r, rows, stride=dilation), :] = (
                    ref[0, r, :, LANES * cg:LANES * (cg + 1)].astype(F32))
        return jnp.concatenate([scr[cg] for cg in range(ref.shape[3] // LANES)], axis=1)

    def spread(l):
        hi = l.astype(BF16)
        lo = (l - hi.astype(F32)).astype(BF16)
        return (jnp.dot(hi, spread_ref[...], preferred_element_type=F32)
                + jnp.dot(lo, spread_ref[...], preferred_element_type=F32))

    o0, l0 = token_order(o0_ref, None), spread(token_order(l0_ref, None))
    o1, l1 = token_order(o1_ref, scratch[0]), spread(token_order(l1_ref, scratch[1]))
    o2, l2 = token_order(o2_ref, scratch[2]), spread(token_order(l2_ref, scratch[3]))
    mx = jnp.maximum(jnp.maximum(l0, l1), l2)
    e0, e1, e2 = jnp.exp(l0 - mx), jnp.exp(l1 - mx), jnp.exp(l2 - mx)
    o = (e0 * o0 + e1 * o1 + e2 * o2) / (e0 + e1 + e2)
    y_attn = _dot(o, woa_ref[...])

    mixed = _dot(jax.nn.sigmoid(za_ref[...]) * y_rwkv + jax.nn.sigmoid(zb_ref[...]) * y_attn,
                 wout_ref[...])
    out_ref[...] = x_ref[...] + gt_ref[0] * (_rms(mixed) * gpost_ref[...])


def _mix(y, bonus, g, proj, x2, o_l, gt, ln_w, ln_b, g_post, w_or, w_oa, w_out, e_seg, seq, tm):
    n_tok = x2.shape[0]
    tpb = seq // tm
    row = lambda i: (i, 0)
    const = lambda i: (0, 0)
    tok = pl.BlockSpec((tm, D_MODEL), row)
    vec = pl.BlockSpec((1, D_MODEL), const)

    def att(dilation, width):
        return pl.BlockSpec((1, dilation, tm // dilation, width), lambda i: (i // tpb, 0, i % tpb, 0))

    (o0, l0), (o1, l1), (o2, l2) = o_l
    d0, d1, d2 = [d for _, d in ATTN_GROUPS]
    assert d0 == 1
    gw = ATTN_GROUP_WIDTH
    head_of_lane = jnp.arange(gw) // ATTN_HEAD
    spread = (jnp.arange(LANES)[:, None] == head_of_lane[None, :]).astype(BF16)
    return pl.pallas_call(
        _mix_kernel,
        grid=(n_tok // tm,),
        in_specs=[tok, tok, tok,
                  pl.BlockSpec((tm, D_MODEL), lambda i: (i, COL_ZA // D_MODEL)),
                  pl.BlockSpec((tm, D_MODEL), lambda i: (i, COL_ZB // D_MODEL)),
                  tok, att(d0, gw), att(d1, gw), att(d2, gw), att(d0, LANES), att(d1, LANES), att(d2, LANES),
                  pl.BlockSpec((1, 1, D_MODEL), lambda i: (i // tpb, 0, 0)),
                  vec, vec, vec,
                  pl.BlockSpec((D_MODEL, D_MODEL), const),
                  pl.BlockSpec((gw, D_MODEL), const),
                  pl.BlockSpec((D_MODEL, D_MODEL), const),
                  pl.BlockSpec((SEG_TILE, SEG_TILE), const),
                  pl.BlockSpec((LANES, gw), const)],
        out_specs=tok,
        out_shape=jax.ShapeDtypeStruct((n_tok, D_MODEL), F32),
        scratch_shapes=[pltpu.VMEM((gw // LANES, tm, LANES), F32), pltpu.VMEM((1, tm, LANES), F32)] * 2,
        compiler_params=_params("arbitrary"),
        name="mix_out",
    )(y, bonus, g, proj, proj, x2, o0, o1, o2, l0, l1, l2, gt, ln_w, ln_b, g_post,
      w_or, w_oa, w_out, e_seg, spread)


def _ffn_kernel(x_ref, sc_ref, sh_ref, gt_ref, gpre_ref, gpost_ref, wg_ref, wu_ref, wo_ref, out_ref):
    x = x_ref[...]
    h = (_rms(x) * gpre_ref[...] * (1.0 + sc_ref[0]) + sh_ref[0]).astype(BF16)
    acts = []
    for t in range(FFN_HIDDEN // SEG_TILE):
        sl = slice(SEG_TILE * t, SEG_TILE * (t + 1))
        u_gate = jnp.dot(h, wg_ref[:, sl], preferred_element_type=F32)
        u_up = jnp.dot(h, wu_ref[:, sl], preferred_element_type=F32)
        acts.append((u_gate * jax.nn.sigmoid(u_gate) * u_up).astype(BF16))
    y = jnp.dot(jnp.concatenate(acts, axis=1), wo_ref[...], preferred_element_type=F32)
    out_ref[...] = x + gt_ref[0] * (_rms(y) * gpost_ref[...])


def _ffn(x1, sc, sh, gt, g_pre, g_post, w_in, w_out, seq, tm):
    n_tok = x1.shape[0]
    tpb = seq // tm
    row = lambda i: (i, 0)
    const = lambda i: (0, 0)
    modrow = lambda i: (i // tpb, 0, 0)
    tok = pl.BlockSpec((tm, D_MODEL), row)
    mod = pl.BlockSpec((1, 1, D_MODEL), modrow)
    vec = pl.BlockSpec((1, D_MODEL), const)
    resident = pl.Buffered(1)
    return pl.pallas_call(
        _ffn_kernel,
        grid=(n_tok // tm,),
        in_specs=[tok, mod, mod, mod, vec, vec,
                  pl.BlockSpec((D_MODEL, FFN_HIDDEN), lambda i: (0, 0), pipeline_mode=resident),
                  pl.BlockSpec((D_MODEL, FFN_HIDDEN), lambda i: (0, 1), pipeline_mode=resident),
                  pl.BlockSpec((FFN_HIDDEN, D_MODEL), const, pipeline_mode=resident)],
        out_specs=tok,
        out_shape=jax.ShapeDtypeStruct((n_tok, D_MODEL), F32),
        compiler_params=_params("arbitrary"),
        name="swiglu_ffn",
    )(x1, sc, sh, gt, g_pre, g_post, w_in, w_in, w_out)


def _pad_cols(w, n):
    return jnp.pad(w, ((0, 0), (0, n - w.shape[1]))).astype(BF16)


def _pad_rows(w, n):
    return jnp.pad(w, ((0, n - w.shape[0]), (0, 0))).astype(BF16)


def kernel(x, c, w_mod, b_mod, g_pre_mix, g_post_mix, g_pre_ffn, g_post_ffn, w_in, mu_rkv, mu_lora,
           w0, w1, w2, a0, a1, a2, g1, g2, k_k, k_a, r_k, ln_x_w, ln_x_b, w_o_rwkv, w_o_attn, w_out,
           w_ffn_in, w_ffn_out):
    bsz, seq, _ = x.shape
    depth = w_mod.shape[0]
    assert seq % (2 * ATTN_BLK * ATTN_GROUPS[-1][1]) == 0
    n_tok = bsz * seq
    tm_proj = min(1024, seq)
    tm_aproj = min(512, seq)
    tm_prep = min(256, seq)
    tm_mix = min(256, seq)
    tm_ffn = min(512, seq)

    slopes = _alibi_slopes(len(ATTN_GROUPS) * ATTN_HEADS_PER_GROUP)
    seg = jnp.arange(SEG_TILE) // RWKV_HEAD
    e_seg = (seg[:, None] == seg[None, :]).astype(BF16)
    row = lambda t: t.reshape(1, -1)

    x2 = x.reshape(n_tok, D_MODEL)
    for l in range(depth):
        mod = _mod(c, w_mod[l], b_mod[l])
        sh_m, sc_m, gt_m, sh_f, sc_f, gt_f = [mod[:, i * D_MODEL:(i + 1) * D_MODEL].reshape(bsz, 1, D_MODEL)
                                               for i in range(6)]
        wl = w_in[l]
        rkv_cols = 3 * D_MODEL
        att_cols = 3 * ATTN_WIDTH
        w_main = jnp.concatenate([wl[:, :rkv_cols], wl[:, rkv_cols + att_cols:]], axis=1).astype(BF16)

        proj, h, lw, la, lg = _inproj(
            x2, sc_m, sh_m, row(g_pre_mix[l]), mu_lora[l],
            _pad_cols(w1[l], LORA_DECAY_PAD), _pad_cols(a1[l], LORA_ICLR_PAD),
            _pad_cols(g1[l], LORA_GATE_PAD), w_main, seq, tm_proj)

        r, lwd, k2, v, kk, kka, g, bonus = _prep(
            proj, lw, la, lg, mu_rkv[l], row(w0[l]), row(a0[l]), row(k_k[l]), row(k_a[l]), row(r_k[l]),
            _pad_rows(w2[l], LORA_DECAY_PAD), _pad_rows(a2[l], LORA_ICLR_PAD),
            _pad_rows(g2[l], LORA_GATE_PAD), e_seg, seq, tm_prep)

        b3 = lambda t: t.reshape(bsz, seq, D_MODEL)
        y = _chunk(b3(r), b3(lwd), b3(k2), b3(v), b3(kk), b3(kka)).reshape(n_tok, D_MODEL)

        o_l = []
        for gi, (window, dilation) in enumerate(ATTN_GROUPS):
            assert window // dilation == ATTN_BLK
            gsl = lambda which: wl[:, rkv_cols + which * ATTN_WIDTH + gi * ATTN_GROUP_WIDTH:
                                   rkv_cols + which * ATTN_WIDTH + (gi + 1) * ATTN_GROUP_WIDTH]
            w_qkv = jnp.concatenate([gsl(1), gsl(2), gsl(0)], axis=1).astype(BF16)
            qkv = _attn_proj(h, w_qkv, dilation, bsz, seq, tm_aproj)
            o_l.append(_attn_group(
                qkv, dilation, slopes[gi * ATTN_HEADS_PER_GROUP:(gi + 1) * ATTN_HEADS_PER_GROUP]))

        x2 = _mix(y, bonus, g, proj, x2, o_l, gt_m, row(ln_x_w[l]), row(ln_x_b[l]), row(g_post_mix[l]),
                  w_o_rwkv[l].astype(BF16), w_o_attn[l].astype(BF16), w_out[l].astype(BF16), e_seg,
                  seq, tm_mix)

        x2 = _ffn(x2, sc_f, sh_f, gt_f, row(g_pre_ffn[l]), row(g_post_ffn[l]),
                  w_ffn_in[l].astype(BF16), w_ffn_out[l].astype(BF16), seq, tm_ffn)
    return x2.reshape(bsz, seq, D_MODEL)
```

```python
import functools
import math

import jax
import jax.numpy as jnp
from jax import lax
from jax.experimental import pallas as pl
from jax.experimental.pallas import tpu as pltpu

F32 = jnp.float32
BF16 = jnp.bfloat16
HIGHEST = lax.Precision.HIGHEST

LANES = 128
SEG_TILE = 256
D_MODEL = 1024
RWKV_HEAD = 64
GN_EPS = 64e-5
RMS_EPS = 1e-6
ATTN_HEAD = 64
ATTN_HEADS_PER_GROUP = 8
ATTN_GROUPS = ((128, 1), (512, 4), (2048, 16))
ATTN_GROUP_WIDTH = ATTN_HEADS_PER_GROUP * ATTN_HEAD
ATTN_WIDTH = len(ATTN_GROUPS) * ATTN_GROUP_WIDTH
ATTN_BLK = 128
FFN_HIDDEN = 2816
LORA_DECAY_PAD = 128
LORA_ICLR_PAD = 128
LORA_GATE_PAD = 256
CHUNK = 64
CHUNKS_PER_STEP = 4
PAIR = 2 * RWKV_HEAD
N_PAIRS = D_MODEL // PAIR

COL_R, COL_K, COL_V, COL_ZA, COL_ZB = 0, 1024, 2048, 3072, 4096
MAIN_COLS = 5120
PROJ_TILE = 1280

VMEM_LIMIT = 48 * 1024 * 1024


def _alibi_slopes(n):
    def pow2(m):
        start = 2.0 ** (-8.0 / m)
        return [start ** (i + 1) for i in range(m)]
    if math.log2(n).is_integer():
        s = pow2(n)
    else:
        p = 2 ** int(math.floor(math.log2(n)))
        s = pow2(p) + pow2(2 * p)[0::2][: n - p]
    return sorted(s, reverse=True)


def _dot(a, b):
    return jnp.dot(a.astype(BF16), b.astype(BF16), preferred_element_type=F32)


def _dot_nt(a, b):
    return lax.dot_general(a.astype(BF16), b.astype(BF16), (((1,), (1,)), ((), ())),
                           preferred_element_type=F32)


def _dot_tn(a, b):
    return lax.dot_general(a.astype(BF16), b.astype(BF16), (((0,), (0,)), ((), ())),
                           preferred_element_type=F32)


def _seg_sum(x, e, pieces=1):
    parts = [x.astype(BF16)]
    if pieces == 2:
        parts.append((x - parts[0].astype(F32)).astype(BF16))
    outs = []
    for t in range(x.shape[1] // SEG_TILE):
        sl = slice(SEG_TILE * t, SEG_TILE * (t + 1))
        outs.append(sum(jnp.dot(part[:, sl], e, preferred_element_type=F32) for part in parts))
    return jnp.concatenate(outs, axis=1)


def _shift_rows(t, prev_row):
    rows = lax.broadcasted_iota(jnp.int32, t.shape, 0)
    return jnp.where(rows == 0, prev_row, pltpu.roll(t, 1, 0))


def _rms(t):
    return t * lax.rsqrt(jnp.mean(t * t, axis=-1, keepdims=True) + RMS_EPS)


def _params(*semantics):
    return pltpu.CompilerParams(dimension_semantics=semantics, vmem_limit_bytes=VMEM_LIMIT)


def _mod_kernel(c_ref, w_ref, b_ref, o_ref):
    o_ref[...] = jnp.dot(c_ref[...], w_ref[...], preferred_element_type=F32,
                         precision=HIGHEST) + b_ref[...]


def _mod(c, w_mod, b_mod):
    bsz = c.shape[0]
    n_out = w_mod.shape[1]
    return pl.pallas_call(
        _mod_kernel,
        grid=(n_out // D_MODEL,),
        in_specs=[pl.BlockSpec((bsz, D_MODEL), lambda j: (0, 0)),
                  pl.BlockSpec((D_MODEL, D_MODEL), lambda j: (0, j)),
                  pl.BlockSpec((1, D_MODEL), lambda j: (0, j))],
        out_specs=pl.BlockSpec((bsz, D_MODEL), lambda j: (0, j)),
        out_shape=jax.ShapeDtypeStruct((bsz, n_out), F32),
        compiler_params=_params("arbitrary"),
        name="adaln_mod",
    )(c, w_mod, b_mod.reshape(1, n_out))


def _inproj_kernel(x_ref, sc_ref, sh_ref, g_ref, mu_ref, w1_ref, a1_ref, g1_ref, win_ref,
                   p_ref, h_ref, lw_ref, la_ref, lg_ref, last_scr, *, tiles_per_batch):
    i = pl.program_id(0)
    j = pl.program_id(1)
    tm = x_ref.shape[0]

    @pl.when(j == 0)
    def _():
        @pl.when(i % tiles_per_batch == 0)
        def _():
            last_scr[...] = jnp.zeros_like(last_scr)

        h = _rms(x_ref[...]) * g_ref[...] * (1.0 + sc_ref[0]) + sh_ref[0]
        hs = _shift_rows(h, last_scr[...])
        last_scr[...] = h[tm - 1:tm, :]
        dh = hs - h
        lw_ref[...] = _dot(h + dh * mu_ref[0:1, :], w1_ref[...])
        la_ref[...] = _dot(h + dh * mu_ref[1:2, :], a1_ref[...])
        lg_ref[...] = _dot(h + dh * mu_ref[2:3, :], g1_ref[...])
        h_ref[...] = h.astype(BF16)

    p_ref[...] = jnp.dot(h_ref[...], win_ref[...], preferred_element_type=F32)


def _inproj(x2, sc, sh, g_pre, mu_lora, w1p, a1p, g1p, w_main, seq, tm):
    n_tok = x2.shape[0]
    tpb = seq // tm
    row = lambda i, j: (i, 0)
    const = lambda i, j: (0, 0)
    modrow = lambda i, j: (i // tpb, 0, 0)
    return pl.pallas_call(
        functools.partial(_inproj_kernel, tiles_per_batch=tpb),
        grid=(n_tok // tm, MAIN_COLS // PROJ_TILE),
        in_specs=[pl.BlockSpec((tm, D_MODEL), row),
                  pl.BlockSpec((1, 1, D_MODEL), modrow),
                  pl.BlockSpec((1, 1, D_MODEL), modrow),
                  pl.BlockSpec((1, D_MODEL), const),
                  pl.BlockSpec((3, D_MODEL), const),
                  pl.BlockSpec((D_MODEL, LORA_DECAY_PAD), const),
                  pl.BlockSpec((D_MODEL, LORA_ICLR_PAD), const),
                  pl.BlockSpec((D_MODEL, LORA_GATE_PAD), const),
                  pl.BlockSpec((D_MODEL, PROJ_TILE), lambda i, j: (0, j))],
        out_specs=[pl.BlockSpec((tm, PROJ_TILE), lambda i, j: (i, j)),
                   pl.BlockSpec((tm, D_MODEL), row),
                   pl.BlockSpec((tm, LORA_DECAY_PAD), row),
                   pl.BlockSpec((tm, LORA_ICLR_PAD), row),
                   pl.BlockSpec((tm, LORA_GATE_PAD), row)],
        out_shape=[jax.ShapeDtypeStruct((n_tok, MAIN_COLS), F32),
                   jax.ShapeDtypeStruct((n_tok, D_MODEL), BF16),
                   jax.ShapeDtypeStruct((n_tok, LORA_DECAY_PAD), F32),
                   jax.ShapeDtypeStruct((n_tok, LORA_ICLR_PAD), F32),
                   jax.ShapeDtypeStruct((n_tok, LORA_GATE_PAD), F32)],
        scratch_shapes=[pltpu.VMEM((1, D_MODEL), F32)],
        compiler_params=_params("arbitrary", "arbitrary"),
        name="in_proj",
    )(x2, sc, sh, g_pre, mu_lora, w1p, a1p, g1p, w_main)


def _attn_proj_kernel(h_ref, w_ref, o_ref, *scratch, dilation):
    acc = jnp.dot(h_ref[...], w_ref[...], preferred_element_type=F32)
    if dilation == 1:
        o_ref[0, 0] = acc.astype(BF16)
    else:
        scr, = scratch
        rows = acc.shape[0] // dilation
        for cg in range(scr.shape[0]):
            scr[cg] = acc[:, LANES * cg:LANES * (cg + 1)]
        for r in range(dilation):
            for cg in range(scr.shape[0]):
                o_ref[0, r, :, LANES * cg:LANES * (cg + 1)] = (
                    scr[cg, pl.ds(r, rows, stride=dilation), :].astype(BF16))


def _attn_proj(h, w_qkv, dilation, bsz, seq, tm):
    tpb = seq // tm
    width = 3 * ATTN_GROUP_WIDTH
    return pl.pallas_call(
        functools.partial(_attn_proj_kernel, dilation=dilation),
        grid=(bsz * tpb,),
        in_specs=[pl.BlockSpec((tm, D_MODEL), lambda i: (i, 0)),
                  pl.BlockSpec((D_MODEL, width), lambda i: (0, 0))],
        out_specs=pl.BlockSpec((1, dilation, tm // dilation, width), lambda i: (i // tpb, 0, i % tpb, 0)),
        out_shape=jax.ShapeDtypeStruct((bsz, dilation, seq // dilation, width), BF16),
        scratch_shapes=[] if dilation == 1 else [pltpu.VMEM((width // LANES, tm, LANES), F32)],
        compiler_params=_params("arbitrary"),
        name=f"attn_proj_d{dilation}",
    )(h, w_qkv)


def _prep_kernel(pr_ref, pk_ref, pv_ref, lw_ref, la_ref, lg_ref, mu_ref, w0_ref, a0_ref, kk_ref,
                 ka_ref, rk_ref, w2_ref, a2_ref, g2_ref, e_ref,
                 r_o, lw_o, k_o, v_o, kk_o, kka_o, g_o, bonus_o, last_scr, *, tiles_per_batch):
    i = pl.program_id(0)
    tm = pr_ref.shape[0]

    @pl.when(i % tiles_per_batch == 0)
    def _():
        last_scr[...] = jnp.zeros_like(last_scr)

    def mix(p_ref, idx):
        p = p_ref[...]
        ps = _shift_rows(p, last_scr[idx:idx + 1, :])
        last_scr[idx:idx + 1, :] = p[tm - 1:tm, :]
        return p + (ps - p) * mu_ref[idx:idx + 1, :]

    r = mix(pr_ref, 0)
    k = mix(pk_ref, 1)
    v = mix(pv_ref, 2)
    e = e_ref[...]

    z = w0_ref[...] + _dot(jnp.tanh(lw_ref[...]), w2_ref[...])
    lw_o[...] = -math.exp(-0.5) * jax.nn.sigmoid(z)
    a = jax.nn.sigmoid(a0_ref[...] + _dot(la_ref[...], a2_ref[...]))
    g_o[...] = _dot(jax.nn.sigmoid(lg_ref[...]), g2_ref[...]).astype(BF16)

    kk = k * kk_ref[...]
    kk = kk * lax.rsqrt(jnp.maximum(_seg_sum(kk * kk, e), 1e-24))
    k2 = k * (1.0 + (a - 1.0) * ka_ref[...])
    r_o[...] = r.astype(BF16)
    k_o[...] = k2.astype(BF16)
    v_o[...] = v.astype(BF16)
    kk_o[...] = kk.astype(BF16)
    kka_o[...] = (kk * a).astype(BF16)
    bonus_o[...] = (_seg_sum(r * k2 * rk_ref[...], e) * v).astype(BF16)


def _prep(proj, lw, la, lg, mu_rkv, w0, a0, k_k, k_a, r_k, w2p, a2p, g2p, e_seg, seq, tm):
    n_tok = proj.shape[0]
    tpb = seq // tm
    row = lambda i: (i, 0)
    const = lambda i: (0, 0)
    vec = pl.BlockSpec((1, D_MODEL), const)
    tok = pl.BlockSpec((tm, D_MODEL), row)
    return pl.pallas_call(
        functools.partial(_prep_kernel, tiles_per_batch=tpb),
        grid=(n_tok // tm,),
        in_specs=[pl.BlockSpec((tm, D_MODEL), lambda i: (i, COL_R // D_MODEL)),
                  pl.BlockSpec((tm, D_MODEL), lambda i: (i, COL_K // D_MODEL)),
                  pl.BlockSpec((tm, D_MODEL), lambda i: (i, COL_V // D_MODEL)),
                  pl.BlockSpec((tm, LORA_DECAY_PAD), row),
                  pl.BlockSpec((tm, LORA_ICLR_PAD), row),
                  pl.BlockSpec((tm, LORA_GATE_PAD), row),
                  pl.BlockSpec((3, D_MODEL), const),
                  vec, vec, vec, vec, vec,
                  pl.BlockSpec((LORA_DECAY_PAD, D_MODEL), const),
                  pl.BlockSpec((LORA_ICLR_PAD, D_MODEL), const),
                  pl.BlockSpec((LORA_GATE_PAD, D_MODEL), const),
                  pl.BlockSpec((SEG_TILE, SEG_TILE), const)],
        out_specs=[tok] * 8,
        out_shape=[jax.ShapeDtypeStruct((n_tok, D_MODEL), F32 if name == "lw" else BF16)
                   for name in ("r", "lw", "k", "v", "kk", "kka", "g", "bonus")],
        scratch_shapes=[pltpu.VMEM((3, D_MODEL), F32)],
        compiler_params=_params("arbitrary"),
        name="rwkv_prep",
    )(proj, proj, proj, lw, la, lg, mu_rkv, w0, a0, k_k, k_a, r_k, w2p, a2p, g2p, e_seg)


def _chunk_kernel(r_ref, lw_ref, k_ref, v_ref, kk_ref, kka_ref, y_ref, s_scr):
    @pl.when(pl.program_id(1) == 0)
    def _():
        s_scr[...] = jnp.zeros_like(s_scr)

    c = CHUNK
    rows = lw_ref.shape[1]
    n_chunks = rows // c
    lw = lw_ref[0]
    ti = lax.broadcasted_iota(jnp.int32, (rows, rows), 0)
    tj = lax.broadcasted_iota(jnp.int32, (rows, rows), 1)
    tril = ((ti >= tj) & (ti // c == tj // c)).astype(F32)
    cum = jnp.dot(tril, lw, preferred_element_type=F32, precision=HIGHEST)
    cum_prev = cum - lw
    cum_end = jnp.concatenate(
        [jnp.broadcast_to(cum[c * (ck + 1) - 1:c * (ck + 1), :], (c, D_MODEL)) for ck in range(n_chunks)], axis=0)
    kk = kk_ref[0].astype(F32)
    kka = kka_ref[0].astype(F32)
    k = k_ref[0].astype(F32)
    e_neg = jnp.exp(-cum)
    e_end = jnp.exp(cum_end - cum)
    at_all = -kk * jnp.exp(cum_prev)
    rt_all = r_ref[0].astype(F32) * jnp.exp(cum)
    bt_all = kka * e_neg
    kt_all = k * e_neg
    bh_all = kka * e_end
    kh_all = k * e_end
    p_end = jnp.exp(cum_end)
    v_all = v_ref[0].astype(F32)

    ri = lax.broadcasted_iota(jnp.int32, (PAIR, PAIR), 0)
    ci = lax.broadcasted_iota(jnp.int32, (PAIR, PAIR), 1)
    same = (ri // RWKV_HEAD) == (ci // RWKV_HEAD)
    same2 = jnp.concatenate([same, same], axis=1)
    eye = ri == ci
    tt = lax.broadcasted_iota(jnp.int32, (c, PAIR), 0)
    ss = lax.broadcasted_iota(jnp.int32, (c, PAIR), 1) % c
    m_strict = tt > ss
    m_incl = tt >= ss
    eye_rows = (tt == ss).astype(F32)
    zeros_bd = jnp.zeros((PAIR, PAIR), BF16)
    zeros_c = jnp.zeros((c, PAIR), F32)

    def bd(t):
        mask = same if t.shape[1] == PAIR else same2
        return jnp.where(mask, jnp.concatenate([t, t], axis=0), 0.0).astype(BF16)

    items = [(slice(c * ck, c * (ck + 1)), slice(PAIR * p, PAIR * (p + 1)))
             for ck in range(n_chunks) for p in range(N_PAIRS)]
    pairs = range(len(items))
    at = [at_all[rs, sl] for rs, sl in items]
    rt = [rt_all[rs, sl] for rs, sl in items]
    v_bd = [bd(v_all[rs, sl]) for rs, sl in items]

    quad = [_dot_nt(jnp.concatenate([at[p], rt[p]], axis=0),
                    jnp.concatenate([bd(bt_all[items[p]]), bd(kt_all[items[p]])], axis=0))
            for p in pairs]
    a_ab = [jnp.where(m_strict, q[:c, :PAIR], 0.0) for q in quad]
    a_ak = [jnp.where(m_strict, q[:c, PAIR:], 0.0) for q in quad]
    a_r = [jnp.concatenate([jnp.where(m_incl, q[c:, :PAIR], 0.0),
                            jnp.where(m_incl, q[c:, PAIR:], 0.0)], axis=1).astype(BF16) for q in quad]
    x0 = [jnp.concatenate([at[p], _dot(a_ak[p], v_bd[p])], axis=1) for p in pairs]

    t_inv = [eye_rows + a_ab[p] for p in pairs]
    pw = a_ab
    for _ in range(int(math.log2(c)) - 1):
        pw = [_dot(pw[p], bd(pw[p])) for p in pairs]
        t_inv = [t_inv[p] + _dot(pw[p], bd(t_inv[p])) for p in pairs]
    x = [_dot(t_inv[p], bd(x0[p])) for p in pairs]

    top = [_dot(a_r[p], jnp.concatenate([bd(x[p]), jnp.concatenate([zeros_bd, v_bd[p]], axis=1)], axis=0))
           for p in pairs]
    bot = [_dot_tn(jnp.concatenate([bh_all[items[p]], kh_all[items[p]]], axis=0),
                   jnp.concatenate([x[p], jnp.concatenate([zeros_c, v_all[items[p]]], axis=1)], axis=0))
           for p in pairs]

    state = [s_scr[hp] for hp in range(N_PAIRS)]
    for p in pairs:
        rs, sl = items[p]
        hp = p % N_PAIRS
        p_end_row = p_end[rs, sl][c - 1:c, :]
        m_mat = jnp.where(same, bot[p][:, :PAIR], 0.0) + jnp.where(
            eye, jnp.broadcast_to(p_end_row, (PAIR, PAIR)), 0.0)
        lhs = jnp.concatenate([rt[p] + top[p][:, :PAIR], m_mat], axis=0)
        fin = _dot(lhs, state[hp])
        y_ref[0, rs, sl] = fin[:c] + top[p][:, PAIR:]
        state[hp] = fin[c:] + jnp.where(same, bot[p][:, PAIR:], 0.0)
    for hp in range(N_PAIRS):
        s_scr[hp] = state[hp]


def _chunk(r, lw, k, v, kk, kka):
    bsz, seq, _ = r.shape
    spec = pl.BlockSpec((1, CHUNKS_PER_STEP * CHUNK, D_MODEL), lambda b, c: (b, c, 0))
    return pl.pallas_call(
        _chunk_kernel,
        grid=(bsz, seq // (CHUNKS_PER_STEP * CHUNK)),
        in_specs=[spec] * 6,
        out_specs=spec,
        out_shape=jax.ShapeDtypeStruct((bsz, seq, D_MODEL), F32),
        scratch_shapes=[pltpu.VMEM((N_PAIRS, PAIR, PAIR), F32)],
        compiler_params=_params("arbitrary", "arbitrary"),
        name="rwkv_chunk",
    )(r, lw, k, v, kk, kka)


def _attn_kernel(cur_ref, prev_ref, o_ref, l_ref, *, dilation, slopes):
    n = pl.program_id(2)
    blk = ATTN_BLK
    gw = ATTN_GROUP_WIDTH
    ii = lax.broadcasted_iota(jnp.int32, (blk, 2 * blk), 0)
    jj = lax.broadcasted_iota(jnp.int32, (blk, 2 * blk), 1)
    delta = ii - jj + blk
    in_window = (delta >= 0) & (delta <= blk)
    dist = delta.astype(F32) * float(dilation)
    lane_head = lax.broadcasted_iota(jnp.int32, (blk, PAIR), 1) // ATTN_HEAD
    lane = lax.broadcasted_iota(jnp.int32, (blk, LANES), 1)
    neg_inf = float("-inf")
    pairs = range(gw // PAIR)
    heads = [(pp, hh) for pp in pairs for hh in range(2)]
    sls = [slice(PAIR * pp, PAIR * (pp + 1)) for pp in pairs]

    for sb in range(2):
        rows = slice(blk * sb, blk * (sb + 1))
        if sb == 0:
            valid = in_window & ((n > 0) | (jj >= blk))
            kcat = [jnp.concatenate([prev_ref[0, 0, :, sl], cur_ref[0, 0, rows, sl]], axis=0) for sl in sls]
            vcat = [jnp.concatenate([prev_ref[0, 0, :, gw + PAIR * pp:gw + PAIR * (pp + 1)],
                                     cur_ref[0, 0, rows, gw + PAIR * pp:gw + PAIR * (pp + 1)]], axis=0)
                    for pp in pairs]
        else:
            valid = in_window
            kcat = [cur_ref[0, 0, :, sl] for sl in sls]
            vcat = [cur_ref[0, 0, :, gw + PAIR * pp:gw + PAIR * (pp + 1)] for pp in pairs]
        q2 = [cur_ref[0, 0, rows, 2 * gw + PAIR * pp:2 * gw + PAIR * (pp + 1)] * (ATTN_HEAD ** -0.5)
              for pp in pairs]
        s = [jnp.where(valid,
                       _dot_nt(jnp.where(lane_head == hh, q2[pp], jnp.zeros_like(q2[pp])), kcat[pp])
                       - float(slopes[2 * pp + hh]) * dist, neg_inf) for pp, hh in heads]
        m = [jnp.max(t, axis=-1, keepdims=True) for t in s]
        p = [jnp.exp(s[i] - m[i]) for i in range(len(heads))]
        den = [jnp.sum(t, axis=-1, keepdims=True) for t in p]
        o = [_dot(p[i], vcat[pp]) * (1.0 / den[i]) for i, (pp, _) in enumerate(heads)]
        for pp in pairs:
            o_ref[0, 0, rows, sls[pp]] = jnp.where(lane_head == 0, o[2 * pp], o[2 * pp + 1]).astype(BF16)
        lse = jnp.zeros((blk, LANES), F32)
        for i in range(len(heads)):
            lse = jnp.where(lane == i, m[i] + jnp.log(den[i]), lse)
        l_ref[0, 0, rows, :] = lse


def _attn_group(kvq, dilation, slopes):
    bsz, _, sub, _ = kvq.shape
    gw = ATTN_GROUP_WIDTH
    step = 2 * ATTN_BLK
    return pl.pallas_call(
        functools.partial(_attn_kernel, dilation=dilation, slopes=tuple(slopes)),
        grid=(bsz, dilation, sub // step),
        in_specs=[pl.BlockSpec((1, 1, step, 3 * gw), lambda b, r, n: (b, r, n, 0)),
                  pl.BlockSpec((1, 1, ATTN_BLK, 2 * gw), lambda b, r, n: (b, r, jnp.maximum(2 * n - 1, 0), 0))],
        out_specs=[pl.BlockSpec((1, 1, step, gw), lambda b, r, n: (b, r, n, 0)),
                   pl.BlockSpec((1, 1, step, LANES), lambda b, r, n: (b, r, n, 0))],
        out_shape=[jax.ShapeDtypeStruct((bsz, dilation, sub, gw), BF16),
                   jax.ShapeDtypeStruct((bsz, dilation, sub, LANES), F32)],
        compiler_params=_params("arbitrary", "arbitrary", "arbitrary"),
        name=f"dilated_attn_d{dilation}",
    )(kvq, kvq)


def _mix_kernel(y_ref, bonus_ref, g_ref, za_ref, zb_ref, x_ref, o0_ref, o1_ref, o2_ref,
                l0_ref, l1_ref, l2_ref, gt_ref, lnw_ref, lnb_ref, gpost_ref,
                wor_ref, woa_ref, wout_ref, e_ref, spread_ref, out_ref, *scratch):
    e = e_ref[...]
    inv_n = 1.0 / RWKV_HEAD
    y = y_ref[...]
    mean = _seg_sum(y, e, pieces=2) * inv_n
    yc = y - mean
    var = _seg_sum(yc * yc, e) * inv_n
    yn = yc * lax.rsqrt(var + GN_EPS) * lnw_ref[...] + lnb_ref[...]
    y_rwkv = _dot((yn + bonus_ref[...].astype(F32)) * g_ref[...].astype(F32), wor_ref[...])

    def token_order(ref, scr):
        dilation, rows = ref.shape[1], ref.shape[2]
        if dilation == 1:
            return ref[0, 0].astype(F32)
        for r in range(dilation):
            for cg in range(ref.shape[3] // LANES):
                scr[cg, pl.ds(r, rows, stride=dilation), :] = (
                    ref[0, r, :, LANES * cg:LANES * (cg + 1)].astype(F32))
        return jnp.concatenate([scr[cg] for cg in range(ref.shape[3] // LANES)], axis=1)

    def spread(w):
        return jnp.dot(w.astype(BF16), spread_ref[...], preferred_element_type=F32)

    o0, l0 = token_order(o0_ref, None), token_order(l0_ref, None)
    o1, l1 = token_order(o1_ref, scratch[0]), token_order(l1_ref, scratch[1])
    o2, l2 = token_order(o2_ref, scratch[2]), token_order(l2_ref, scratch[3])
    mx = jnp.maximum(jnp.maximum(l0, l1), l2)
    e0, e1, e2 = jnp.exp(l0 - mx), jnp.exp(l1 - mx), jnp.exp(l2 - mx)
    inv = 1.0 / (e0 + e1 + e2)
    o = spread(e0 * inv) * o0 + spread(e1 * inv) * o1 + spread(e2 * inv) * o2
    y_attn = _dot(o, woa_ref[...])

    mixed = _dot(jax.nn.sigmoid(za_ref[...]) * y_rwkv + jax.nn.sigmoid(zb_ref[...]) * y_attn,
                 wout_ref[...])
    out_ref[...] = x_ref[...] + gt_ref[0] * (_rms(mixed) * gpost_ref[...])


def _mix(y, bonus, g, proj, x2, o_l, gt, ln_w, ln_b, g_post, w_or, w_oa, w_out, e_seg, seq, tm):
    n_tok = x2.shape[0]
    tpb = seq // tm
    row = lambda i: (i, 0)
    const = lambda i: (0, 0)
    tok = pl.BlockSpec((tm, D_MODEL), row)
    vec = pl.BlockSpec((1, D_MODEL), const)

    def att(dilation, width):
        return pl.BlockSpec((1, dilation, tm // dilation, width), lambda i: (i // tpb, 0, i % tpb, 0))

    (o0, l0), (o1, l1), (o2, l2) = o_l
    d0, d1, d2 = [d for _, d in ATTN_GROUPS]
    assert d0 == 1
    gw = ATTN_GROUP_WIDTH
    head_of_lane = jnp.arange(gw) // ATTN_HEAD
    spread = (jnp.arange(LANES)[:, None] == head_of_lane[None, :]).astype(BF16)
    return pl.pallas_call(
        _mix_kernel,
        grid=(n_tok // tm,),
        in_specs=[tok, tok, tok,
                  pl.BlockSpec((tm, D_MODEL), lambda i: (i, COL_ZA // D_MODEL)),
                  pl.BlockSpec((tm, D_MODEL), lambda i: (i, COL_ZB // D_MODEL)),
                  tok, att(d0, gw), att(d1, gw), att(d2, gw), att(d0, LANES), att(d1, LANES), att(d2, LANES),
                  pl.BlockSpec((1, 1, D_MODEL), lambda i: (i // tpb, 0, 0)),
                  vec, vec, vec,
                  pl.BlockSpec((D_MODEL, D_MODEL), const),
                  pl.BlockSpec((gw, D_MODEL), const),
                  pl.BlockSpec((D_MODEL, D_MODEL), const),
                  pl.BlockSpec((SEG_TILE, SEG_TILE), const),
                  pl.BlockSpec((LANES, gw), const)],
        out_specs=tok,
        out_shape=jax.ShapeDtypeStruct((n_tok, D_MODEL), F32),
        scratch_shapes=[pltpu.VMEM((gw // LANES, tm, LANES), F32), pltpu.VMEM((1, tm, LANES), F32)] * 2,
        compiler_params=_params("arbitrary"),
        name="mix_out",
    )(y, bonus, g, proj, proj, x2, o0, o1, o2, l0, l1, l2, gt, ln_w, ln_b, g_post,
      w_or, w_oa, w_out, e_seg, spread)


def _ffn_kernel(x_ref, sc_ref, sh_ref, gt_ref, gpre_ref, gpost_ref, wg_ref, wu_ref, wo_ref, out_ref):
    x = x_ref[...]
    h = (_rms(x) * gpre_ref[...] * (1.0 + sc_ref[0]) + sh_ref[0]).astype(BF16)
    acts = []
    for t in range(FFN_HIDDEN // SEG_TILE):
        sl = slice(SEG_TILE * t, SEG_TILE * (t + 1))
        u_gate = jnp.dot(h, wg_ref[:, sl], preferred_element_type=F32)
        u_up = jnp.dot(h, wu_ref[:, sl], preferred_element_type=F32)
        acts.append((u_gate * jax.nn.sigmoid(u_gate) * u_up).astype(BF16))
    y = jnp.dot(jnp.concatenate(acts, axis=1), wo_ref[...], preferred_element_type=F32)
    out_ref[...] = x + gt_ref[0] * (_rms(y) * gpost_ref[...])


def _ffn(x1, sc, sh, gt, g_pre, g_post, w_in, w_out, seq, tm):
    n_tok = x1.shape[0]
    tpb = seq // tm
    row = lambda i: (i, 0)
    const = lambda i: (0, 0)
    modrow = lambda i: (i // tpb, 0, 0)
    tok = pl.BlockSpec((tm, D_MODEL), row)
    mod = pl.BlockSpec((1, 1, D_MODEL), modrow)
    vec = pl.BlockSpec((1, D_MODEL), const)
    resident = pl.Buffered(1)
    return pl.pallas_call(
        _ffn_kernel,
        grid=(n_tok // tm,),
        in_specs=[tok, mod, mod, mod, vec, vec,
                  pl.BlockSpec((D_MODEL, FFN_HIDDEN), lambda i: (0, 0), pipeline_mode=resident),
                  pl.BlockSpec((D_MODEL, FFN_HIDDEN), lambda i: (0, 1), pipeline_mode=resident),
                  pl.BlockSpec((FFN_HIDDEN, D_MODEL), const, pipeline_mode=resident)],
        out_specs=tok,
        out_shape=jax.ShapeDtypeStruct((n_tok, D_MODEL), F32),
        compiler_params=_params("arbitrary"),
        name="swiglu_ffn",
    )(x1, sc, sh, gt, g_pre, g_post, w_in, w_in, w_out)


def _pad_cols(w, n):
    return jnp.pad(w, ((0, 0), (0, n - w.shape[1]))).astype(BF16)


def _pad_rows(w, n):
    return jnp.pad(w, ((0, n - w.shape[0]), (0, 0))).astype(BF16)


def kernel(x, c, w_mod, b_mod, g_pre_mix, g_post_mix, g_pre_ffn, g_post_ffn, w_in, mu_rkv, mu_lora,
           w0, w1, w2, a0, a1, a2, g1, g2, k_k, k_a, r_k, ln_x_w, ln_x_b, w_o_rwkv, w_o_attn, w_out,
           w_ffn_in, w_ffn_out):
    bsz, seq, _ = x.shape
    depth = w_mod.shape[0]
    assert seq % (2 * ATTN_BLK * ATTN_GROUPS[-1][1]) == 0
    n_tok = bsz * seq
    tm_proj = min(1024, seq)
    tm_aproj = min(512, seq)
    tm_prep = min(256, seq)
    tm_mix = min(256, seq)
    tm_ffn = min(512, seq)

    slopes = _alibi_slopes(len(ATTN_GROUPS) * ATTN_HEADS_PER_GROUP)
    seg = jnp.arange(SEG_TILE) // RWKV_HEAD
    e_seg = (seg[:, None] == seg[None, :]).astype(BF16)
    row = lambda t: t.reshape(1, -1)

    x2 = x.reshape(n_tok, D_MODEL)
    for l in range(depth):
        mod = _mod(c, w_mod[l], b_mod[l])
        sh_m, sc_m, gt_m, sh_f, sc_f, gt_f = [mod[:, i * D_MODEL:(i + 1) * D_MODEL].reshape(bsz, 1, D_MODEL)
                                               for i in range(6)]
        wl = w_in[l]
        rkv_cols = 3 * D_MODEL
        att_cols = 3 * ATTN_WIDTH
        w_main = jnp.concatenate([wl[:, :rkv_cols], wl[:, rkv_cols + att_cols:]], axis=1).astype(BF16)

        proj, h, lw, la, lg = _inproj(
            x2, sc_m, sh_m, row(g_pre_mix[l]), mu_lora[l],
            _pad_cols(w1[l], LORA_DECAY_PAD), _pad_cols(a1[l], LORA_ICLR_PAD),
            _pad_cols(g1[l], LORA_GATE_PAD), w_main, seq, tm_proj)

        r, lwd, k2, v, kk, kka, g, bonus = _prep(
            proj, lw, la, lg, mu_rkv[l], row(w0[l]), row(a0[l]), row(k_k[l]), row(k_a[l]), row(r_k[l]),
            _pad_rows(w2[l], LORA_DECAY_PAD), _pad_rows(a2[l], LORA_ICLR_PAD),
            _pad_rows(g2[l], LORA_GATE_PAD), e_seg, seq, tm_prep)

        b3 = lambda t: t.reshape(bsz, seq, D_MODEL)
        y = _chunk(b3(r), b3(lwd), b3(k2), b3(v), b3(kk), b3(kka)).reshape(n_tok, D_MODEL)

        o_l = []
        for gi, (window, dilation) in enumerate(ATTN_GROUPS):
            assert window // dilation == ATTN_BLK
            gsl = lambda which: wl[:, rkv_cols + which * ATTN_WIDTH + gi * ATTN_GROUP_WIDTH:
                                   rkv_cols + which * ATTN_WIDTH + (gi + 1) * ATTN_GROUP_WIDTH]
            w_qkv = jnp.concatenate([gsl(1), gsl(2), gsl(0)], axis=1).astype(BF16)
            qkv = _attn_proj(h, w_qkv, dilation, bsz, seq, tm_aproj)
            o_l.append(_attn_group(
                qkv, dilation, slopes[gi * ATTN_HEADS_PER_GROUP:(gi + 1) * ATTN_HEADS_PER_GROUP]))

        x2 = _mix(y, bonus, g, proj, x2, o_l, gt_m, row(ln_x_w[l]), row(ln_x_b[l]), row(g_post_mix[l]),
                  w_o_rwkv[l].astype(BF16), w_o_attn[l].astype(BF16), w_out[l].astype(BF16), e_seg,
                  seq, tm_mix)

        x2 = _ffn(x2, sc_f, sh_f, gt_f, row(g_pre_ffn[l]), row(g_post_ffn[l]),
                  w_ffn_in[l].astype(BF16), w_ffn_out[l].astype(BF16), seq, tm_ffn)
    return x2.reshape(bsz, seq, D_MODEL)
```

```python
import functools
import math

import jax
import jax.numpy as jnp
from jax import lax
from jax.experimental import pallas as pl
from jax.experimental.pallas import tpu as pltpu

F32 = jnp.float32
BF16 = jnp.bfloat16
HIGHEST = lax.Precision.HIGHEST

LANES = 128
SEG_TILE = 256
D_MODEL = 1024
RWKV_HEAD = 64
GN_EPS = 64e-5
RMS_EPS = 1e-6
ATTN_HEAD = 64
ATTN_HEADS_PER_GROUP = 8
ATTN_GROUPS = ((128, 1), (512, 4), (2048, 16))
ATTN_GROUP_WIDTH = ATTN_HEADS_PER_GROUP * ATTN_HEAD
ATTN_WIDTH = len(ATTN_GROUPS) * ATTN_GROUP_WIDTH
ATTN_BLK = 128
FFN_HIDDEN = 2816
LORA_DECAY_PAD = 128
LORA_ICLR_PAD = 128
LORA_GATE_PAD = 256
CHUNK = 64
CHUNKS_PER_STEP = 4
PAIR = 2 * RWKV_HEAD
N_PAIRS = D_MODEL // PAIR

COL_R, COL_K, COL_V, COL_ZA, COL_ZB = 0, 1024, 2048, 3072, 4096
MAIN_COLS = 5120
PROJ_TILE = 1280

VMEM_LIMIT = 48 * 1024 * 1024


def _alibi_slopes(n):
    def pow2(m):
        start = 2.0 ** (-8.0 / m)
        return [start ** (i + 1) for i in range(m)]
    if math.log2(n).is_integer():
        s = pow2(n)
    else:
        p = 2 ** int(math.floor(math.log2(n)))
        s = pow2(p) + pow2(2 * p)[0::2][: n - p]
    return sorted(s, reverse=True)


def _dot(a, b):
    return jnp.dot(a.astype(BF16), b.astype(BF16), preferred_element_type=F32)


def _dot_nt(a, b):
    return lax.dot_general(a.astype(BF16), b.astype(BF16), (((1,), (1,)), ((), ())),
                           preferred_element_type=F32)


def _dot_tn(a, b):
    return lax.dot_general(a.astype(BF16), b.astype(BF16), (((0,), (0,)), ((), ())),
                           preferred_element_type=F32)


def _seg_sum(x, e, pieces=1):
    parts = [x.astype(BF16)]
    if pieces == 2:
        parts.append((x - parts[0].astype(F32)).astype(BF16))
    outs = []
    for t in range(x.shape[1] // SEG_TILE):
        sl = slice(SEG_TILE * t, SEG_TILE * (t + 1))
        outs.append(sum(jnp.dot(part[:, sl], e, preferred_element_type=F32) for part in parts))
    return jnp.concatenate(outs, axis=1)


def _shift_rows(t, prev_row):
    rows = lax.broadcasted_iota(jnp.int32, t.shape, 0)
    return jnp.where(rows == 0, prev_row, pltpu.roll(t, 1, 0))


def _rms(t):
    return t * lax.rsqrt(jnp.mean(t * t, axis=-1, keepdims=True) + RMS_EPS)


def _params(*semantics):
    return pltpu.CompilerParams(dimension_semantics=semantics, vmem_limit_bytes=VMEM_LIMIT)


def _mod_kernel(c_ref, w_ref, b_ref, o_ref):
    o_ref[...] = jnp.dot(c_ref[...], w_ref[...], preferred_element_type=F32,
                         precision=HIGHEST) + b_ref[...]


def _mod(c, w_mod, b_mod):
    bsz = c.shape[0]
    n_out = w_mod.shape[1]
    return pl.pallas_call(
        _mod_kernel,
        grid=(n_out // D_MODEL,),
        in_specs=[pl.BlockSpec((bsz, D_MODEL), lambda j: (0, 0)),
                  pl.BlockSpec((D_MODEL, D_MODEL), lambda j: (0, j)),
                  pl.BlockSpec((1, D_MODEL), lambda j: (0, j))],
        out_specs=pl.BlockSpec((bsz, D_MODEL), lambda j: (0, j)),
        out_shape=jax.ShapeDtypeStruct((bsz, n_out), F32),
        compiler_params=_params("arbitrary"),
        name="adaln_mod",
    )(c, w_mod, b_mod.reshape(1, n_out))


def _inproj_kernel(x_ref, sc_ref, sh_ref, g_ref, mu_ref, w1_ref, a1_ref, g1_ref, win_ref,
                   p_ref, h_ref, lw_ref, la_ref, lg_ref, last_scr, *, tiles_per_batch):
    i = pl.program_id(0)
    j = pl.program_id(1)
    tm = x_ref.shape[0]

    @pl.when(j == 0)
    def _():
        @pl.when(i % tiles_per_batch == 0)
        def _():
            last_scr[...] = jnp.zeros_like(last_scr)

        h = _rms(x_ref[...]) * g_ref[...] * (1.0 + sc_ref[0]) + sh_ref[0]
        hs = _shift_rows(h, last_scr[...])
        last_scr[...] = h[tm - 1:tm, :]
        dh = hs - h
        lw_ref[...] = _dot(h + dh * mu_ref[0:1, :], w1_ref[...])
        la_ref[...] = _dot(h + dh * mu_ref[1:2, :], a1_ref[...])
        lg_ref[...] = _dot(h + dh * mu_ref[2:3, :], g1_ref[...])
        h_ref[...] = h.astype(BF16)

    p_ref[...] = jnp.dot(h_ref[...], win_ref[...], preferred_element_type=F32)


def _inproj(x2, sc, sh, g_pre, mu_lora, w1p, a1p, g1p, w_main, seq, tm):
    n_tok = x2.shape[0]
    tpb = seq // tm
    row = lambda i, j: (i, 0)
    const = lambda i, j: (0, 0)
    modrow = lambda i, j: (i // tpb, 0, 0)
    return pl.pallas_call(
        functools.partial(_inproj_kernel, tiles_per_batch=tpb),
        grid=(n_tok // tm, MAIN_COLS // PROJ_TILE),
        in_specs=[pl.BlockSpec((tm, D_MODEL), row),
                  pl.BlockSpec((1, 1, D_MODEL), modrow),
                  pl.BlockSpec((1, 1, D_MODEL), modrow),
                  pl.BlockSpec((1, D_MODEL), const),
                  pl.BlockSpec((3, D_MODEL), const),
                  pl.BlockSpec((D_MODEL, LORA_DECAY_PAD), const),
                  pl.BlockSpec((D_MODEL, LORA_ICLR_PAD), const),
                  pl.BlockSpec((D_MODEL, LORA_GATE_PAD), const),
                  pl.BlockSpec((D_MODEL, PROJ_TILE), lambda i, j: (0, j))],
        out_specs=[pl.BlockSpec((tm, PROJ_TILE), lambda i, j: (i, j)),
                   pl.BlockSpec((tm, D_MODEL), row),
                   pl.BlockSpec((tm, LORA_DECAY_PAD), row),
                   pl.BlockSpec((tm, LORA_ICLR_PAD), row),
                   pl.BlockSpec((tm, LORA_GATE_PAD), row)],
        out_shape=[jax.ShapeDtypeStruct((n_tok, MAIN_COLS), F32),
                   jax.ShapeDtypeStruct((n_tok, D_MODEL), BF16),
                   jax.ShapeDtypeStruct((n_tok, LORA_DECAY_PAD), F32),
                   jax.ShapeDtypeStruct((n_tok, LORA_ICLR_PAD), F32),
                   jax.ShapeDtypeStruct((n_tok, LORA_GATE_PAD), F32)],
        scratch_shapes=[pltpu.VMEM((1, D_MODEL), F32)],
        compiler_params=_params("arbitrary", "arbitrary"),
        name="in_proj",
    )(x2, sc, sh, g_pre, mu_lora, w1p, a1p, g1p, w_main)


def _attn_proj_kernel(h_ref, w_ref, o_ref, *scratch, dilation):
    acc = jnp.dot(h_ref[...], w_ref[...], preferred_element_type=F32)
    if dilation == 1:
        o_ref[0, 0] = acc.astype(BF16)
    else:
        scr, = scratch
        rows = acc.shape[0] // dilation
        for cg in range(scr.shape[0]):
            scr[cg] = acc[:, LANES * cg:LANES * (cg + 1)]
        for r in range(dilation):
            for cg in range(scr.shape[0]):
                o_ref[0, r, :, LANES * cg:LANES * (cg + 1)] = (
                    scr[cg, pl.ds(r, rows, stride=dilation), :].astype(BF16))


def _attn_proj(h, w_qkv, dilation, bsz, seq, tm):
    tpb = seq // tm
    width = 3 * ATTN_GROUP_WIDTH
    return pl.pallas_call(
        functools.partial(_attn_proj_kernel, dilation=dilation),
        grid=(bsz * tpb,),
        in_specs=[pl.BlockSpec((tm, D_MODEL), lambda i: (i, 0)),
                  pl.BlockSpec((D_MODEL, width), lambda i: (0, 0))],
        out_specs=pl.BlockSpec((1, dilation, tm // dilation, width), lambda i: (i // tpb, 0, i % tpb, 0)),
        out_shape=jax.ShapeDtypeStruct((bsz, dilation, seq // dilation, width), BF16),
        scratch_shapes=[] if dilation == 1 else [pltpu.VMEM((width // LANES, tm, LANES), F32)],
        compiler_params=_params("arbitrary"),
        name=f"attn_proj_d{dilation}",
    )(h, w_qkv)


def _prep_kernel(pr_ref, pk_ref, pv_ref, lw_ref, la_ref, lg_ref, mu_ref, w0_ref, a0_ref, kk_ref,
                 ka_ref, rk_ref, w2_ref, a2_ref, g2_ref, e_ref,
                 r_o, lw_o, k_o, v_o, kk_o, kka_o, g_o, bonus_o, last_scr, *, tiles_per_batch):
    i = pl.program_id(0)
    tm = pr_ref.shape[0]

    @pl.when(i % tiles_per_batch == 0)
    def _():
        last_scr[...] = jnp.zeros_like(last_scr)

    def mix(p_ref, idx):
        p = p_ref[...]
        ps = _shift_rows(p, last_scr[idx:idx + 1, :])
        last_scr[idx:idx + 1, :] = p[tm - 1:tm, :]
        return p + (ps - p) * mu_ref[idx:idx + 1, :]

    r = mix(pr_ref, 0)
    k = mix(pk_ref, 1)
    v = mix(pv_ref, 2)
    e = e_ref[...]

    z = w0_ref[...] + _dot(jnp.tanh(lw_ref[...]), w2_ref[...])
    lw_o[...] = -math.exp(-0.5) * jax.nn.sigmoid(z)
    a = jax.nn.sigmoid(a0_ref[...] + _dot(la_ref[...], a2_ref[...]))
    g_o[...] = _dot(jax.nn.sigmoid(lg_ref[...]), g2_ref[...]).astype(BF16)

    kk = k * kk_ref[...]
    kk = kk * lax.rsqrt(jnp.maximum(_seg_sum(kk * kk, e), 1e-24))
    k2 = k * (1.0 + (a - 1.0) * ka_ref[...])
    r_o[...] = r.astype(BF16)
    k_o[...] = k2.astype(BF16)
    v_o[...] = v.astype(BF16)
    kk_o[...] = kk.astype(BF16)
    kka_o[...] = (kk * a).astype(BF16)
    bonus_o[...] = (_seg_sum(r * k2 * rk_ref[...], e) * v).astype(BF16)


def _prep(proj, lw, la, lg, mu_rkv, w0, a0, k_k, k_a, r_k, w2p, a2p, g2p, e_seg, seq, tm):
    n_tok = proj.shape[0]
    tpb = seq // tm
    row = lambda i: (i, 0)
    const = lambda i: (0, 0)
    vec = pl.BlockSpec((1, D_MODEL), const)
    tok = pl.BlockSpec((tm, D_MODEL), row)
    return pl.pallas_call(
        functools.partial(_prep_kernel, tiles_per_batch=tpb),
        grid=(n_tok // tm,),
        in_specs=[pl.BlockSpec((tm, D_MODEL), lambda i: (i, COL_R // D_MODEL)),
                  pl.BlockSpec((tm, D_MODEL), lambda i: (i, COL_K // D_MODEL)),
                  pl.BlockSpec((tm, D_MODEL), lambda i: (i, COL_V // D_MODEL)),
                  pl.BlockSpec((tm, LORA_DECAY_PAD), row),
                  pl.BlockSpec((tm, LORA_ICLR_PAD), row),
                  pl.BlockSpec((tm, LORA_GATE_PAD), row),
                  pl.BlockSpec((3, D_MODEL), const),
                  vec, vec, vec, vec, vec,
                  pl.BlockSpec((LORA_DECAY_PAD, D_MODEL), const),
                  pl.BlockSpec((LORA_ICLR_PAD, D_MODEL), const),
                  pl.BlockSpec((LORA_GATE_PAD, D_MODEL), const),
                  pl.BlockSpec((SEG_TILE, SEG_TILE), const)],
        out_specs=[tok] * 8,
        out_shape=[jax.ShapeDtypeStruct((n_tok, D_MODEL), F32 if name == "lw" else BF16)
                   for name in ("r", "lw", "k", "v", "kk", "kka", "g", "bonus")],
        scratch_shapes=[pltpu.VMEM((3, D_MODEL), F32)],
        compiler_params=_params("arbitrary"),
        name="rwkv_prep",
    )(proj, proj, proj, lw, la, lg, mu_rkv, w0, a0, k_k, k_a, r_k, w2p, a2p, g2p, e_seg)


def _chunk_kernel(r_ref, lw_ref, k_ref, v_ref, kk_ref, kka_ref, y_ref, s_scr):
    @pl.when(pl.program_id(1) == 0)
    def _():
        s_scr[...] = jnp.zeros_like(s_scr)

    c = CHUNK
    rows = lw_ref.shape[1]
    n_chunks = rows // c
    lw = lw_ref[0]
    ti = lax.broadcasted_iota(jnp.int32, (rows, rows), 0)
    tj = lax.broadcasted_iota(jnp.int32, (rows, rows), 1)
    tril = ((ti >= tj) & (ti // c == tj // c)).astype(F32)
    cum = jnp.dot(tril, lw, preferred_element_type=F32, precision=HIGHEST)
    cum_prev = cum - lw
    cum_end = jnp.concatenate(
        [jnp.broadcast_to(cum[c * (ck + 1) - 1:c * (ck + 1), :], (c, D_MODEL)) for ck in range(n_chunks)], axis=0)
    kk = kk_ref[0].astype(F32)
    kka = kka_ref[0].astype(F32)
    k = k_ref[0].astype(F32)
    e_neg = jnp.exp(-cum)
    e_end = jnp.exp(cum_end - cum)
    at_all = -kk * jnp.exp(cum_prev)
    rt_all = r_ref[0].astype(F32) * jnp.exp(cum)
    bt_all = kka * e_neg
    kt_all = k * e_neg
    bh_all = kka * e_end
    kh_all = k * e_end
    p_end = jnp.exp(cum_end)
    v_all = v_ref[0].astype(F32)

    ri = lax.broadcasted_iota(jnp.int32, (PAIR, PAIR), 0)
    ci = lax.broadcasted_iota(jnp.int32, (PAIR, PAIR), 1)
    same = (ri // RWKV_HEAD) == (ci // RWKV_HEAD)
    same2 = jnp.concatenate([same, same], axis=1)
    eye = ri == ci
    tt = lax.broadcasted_iota(jnp.int32, (c, PAIR), 0)
    ss = lax.broadcasted_iota(jnp.int32, (c, PAIR), 1) % c
    m_strict = tt > ss
    m_incl = tt >= ss
    eye_rows = (tt == ss).astype(F32)
    zeros_c = jnp.zeros((c, PAIR), F32)

    def bd(t):
        mask = same if t.shape[1] == PAIR else same2
        return jnp.where(mask, jnp.concatenate([t, t], axis=0), 0.0).astype(BF16)

    items = [(slice(c * ck, c * (ck + 1)), slice(PAIR * p, PAIR * (p + 1)))
             for ck in range(n_chunks) for p in range(N_PAIRS)]
    pairs = range(len(items))
    at = [at_all[rs, sl] for rs, sl in items]
    rt = [rt_all[rs, sl] for rs, sl in items]
    v_bd = [bd(v_all[rs, sl]) for rs, sl in items]

    quad = [_dot_nt(jnp.concatenate([at[p], rt[p]], axis=0),
                    jnp.concatenate([bd(bt_all[items[p]]), bd(kt_all[items[p]])], axis=0))
            for p in pairs]
    a_ab = [jnp.where(m_strict, q[:c, :PAIR], 0.0) for q in quad]
    a_ak = [jnp.where(m_strict, q[:c, PAIR:], 0.0) for q in quad]
    a_rb = [jnp.where(m_incl, q[c:, :PAIR], 0.0) for q in quad]
    a_rk = [jnp.where(m_incl, q[c:, PAIR:], 0.0) for q in quad]
    av = [_dot(jnp.concatenate([a_ak[p], a_rk[p]], axis=0), v_bd[p]) for p in pairs]
    x0 = [jnp.concatenate([at[p], av[p][:c]], axis=1) for p in pairs]

    t_inv = [eye_rows + a_ab[p] for p in pairs]
    pw = [_dot(a_ab[p], bd(a_ab[p])) for p in pairs]
    t_inv = [t_inv[p] + _dot(pw[p], bd(t_inv[p])) for p in pairs]
    pw = [_dot(pw[p], bd(pw[p])) for p in pairs]
    for _ in range(int(math.log2(c)) - 3):
        both = [_dot(jnp.concatenate([t_inv[p], pw[p]], axis=0), bd(pw[p])) for p in pairs]
        t_inv = [t_inv[p] + both[p][:c] for p in pairs]
        pw = [both[p][c:] for p in pairs]
    t_inv = [t_inv[p] + _dot(t_inv[p], bd(pw[p])) for p in pairs]
    x = [_dot(t_inv[p], bd(x0[p])) for p in pairs]

    top = [_dot(a_rb[p], bd(x[p])) for p in pairs]
    bot = [_dot_tn(jnp.concatenate([bh_all[items[p]], kh_all[items[p]]], axis=0),
                   jnp.concatenate([x[p], jnp.concatenate([zeros_c, v_all[items[p]]], axis=1)], axis=0))
           for p in pairs]

    state = [s_scr[hp] for hp in range(N_PAIRS)]
    for p in pairs:
        rs, sl = items[p]
        hp = p % N_PAIRS
        p_end_row = p_end[rs, sl][c - 1:c, :]
        m_mat = jnp.where(same, bot[p][:, :PAIR], 0.0) + jnp.where(
            eye, jnp.broadcast_to(p_end_row, (PAIR, PAIR)), 0.0)
        lhs = jnp.concatenate([rt[p] + top[p][:, :PAIR], m_mat], axis=0)
        fin = _dot(lhs, state[hp])
        y_ref[0, rs, sl] = fin[:c] + top[p][:, PAIR:] + av[p][c:]
        state[hp] = fin[c:] + jnp.where(same, bot[p][:, PAIR:], 0.0)
    for hp in range(N_PAIRS):
        s_scr[hp] = state[hp]


def _chunk(r, lw, k, v, kk, kka):
    bsz, seq, _ = r.shape
    spec = pl.BlockSpec((1, CHUNKS_PER_STEP * CHUNK, D_MODEL), lambda b, c: (b, c, 0))
    return pl.pallas_call(
        _chunk_kernel,
        grid=(bsz, seq // (CHUNKS_PER_STEP * CHUNK)),
        in_specs=[spec] * 6,
        out_specs=spec,
        out_shape=jax.ShapeDtypeStruct((bsz, seq, D_MODEL), F32),
        scratch_shapes=[pltpu.VMEM((N_PAIRS, PAIR, PAIR), F32)],
        compiler_params=_params("arbitrary", "arbitrary"),
        name="rwkv_chunk",
    )(r, lw, k, v, kk, kka)


def _attn_kernel(cur_ref, prev_ref, o_ref, l_ref, *, dilation, slopes):
    n = pl.program_id(2)
    blk = ATTN_BLK
    gw = ATTN_GROUP_WIDTH
    ii = lax.broadcasted_iota(jnp.int32, (blk, 2 * blk), 0)
    jj = lax.broadcasted_iota(jnp.int32, (blk, 2 * blk), 1)
    delta = ii - jj + blk
    in_window = (delta >= 0) & (delta <= blk)
    dist = delta.astype(F32) * float(dilation)
    lane_head = lax.broadcasted_iota(jnp.int32, (blk, PAIR), 1) // ATTN_HEAD
    lane = lax.broadcasted_iota(jnp.int32, (blk, LANES), 1)
    neg_inf = float("-inf")
    pairs = range(gw // PAIR)
    heads = [(pp, hh) for pp in pairs for hh in range(2)]
    sls = [slice(PAIR * pp, PAIR * (pp + 1)) for pp in pairs]

    for sb in range(2):
        rows = slice(blk * sb, blk * (sb + 1))
        if sb == 0:
            valid = in_window & ((n > 0) | (jj >= blk))
            kcat = [jnp.concatenate([prev_ref[0, 0, :, sl], cur_ref[0, 0, rows, sl]], axis=0) for sl in sls]
            vcat = [jnp.concatenate([prev_ref[0, 0, :, gw + PAIR * pp:gw + PAIR * (pp + 1)],
                                     cur_ref[0, 0, rows, gw + PAIR * pp:gw + PAIR * (pp + 1)]], axis=0)
                    for pp in pairs]
        else:
            valid = in_window
            kcat = [cur_ref[0, 0, :, sl] for sl in sls]
            vcat = [cur_ref[0, 0, :, gw + PAIR * pp:gw + PAIR * (pp + 1)] for pp in pairs]
        q2 = [cur_ref[0, 0, rows, 2 * gw + PAIR * pp:2 * gw + PAIR * (pp + 1)] * (ATTN_HEAD ** -0.5)
              for pp in pairs]
        s = [jnp.where(valid,
                       _dot_nt(jnp.where(lane_head == hh, q2[pp], jnp.zeros_like(q2[pp])), kcat[pp])
                       - float(slopes[2 * pp + hh]) * dist, neg_inf) for pp, hh in heads]
        m = [jnp.max(t, axis=-1, keepdims=True) for t in s]
        p = [jnp.exp(s[i] - m[i]) for i in range(len(heads))]
        den = [jnp.sum(t, axis=-1, keepdims=True) for t in p]
        o = [_dot(p[i], vcat[pp]) * (1.0 / den[i]) for i, (pp, _) in enumerate(heads)]
        for pp in pairs:
            o_ref[0, 0, rows, sls[pp]] = jnp.where(lane_head == 0, o[2 * pp], o[2 * pp + 1]).astype(BF16)
        lse = jnp.zeros((blk, LANES), F32)
        for i in range(len(heads)):
            lse = jnp.where(lane == i, m[i] + jnp.log(den[i]), lse)
        l_ref[0, 0, rows, :] = lse


def _attn_group(kvq, dilation, slopes):
    bsz, _, sub, _ = kvq.shape
    gw = ATTN_GROUP_WIDTH
    step = 2 * ATTN_BLK
    return pl.pallas_call(
        functools.partial(_attn_kernel, dilation=dilation, slopes=tuple(slopes)),
        grid=(bsz, dilation, sub // step),
        in_specs=[pl.BlockSpec((1, 1, step, 3 * gw), lambda b, r, n: (b, r, n, 0)),
                  pl.BlockSpec((1, 1, ATTN_BLK, 2 * gw), lambda b, r, n: (b, r, jnp.maximum(2 * n - 1, 0), 0))],
        out_specs=[pl.BlockSpec((1, 1, step, gw), lambda b, r, n: (b, r, n, 0)),
                   pl.BlockSpec((1, 1, step, LANES), lambda b, r, n: (b, r, n, 0))],
        out_shape=[jax.ShapeDtypeStruct((bsz, dilation, sub, gw), BF16),
                   jax.ShapeDtypeStruct((bsz, dilation, sub, LANES), F32)],
        compiler_params=_params("arbitrary", "arbitrary", "arbitrary"),
        name=f"dilated_attn_d{dilation}",
    )(kvq, kvq)


def _mix_kernel(y_ref, bonus_ref, g_ref, za_ref, zb_ref, x_ref, o0_ref, o1_ref, o2_ref,
                l0_ref, l1_ref, l2_ref, gt_ref, lnw_ref, lnb_ref, gpost_ref,
                wor_ref, woa_ref, wout_ref, e_ref, spread_ref, out_ref, *scratch):
    e = e_ref[...]
    inv_n = 1.0 / RWKV_HEAD
    y = y_ref[...]
    mean = _seg_sum(y, e, pieces=2) * inv_n
    yc = y - mean
    var = _seg_sum(yc * yc, e) * inv_n
    yn = yc * lax.rsqrt(var + GN_EPS) * lnw_ref[...] + lnb_ref[...]
    y_rwkv = _dot((yn + bonus_ref[...].astype(F32)) * g_ref[...].astype(F32), wor_ref[...])

    def token_order(ref, scr):
        dilation, rows = ref.shape[1], ref.shape[2]
        if dilation == 1:
            return ref[0, 0].astype(F32)
        for r in range(dilation):
            for cg in range(ref.shape[3] // LANES):
                scr[cg, pl.ds(r, rows, stride=dilation), :] = (
                    ref[0, r, :, LANES * cg:LANES * (cg + 1)].astype(F32))
        return jnp.concatenate([scr[cg] for cg in range(ref.shape[3] // LANES)], axis=1)

    def spread(w):
        return jnp.dot(w.astype(BF16), spread_ref[...], preferred_element_type=F32)

    o0, l0 = token_order(o0_ref, None), token_order(l0_ref, None)
    o1, l1 = token_order(o1_ref, scratch[0]), token_order(l1_ref, scratch[1])
    o2, l2 = token_order(o2_ref, scratch[2]), token_order(l2_ref, scratch[3])
    mx = jnp.maximum(jnp.maximum(l0, l1), l2)
    e0, e1, e2 = jnp.exp(l0 - mx), jnp.exp(l1 - mx), jnp.exp(l2 - mx)
    inv = 1.0 / (e0 + e1 + e2)
    o = spread(e0 * inv) * o0 + spread(e1 * inv) * o1 + spread(e2 * inv) * o2
    y_attn = _dot(o, woa_ref[...])

    mixed = _dot(jax.nn.sigmoid(za_ref[...]) * y_rwkv + jax.nn.sigmoid(zb_ref[...]) * y_attn,
                 wout_ref[...])
    out_ref[...] = x_ref[...] + gt_ref[0] * (_rms(mixed) * gpost_ref[...])


def _mix(y, bonus, g, proj, x2, o_l, gt, ln_w, ln_b, g_post, w_or, w_oa, w_out, e_seg, seq, tm):
    n_tok = x2.shape[0]
    tpb = seq // tm
    row = lambda i: (i, 0)
    const = lambda i: (0, 0)
    tok = pl.BlockSpec((tm, D_MODEL), row)
    vec = pl.BlockSpec((1, D_MODEL), const)

    def att(dilation, width):
        return pl.BlockSpec((1, dilation, tm // dilation, width), lambda i: (i // tpb, 0, i % tpb, 0))

    (o0, l0), (o1, l1), (o2, l2) = o_l
    d0, d1, d2 = [d for _, d in ATTN_GROUPS]
    assert d0 == 1
    gw = ATTN_GROUP_WIDTH
    head_of_lane = jnp.arange(gw) // ATTN_HEAD
    spread = (jnp.arange(LANES)[:, None] == head_of_lane[None, :]).astype(BF16)
    return pl.pallas_call(
        _mix_kernel,
        grid=(n_tok // tm,),
        in_specs=[tok, tok, tok,
                  pl.BlockSpec((tm, D_MODEL), lambda i: (i, COL_ZA // D_MODEL)),
                  pl.BlockSpec((tm, D_MODEL), lambda i: (i, COL_ZB // D_MODEL)),
                  tok, att(d0, gw), att(d1, gw), att(d2, gw), att(d0, LANES), att(d1, LANES), att(d2, LANES),
                  pl.BlockSpec((1, 1, D_MODEL), lambda i: (i // tpb, 0, 0)),
                  vec, vec, vec,
                  pl.BlockSpec((D_MODEL, D_MODEL), const, pipeline_mode=pl.Buffered(1)),
                  pl.BlockSpec((gw, D_MODEL), const, pipeline_mode=pl.Buffered(1)),
                  pl.BlockSpec((D_MODEL, D_MODEL), const, pipeline_mode=pl.Buffered(1)),
                  pl.BlockSpec((SEG_TILE, SEG_TILE), const),
                  pl.BlockSpec((LANES, gw), const)],
        out_specs=tok,
        out_shape=jax.ShapeDtypeStruct((n_tok, D_MODEL), F32),
        scratch_shapes=[pltpu.VMEM((gw // LANES, tm, LANES), F32), pltpu.VMEM((1, tm, LANES), F32)] * 2,
        compiler_params=_params("arbitrary"),
        name="mix_out",
    )(y, bonus, g, proj, proj, x2, o0, o1, o2, l0, l1, l2, gt, ln_w, ln_b, g_post,
      w_or, w_oa, w_out, e_seg, spread)


def _ffn_kernel(x_ref, sc_ref, sh_ref, gt_ref, gpre_ref, gpost_ref, wg_ref, wu_ref, wo_ref, out_ref):
    x = x_ref[...]
    h = (_rms(x) * gpre_ref[...] * (1.0 + sc_ref[0]) + sh_ref[0]).astype(BF16)
    acts = []
    for t in range(FFN_HIDDEN // SEG_TILE):
        sl = slice(SEG_TILE * t, SEG_TILE * (t + 1))
        u_gate = jnp.dot(h, wg_ref[:, sl], preferred_element_type=F32)
        u_up = jnp.dot(h, wu_ref[:, sl], preferred_element_type=F32)
        acts.append((u_gate * jax.nn.sigmoid(u_gate) * u_up).astype(BF16))
    y = jnp.dot(jnp.concatenate(acts, axis=1), wo_ref[...], preferred_element_type=F32)
    out_ref[...] = x + gt_ref[0] * (_rms(y) * gpost_ref[...])


def _ffn(x1, sc, sh, gt, g_pre, g_post, w_in, w_out, seq, tm):
    n_tok = x1.shape[0]
    tpb = seq // tm
    row = lambda i: (i, 0)
    const = lambda i: (0, 0)
    modrow = lambda i: (i // tpb, 0, 0)
    tok = pl.BlockSpec((tm, D_MODEL), row)
    mod = pl.BlockSpec((1, 1, D_MODEL), modrow)
    vec = pl.BlockSpec((1, D_MODEL), const)
    resident = pl.Buffered(1)
    return pl.pallas_call(
        _ffn_kernel,
        grid=(n_tok // tm,),
        in_specs=[tok, mod, mod, mod, vec, vec,
                  pl.BlockSpec((D_MODEL, FFN_HIDDEN), lambda i: (0, 0), pipeline_mode=resident),
                  pl.BlockSpec((D_MODEL, FFN_HIDDEN), lambda i: (0, 1), pipeline_mode=resident),
                  pl.BlockSpec((FFN_HIDDEN, D_MODEL), const, pipeline_mode=resident)],
        out_specs=tok,
        out_shape=jax.ShapeDtypeStruct((n_tok, D_MODEL), F32),
        compiler_params=_params("arbitrary"),
        name="swiglu_ffn",
    )(x1, sc, sh, gt, g_pre, g_post, w_in, w_in, w_out)


def _pad_cols(w, n):
    return jnp.pad(w, ((0, 0), (0, n - w.shape[1]))).astype(BF16)


def _pad_rows(w, n):
    return jnp.pad(w, ((0, n - w.shape[0]), (0, 0))).astype(BF16)


def kernel(x, c, w_mod, b_mod, g_pre_mix, g_post_mix, g_pre_ffn, g_post_ffn, w_in, mu_rkv, mu_lora,
           w0, w1, w2, a0, a1, a2, g1, g2, k_k, k_a, r_k, ln_x_w, ln_x_b, w_o_rwkv, w_o_attn, w_out,
           w_ffn_in, w_ffn_out):
    bsz, seq, _ = x.shape
    depth = w_mod.shape[0]
    assert seq % (2 * ATTN_BLK * ATTN_GROUPS[-1][1]) == 0
    n_tok = bsz * seq
    tm_proj = min(1024, seq)
    tm_aproj = min(512, seq)
    tm_prep = min(512, seq)
    tm_mix = min(512, seq)
    tm_ffn = min(512, seq)

    slopes = _alibi_slopes(len(ATTN_GROUPS) * ATTN_HEADS_PER_GROUP)
    seg = jnp.arange(SEG_TILE) // RWKV_HEAD
    e_seg = (seg[:, None] == seg[None, :]).astype(BF16)
    row = lambda t: t.reshape(1, -1)

    x2 = x.reshape(n_tok, D_MODEL)
    for l in range(depth):
        mod = _mod(c, w_mod[l], b_mod[l])
        sh_m, sc_m, gt_m, sh_f, sc_f, gt_f = [mod[:, i * D_MODEL:(i + 1) * D_MODEL].reshape(bsz, 1, D_MODEL)
                                               for i in range(6)]
        wl = w_in[l]
        rkv_cols = 3 * D_MODEL
        att_cols = 3 * ATTN_WIDTH
        w_main = jnp.concatenate([wl[:, :rkv_cols], wl[:, rkv_cols + att_cols:]], axis=1).astype(BF16)

        proj, h, lw, la, lg = _inproj(
            x2, sc_m, sh_m, row(g_pre_mix[l]), mu_lora[l],
            _pad_cols(w1[l], LORA_DECAY_PAD), _pad_cols(a1[l], LORA_ICLR_PAD),
            _pad_cols(g1[l], LORA_GATE_PAD), w_main, seq, tm_proj)

        r, lwd, k2, v, kk, kka, g, bonus = _prep(
            proj, lw, la, lg, mu_rkv[l], row(w0[l]), row(a0[l]), row(k_k[l]), row(k_a[l]), row(r_k[l]),
            _pad_rows(w2[l], LORA_DECAY_PAD), _pad_rows(a2[l], LORA_ICLR_PAD),
            _pad_rows(g2[l], LORA_GATE_PAD), e_seg, seq, tm_prep)

        b3 = lambda t: t.reshape(bsz, seq, D_MODEL)
        y = _chunk(b3(r), b3(lwd), b3(k2), b3(v), b3(kk), b3(kka)).reshape(n_tok, D_MODEL)

        o_l = []
        for gi, (window, dilation) in enumerate(ATTN_GROUPS):
            assert window // dilation == ATTN_BLK
            gsl = lambda which: wl[:, rkv_cols + which * ATTN_WIDTH + gi * ATTN_GROUP_WIDTH:
                                   rkv_cols + which * ATTN_WIDTH + (gi + 1) * ATTN_GROUP_WIDTH]
            w_qkv = jnp.concatenate([gsl(1), gsl(2), gsl(0)], axis=1).astype(BF16)
            qkv = _attn_proj(h, w_qkv, dilation, bsz, seq, tm_aproj)
            o_l.append(_attn_group(
                qkv, dilation, slopes[gi * ATTN_HEADS_PER_GROUP:(gi + 1) * ATTN_HEADS_PER_GROUP]))

        x2 = _mix(y, bonus, g, proj, x2, o_l, gt_m, row(ln_x_w[l]), row(ln_x_b[l]), row(g_post_mix[l]),
                  w_o_rwkv[l].astype(BF16), w_o_attn[l].astype(BF16), w_out[l].astype(BF16), e_seg,
                  seq, tm_mix)

        x2 = _ffn(x2, sc_f, sh_f, gt_f, row(g_pre_ffn[l]), row(g_post_ffn[l]),
                  w_ffn_in[l].astype(BF16), w_ffn_out[l].astype(BF16), seq, tm_ffn)
    return x2.reshape(bsz, seq, D_MODEL)
```

```python
import functools
import math

import jax
import jax.numpy as jnp
from jax import lax
from jax.experimental import pallas as pl
from jax.experimental.pallas import tpu as pltpu

F32 = jnp.float32
BF16 = jnp.bfloat16
HIGHEST = lax.Precision.HIGHEST

LANES = 128
SEG_TILE = 256
D_MODEL = 1024
RWKV_HEAD = 64
GN_EPS = 64e-5
RMS_EPS = 1e-6
ATTN_HEAD = 64
ATTN_HEADS_PER_GROUP = 8
ATTN_GROUPS = ((128, 1), (512, 4), (2048, 16))
ATTN_GROUP_WIDTH = ATTN_HEADS_PER_GROUP * ATTN_HEAD
ATTN_WIDTH = len(ATTN_GROUPS) * ATTN_GROUP_WIDTH
ATTN_BLK = 128
FFN_HIDDEN = 2816
LORA_DECAY_PAD = 128
LORA_ICLR_PAD = 128
LORA_GATE_PAD = 256
CHUNK = 64
CHUNKS_PER_STEP = 4
PAIR = 2 * RWKV_HEAD
N_PAIRS = D_MODEL // PAIR

COL_R, COL_K, COL_V, COL_ZA, COL_ZB = 0, 1024, 2048, 3072, 4096
MAIN_COLS = 5120
PROJ_TILE = 5120

VMEM_LIMIT = 48 * 1024 * 1024


def _alibi_slopes(n):
    def pow2(m):
        start = 2.0 ** (-8.0 / m)
        return [start ** (i + 1) for i in range(m)]
    if math.log2(n).is_integer():
        s = pow2(n)
    else:
        p = 2 ** int(math.floor(math.log2(n)))
        s = pow2(p) + pow2(2 * p)[0::2][: n - p]
    return sorted(s, reverse=True)


def _dot(a, b):
    return jnp.dot(a.astype(BF16), b.astype(BF16), preferred_element_type=F32)


def _dot_nt(a, b):
    return lax.dot_general(a.astype(BF16), b.astype(BF16), (((1,), (1,)), ((), ())),
                           preferred_element_type=F32)


def _dot_tn(a, b):
    return lax.dot_general(a.astype(BF16), b.astype(BF16), (((0,), (0,)), ((), ())),
                           preferred_element_type=F32)


def _seg_sum(x, e, pieces=1):
    parts = [x.astype(BF16)]
    if pieces == 2:
        parts.append((x - parts[0].astype(F32)).astype(BF16))
    outs = []
    for t in range(x.shape[1] // SEG_TILE):
        sl = slice(SEG_TILE * t, SEG_TILE * (t + 1))
        outs.append(sum(jnp.dot(part[:, sl], e, preferred_element_type=F32) for part in parts))
    return jnp.concatenate(outs, axis=1)


def _shift_rows(t, prev_row):
    rows = lax.broadcasted_iota(jnp.int32, t.shape, 0)
    return jnp.where(rows == 0, prev_row, pltpu.roll(t, 1, 0))


def _rms(t):
    return t * lax.rsqrt(jnp.mean(t * t, axis=-1, keepdims=True) + RMS_EPS)


def _params(*semantics):
    return pltpu.CompilerParams(dimension_semantics=semantics, vmem_limit_bytes=VMEM_LIMIT)


def _mod_kernel(c_ref, w_ref, b_ref, o_ref):
    o_ref[...] = jnp.dot(c_ref[...], w_ref[...], preferred_element_type=F32,
                         precision=HIGHEST) + b_ref[...]


def _mod(c, w_mod, b_mod):
    bsz = c.shape[0]
    n_out = w_mod.shape[1]
    return pl.pallas_call(
        _mod_kernel,
        grid=(n_out // D_MODEL,),
        in_specs=[pl.BlockSpec((bsz, D_MODEL), lambda j: (0, 0)),
                  pl.BlockSpec((D_MODEL, D_MODEL), lambda j: (0, j)),
                  pl.BlockSpec((1, D_MODEL), lambda j: (0, j))],
        out_specs=pl.BlockSpec((bsz, D_MODEL), lambda j: (0, j)),
        out_shape=jax.ShapeDtypeStruct((bsz, n_out), F32),
        compiler_params=_params("arbitrary"),
        name="adaln_mod",
    )(c, w_mod, b_mod.reshape(1, n_out))


def _inproj_kernel(x_ref, sc_ref, sh_ref, g_ref, mu_ref, w1_ref, a1_ref, g1_ref, win_ref,
                   p_ref, h_ref, lw_ref, la_ref, lg_ref, last_scr, *, tiles_per_batch):
    i = pl.program_id(0)
    j = pl.program_id(1)
    tm = x_ref.shape[0]

    @pl.when(j == 0)
    def _():
        @pl.when(i % tiles_per_batch == 0)
        def _():
            last_scr[...] = jnp.zeros_like(last_scr)

        h = _rms(x_ref[...]) * g_ref[...] * (1.0 + sc_ref[0]) + sh_ref[0]
        hs = _shift_rows(h, last_scr[...])
        last_scr[...] = h[tm - 1:tm, :]
        dh = hs - h
        lw_ref[...] = _dot(h + dh * mu_ref[0:1, :], w1_ref[...])
        la_ref[...] = _dot(h + dh * mu_ref[1:2, :], a1_ref[...])
        lg_ref[...] = _dot(h + dh * mu_ref[2:3, :], g1_ref[...])
        h_ref[...] = h.astype(BF16)

    p_ref[...] = jnp.dot(h_ref[...], win_ref[...], preferred_element_type=F32).astype(BF16)


def _inproj(x2, sc, sh, g_pre, mu_lora, w1p, a1p, g1p, w_main, seq, tm):
    n_tok = x2.shape[0]
    tpb = seq // tm
    row = lambda i, j: (i, 0)
    const = lambda i, j: (0, 0)
    modrow = lambda i, j: (i // tpb, 0, 0)
    return pl.pallas_call(
        functools.partial(_inproj_kernel, tiles_per_batch=tpb),
        grid=(n_tok // tm, MAIN_COLS // PROJ_TILE),
        in_specs=[pl.BlockSpec((tm, D_MODEL), row),
                  pl.BlockSpec((1, 1, D_MODEL), modrow),
                  pl.BlockSpec((1, 1, D_MODEL), modrow),
                  pl.BlockSpec((1, D_MODEL), const),
                  pl.BlockSpec((3, D_MODEL), const),
                  pl.BlockSpec((D_MODEL, LORA_DECAY_PAD), const),
                  pl.BlockSpec((D_MODEL, LORA_ICLR_PAD), const),
                  pl.BlockSpec((D_MODEL, LORA_GATE_PAD), const),
                  pl.BlockSpec((D_MODEL, PROJ_TILE), lambda i, j: (0, j),
                               pipeline_mode=pl.Buffered(1 if PROJ_TILE == MAIN_COLS else 2))],
        out_specs=[pl.BlockSpec((tm, PROJ_TILE), lambda i, j: (i, j)),
                   pl.BlockSpec((tm, D_MODEL), row),
                   pl.BlockSpec((tm, LORA_DECAY_PAD), row),
                   pl.BlockSpec((tm, LORA_ICLR_PAD), row),
                   pl.BlockSpec((tm, LORA_GATE_PAD), row)],
        out_shape=[jax.ShapeDtypeStruct((n_tok, MAIN_COLS), BF16),
                   jax.ShapeDtypeStruct((n_tok, D_MODEL), BF16),
                   jax.ShapeDtypeStruct((n_tok, LORA_DECAY_PAD), F32),
                   jax.ShapeDtypeStruct((n_tok, LORA_ICLR_PAD), F32),
                   jax.ShapeDtypeStruct((n_tok, LORA_GATE_PAD), F32)],
        scratch_shapes=[pltpu.VMEM((1, D_MODEL), F32)],
        compiler_params=_params("arbitrary", "arbitrary"),
        name="in_proj",
    )(x2, sc, sh, g_pre, mu_lora, w1p, a1p, g1p, w_main)


def _attn_proj_kernel(h_ref, w_ref, o_ref, *scratch, dilation):
    acc = jnp.dot(h_ref[...], w_ref[...], preferred_element_type=F32)
    if dilation == 1:
        o_ref[0, 0] = acc.astype(BF16)
    else:
        scr, = scratch
        rows = acc.shape[0] // dilation
        for cg in range(scr.shape[0]):
            scr[cg] = acc[:, LANES * cg:LANES * (cg + 1)]
        for r in range(dilation):
            for cg in range(scr.shape[0]):
                o_ref[0, r, :, LANES * cg:LANES * (cg + 1)] = (
                    scr[cg, pl.ds(r, rows, stride=dilation), :].astype(BF16))


def _attn_proj(h, w_qkv, dilation, bsz, seq, tm):
    tpb = seq // tm
    width = 3 * ATTN_GROUP_WIDTH
    return pl.pallas_call(
        functools.partial(_attn_proj_kernel, dilation=dilation),
        grid=(bsz * tpb,),
        in_specs=[pl.BlockSpec((tm, D_MODEL), lambda i: (i, 0)),
                  pl.BlockSpec((D_MODEL, width), lambda i: (0, 0))],
        out_specs=pl.BlockSpec((1, dilation, tm // dilation, width), lambda i: (i // tpb, 0, i % tpb, 0)),
        out_shape=jax.ShapeDtypeStruct((bsz, dilation, seq // dilation, width), BF16),
        scratch_shapes=[] if dilation == 1 else [pltpu.VMEM((width // LANES, tm, LANES), F32)],
        compiler_params=_params("arbitrary"),
        name=f"attn_proj_d{dilation}",
    )(h, w_qkv)


def _prep_kernel(pr_ref, pk_ref, pv_ref, lw_ref, la_ref, lg_ref, mu_ref, w0_ref, a0_ref, kk_ref,
                 ka_ref, rk_ref, w2_ref, a2_ref, g2_ref, e_ref,
                 r_o, lw_o, k_o, v_o, kk_o, kka_o, g_o, bonus_o, last_scr, *, tiles_per_batch):
    i = pl.program_id(0)
    tm = pr_ref.shape[0]

    @pl.when(i % tiles_per_batch == 0)
    def _():
        last_scr[...] = jnp.zeros_like(last_scr)

    def mix(p_ref, idx):
        p = p_ref[...].astype(F32)
        ps = _shift_rows(p, last_scr[idx:idx + 1, :])
        last_scr[idx:idx + 1, :] = p[tm - 1:tm, :]
        return p + (ps - p) * mu_ref[idx:idx + 1, :]

    r = mix(pr_ref, 0)
    k = mix(pk_ref, 1)
    v = mix(pv_ref, 2)
    e = e_ref[...]

    z = w0_ref[...] + _dot(jnp.tanh(lw_ref[...]), w2_ref[...])
    lw_o[...] = -math.exp(-0.5) * jax.nn.sigmoid(z)
    a = jax.nn.sigmoid(a0_ref[...] + _dot(la_ref[...], a2_ref[...]))
    g_o[...] = _dot(jax.nn.sigmoid(lg_ref[...]), g2_ref[...]).astype(BF16)

    kk = k * kk_ref[...]
    kk = kk * lax.rsqrt(jnp.maximum(_seg_sum(kk * kk, e), 1e-24))
    k2 = k * (1.0 + (a - 1.0) * ka_ref[...])
    r_o[...] = r.astype(BF16)
    k_o[...] = k2.astype(BF16)
    v_o[...] = v.astype(BF16)
    kk_o[...] = kk.astype(BF16)
    kka_o[...] = (kk * a).astype(BF16)
    bonus_o[...] = (_seg_sum(r * k2 * rk_ref[...], e) * v).astype(BF16)


def _prep(proj, lw, la, lg, mu_rkv, w0, a0, k_k, k_a, r_k, w2p, a2p, g2p, e_seg, seq, tm):
    n_tok = proj.shape[0]
    tpb = seq // tm
    row = lambda i: (i, 0)
    const = lambda i: (0, 0)
    vec = pl.BlockSpec((1, D_MODEL), const)
    tok = pl.BlockSpec((tm, D_MODEL), row)
    return pl.pallas_call(
        functools.partial(_prep_kernel, tiles_per_batch=tpb),
        grid=(n_tok // tm,),
        in_specs=[pl.BlockSpec((tm, D_MODEL), lambda i: (i, COL_R // D_MODEL)),
                  pl.BlockSpec((tm, D_MODEL), lambda i: (i, COL_K // D_MODEL)),
                  pl.BlockSpec((tm, D_MODEL), lambda i: (i, COL_V // D_MODEL)),
                  pl.BlockSpec((tm, LORA_DECAY_PAD), row),
                  pl.BlockSpec((tm, LORA_ICLR_PAD), row),
                  pl.BlockSpec((tm, LORA_GATE_PAD), row),
                  pl.BlockSpec((3, D_MODEL), const),
                  vec, vec, vec, vec, vec,
                  pl.BlockSpec((LORA_DECAY_PAD, D_MODEL), const),
                  pl.BlockSpec((LORA_ICLR_PAD, D_MODEL), const),
                  pl.BlockSpec((LORA_GATE_PAD, D_MODEL), const),
                  pl.BlockSpec((SEG_TILE, SEG_TILE), const)],
        out_specs=[tok] * 8,
        out_shape=[jax.ShapeDtypeStruct((n_tok, D_MODEL), F32 if name == "lw" else BF16)
                   for name in ("r", "lw", "k", "v", "kk", "kka", "g", "bonus")],
        scratch_shapes=[pltpu.VMEM((3, D_MODEL), F32)],
        compiler_params=_params("arbitrary"),
        name="rwkv_prep",
    )(proj, proj, proj, lw, la, lg, mu_rkv, w0, a0, k_k, k_a, r_k, w2p, a2p, g2p, e_seg)


def _chunk_kernel(r_ref, lw_ref, k_ref, v_ref, kk_ref, kka_ref, y_ref, s_scr):
    @pl.when(pl.program_id(1) == 0)
    def _():
        s_scr[...] = jnp.zeros_like(s_scr)

    c = CHUNK
    rows = lw_ref.shape[1]
    n_chunks = rows // c
    lw = lw_ref[0]
    ti = lax.broadcasted_iota(jnp.int32, (rows, rows), 0)
    tj = lax.broadcasted_iota(jnp.int32, (rows, rows), 1)
    tril = ((ti >= tj) & (ti // c == tj // c)).astype(F32)
    cum = jnp.dot(tril, lw, preferred_element_type=F32, precision=HIGHEST)
    cum_prev = cum - lw
    cum_end = jnp.concatenate(
        [jnp.broadcast_to(cum[c * (ck + 1) - 1:c * (ck + 1), :], (c, D_MODEL)) for ck in range(n_chunks)], axis=0)
    kk = kk_ref[0].astype(F32)
    kka = kka_ref[0].astype(F32)
    k = k_ref[0].astype(F32)
    e_neg = jnp.exp(-cum)
    e_end = jnp.exp(cum_end - cum)
    at_all = -kk * jnp.exp(cum_prev)
    rt_all = r_ref[0].astype(F32) * jnp.exp(cum)
    bt_all = kka * e_neg
    kt_all = k * e_neg
    bh_all = kka * e_end
    kh_all = k * e_end
    p_end = jnp.exp(cum_end)
    v_all = v_ref[0].astype(F32)

    ri = lax.broadcasted_iota(jnp.int32, (PAIR, PAIR), 0)
    ci = lax.broadcasted_iota(jnp.int32, (PAIR, PAIR), 1)
    same = (ri // RWKV_HEAD) == (ci // RWKV_HEAD)
    same2 = jnp.concatenate([same, same], axis=1)
    eye = ri == ci
    tt = lax.broadcasted_iota(jnp.int32, (c, PAIR), 0)
    ss = lax.broadcasted_iota(jnp.int32, (c, PAIR), 1) % c
    m_strict = tt > ss
    m_incl = tt >= ss
    eye_rows = (tt == ss).astype(F32)
    zeros_c = jnp.zeros((c, PAIR), F32)

    def bd(t):
        mask = same if t.shape[1] == PAIR else same2
        return jnp.where(mask, jnp.concatenate([t, t], axis=0), 0.0).astype(BF16)

    items = [(slice(c * ck, c * (ck + 1)), slice(PAIR * p, PAIR * (p + 1)))
             for ck in range(n_chunks) for p in range(N_PAIRS)]
    pairs = range(len(items))
    at = [at_all[rs, sl] for rs, sl in items]
    rt = [rt_all[rs, sl] for rs, sl in items]
    v_bd = [bd(v_all[rs, sl]) for rs, sl in items]

    quad = [_dot_nt(jnp.concatenate([at[p], rt[p]], axis=0),
                    jnp.concatenate([bd(bt_all[items[p]]), bd(kt_all[items[p]])], axis=0))
            for p in pairs]
    a_ab = [jnp.where(m_strict, q[:c, :PAIR], 0.0) for q in quad]
    a_ak = [jnp.where(m_strict, q[:c, PAIR:], 0.0) for q in quad]
    a_rb = [jnp.where(m_incl, q[c:, :PAIR], 0.0) for q in quad]
    a_rk = [jnp.where(m_incl, q[c:, PAIR:], 0.0) for q in quad]
    av = [_dot(jnp.concatenate([a_ak[p], a_rk[p]], axis=0), v_bd[p]) for p in pairs]
    x0 = [jnp.concatenate([at[p], av[p][:c]], axis=1) for p in pairs]

    t_inv = [eye_rows + a_ab[p] for p in pairs]
    pw = [_dot(a_ab[p], bd(a_ab[p])) for p in pairs]
    t_inv = [t_inv[p] + _dot(pw[p], bd(t_inv[p])) for p in pairs]
    pw = [_dot(pw[p], bd(pw[p])) for p in pairs]
    for _ in range(int(math.log2(c)) - 3):
        both = [_dot(jnp.concatenate([t_inv[p], pw[p]], axis=0), bd(pw[p])) for p in pairs]
        t_inv = [t_inv[p] + both[p][:c] for p in pairs]
        pw = [both[p][c:] for p in pairs]
    t_inv = [t_inv[p] + _dot(t_inv[p], bd(pw[p])) for p in pairs]
    x = [_dot(t_inv[p], bd(x0[p])) for p in pairs]

    top = [_dot(a_rb[p], bd(x[p])) for p in pairs]
    bot = [_dot_tn(jnp.concatenate([bh_all[items[p]], kh_all[items[p]]], axis=0),
                   jnp.concatenate([x[p], jnp.concatenate([zeros_c, v_all[items[p]]], axis=1)], axis=0))
           for p in pairs]

    state = [s_scr[hp] for hp in range(N_PAIRS)]
    for p in pairs:
        rs, sl = items[p]
        hp = p % N_PAIRS
        p_end_row = p_end[rs, sl][c - 1:c, :]
        m_mat = jnp.where(same, bot[p][:, :PAIR], 0.0) + jnp.where(
            eye, jnp.broadcast_to(p_end_row, (PAIR, PAIR)), 0.0)
        lhs = jnp.concatenate([rt[p] + top[p][:, :PAIR], m_mat], axis=0)
        fin = _dot(lhs, state[hp])
        y_ref[0, rs, sl] = fin[:c] + top[p][:, PAIR:] + av[p][c:]
        state[hp] = fin[c:] + jnp.where(same, bot[p][:, PAIR:], 0.0)
    for hp in range(N_PAIRS):
        s_scr[hp] = state[hp]


def _chunk(r, lw, k, v, kk, kka):
    bsz, seq, _ = r.shape
    spec = pl.BlockSpec((1, CHUNKS_PER_STEP * CHUNK, D_MODEL), lambda b, c: (b, c, 0))
    return pl.pallas_call(
        _chunk_kernel,
        grid=(bsz, seq // (CHUNKS_PER_STEP * CHUNK)),
        in_specs=[spec] * 6,
        out_specs=spec,
        out_shape=jax.ShapeDtypeStruct((bsz, seq, D_MODEL), F32),
        scratch_shapes=[pltpu.VMEM((N_PAIRS, PAIR, PAIR), F32)],
        compiler_params=_params("arbitrary", "arbitrary"),
        name="rwkv_chunk",
    )(r, lw, k, v, kk, kka)


def _attn_kernel(cur_ref, prev_ref, o_ref, l_ref, *, dilation, slopes):
    n = pl.program_id(2)
    blk = ATTN_BLK
    gw = ATTN_GROUP_WIDTH
    ii = lax.broadcasted_iota(jnp.int32, (blk, 2 * blk), 0)
    jj = lax.broadcasted_iota(jnp.int32, (blk, 2 * blk), 1)
    delta = ii - jj + blk
    in_window = (delta >= 0) & (delta <= blk)
    dist = delta.astype(F32) * float(dilation)
    lane_head = lax.broadcasted_iota(jnp.int32, (blk, PAIR), 1) // ATTN_HEAD
    lane = lax.broadcasted_iota(jnp.int32, (blk, LANES), 1)
    neg_inf = float("-inf")
    pairs = range(gw // PAIR)
    heads = [(pp, hh) for pp in pairs for hh in range(2)]
    sls = [slice(PAIR * pp, PAIR * (pp + 1)) for pp in pairs]

    for sb in range(2):
        rows = slice(blk * sb, blk * (sb + 1))
        if sb == 0:
            valid = in_window & ((n > 0) | (jj >= blk))
            kcat = [jnp.concatenate([prev_ref[0, 0, :, sl], cur_ref[0, 0, rows, sl]], axis=0) for sl in sls]
            vcat = [jnp.concatenate([prev_ref[0, 0, :, gw + PAIR * pp:gw + PAIR * (pp + 1)],
                                     cur_ref[0, 0, rows, gw + PAIR * pp:gw + PAIR * (pp + 1)]], axis=0)
                    for pp in pairs]
        else:
            valid = in_window
            kcat = [cur_ref[0, 0, :, sl] for sl in sls]
            vcat = [cur_ref[0, 0, :, gw + PAIR * pp:gw + PAIR * (pp + 1)] for pp in pairs]
        q2 = [cur_ref[0, 0, rows, 2 * gw + PAIR * pp:2 * gw + PAIR * (pp + 1)] * (ATTN_HEAD ** -0.5)
              for pp in pairs]
        s = [jnp.where(valid,
                       _dot_nt(jnp.where(lane_head == hh, q2[pp], jnp.zeros_like(q2[pp])), kcat[pp])
                       - float(slopes[2 * pp + hh]) * dist, neg_inf) for pp, hh in heads]
        m = [jnp.max(t, axis=-1, keepdims=True) for t in s]
        p = [jnp.exp(s[i] - m[i]) for i in range(len(heads))]
        den = [jnp.sum(t, axis=-1, keepdims=True) for t in p]
        o = [_dot(p[i], vcat[pp]) * (1.0 / den[i]) for i, (pp, _) in enumerate(heads)]
        for pp in pairs:
            o_ref[0, 0, rows, sls[pp]] = jnp.where(lane_head == 0, o[2 * pp], o[2 * pp + 1]).astype(BF16)
        lse = jnp.zeros((blk, LANES), F32)
        for i in range(len(heads)):
            lse = jnp.where(lane == i, m[i] + jnp.log(den[i]), lse)
        l_ref[0, 0, rows, :] = lse


def _attn_group(kvq, dilation, slopes):
    bsz, _, sub, _ = kvq.shape
    gw = ATTN_GROUP_WIDTH
    step = 2 * ATTN_BLK
    return pl.pallas_call(
        functools.partial(_attn_kernel, dilation=dilation, slopes=tuple(slopes)),
        grid=(bsz, dilation, sub // step),
        in_specs=[pl.BlockSpec((1, 1, step, 3 * gw), lambda b, r, n: (b, r, n, 0)),
                  pl.BlockSpec((1, 1, ATTN_BLK, 2 * gw), lambda b, r, n: (b, r, jnp.maximum(2 * n - 1, 0), 0))],
        out_specs=[pl.BlockSpec((1, 1, step, gw), lambda b, r, n: (b, r, n, 0)),
                   pl.BlockSpec((1, 1, step, LANES), lambda b, r, n: (b, r, n, 0))],
        out_shape=[jax.ShapeDtypeStruct((bsz, dilation, sub, gw), BF16),
                   jax.ShapeDtypeStruct((bsz, dilation, sub, LANES), F32)],
        compiler_params=_params("arbitrary", "arbitrary", "arbitrary"),
        name=f"dilated_attn_d{dilation}",
    )(kvq, kvq)


def _mix_kernel(y_ref, bonus_ref, g_ref, za_ref, zb_ref, x_ref, o0_ref, o1_ref, o2_ref,
                l0_ref, l1_ref, l2_ref, gt_ref, lnw_ref, lnb_ref, gpost_ref,
                wor_ref, woa_ref, wout_ref, e_ref, spread_ref, out_ref, *scratch):
    e = e_ref[...]
    inv_n = 1.0 / RWKV_HEAD
    y = y_ref[...]
    mean = _seg_sum(y, e, pieces=2) * inv_n
    yc = y - mean
    var = _seg_sum(yc * yc, e) * inv_n
    yn = yc * lax.rsqrt(var + GN_EPS) * lnw_ref[...] + lnb_ref[...]
    y_rwkv = _dot((yn + bonus_ref[...].astype(F32)) * g_ref[...].astype(F32), wor_ref[...])

    def token_order(ref, scr):
        dilation, rows = ref.shape[1], ref.shape[2]
        if dilation == 1:
            return ref[0, 0].astype(F32)
        for r in range(dilation):
            for cg in range(ref.shape[3] // LANES):
                scr[cg, pl.ds(r, rows, stride=dilation), :] = (
                    ref[0, r, :, LANES * cg:LANES * (cg + 1)].astype(F32))
        return jnp.concatenate([scr[cg] for cg in range(ref.shape[3] // LANES)], axis=1)

    def spread(w):
        return jnp.dot(w.astype(BF16), spread_ref[...], preferred_element_type=F32)

    o0, l0 = token_order(o0_ref, None), token_order(l0_ref, None)
    o1, l1 = token_order(o1_ref, scratch[0]), token_order(l1_ref, scratch[1])
    o2, l2 = token_order(o2_ref, scratch[2]), token_order(l2_ref, scratch[3])
    mx = jnp.maximum(jnp.maximum(l0, l1), l2)
    e0, e1, e2 = jnp.exp(l0 - mx), jnp.exp(l1 - mx), jnp.exp(l2 - mx)
    inv = 1.0 / (e0 + e1 + e2)
    o = spread(e0 * inv) * o0 + spread(e1 * inv) * o1 + spread(e2 * inv) * o2
    y_attn = _dot(o, woa_ref[...])

    mixed = _dot(jax.nn.sigmoid(za_ref[...].astype(F32)) * y_rwkv
                 + jax.nn.sigmoid(zb_ref[...].astype(F32)) * y_attn,
                 wout_ref[...])
    out_ref[...] = x_ref[...] + gt_ref[0] * (_rms(mixed) * gpost_ref[...])


def _mix(y, bonus, g, proj, x2, o_l, gt, ln_w, ln_b, g_post, w_or, w_oa, w_out, e_seg, seq, tm):
    n_tok = x2.shape[0]
    tpb = seq // tm
    row = lambda i: (i, 0)
    const = lambda i: (0, 0)
    tok = pl.BlockSpec((tm, D_MODEL), row)
    vec = pl.BlockSpec((1, D_MODEL), const)

    def att(dilation, width):
        return pl.BlockSpec((1, dilation, tm // dilation, width), lambda i: (i // tpb, 0, i % tpb, 0))

    (o0, l0), (o1, l1), (o2, l2) = o_l
    d0, d1, d2 = [d for _, d in ATTN_GROUPS]
    assert d0 == 1
    gw = ATTN_GROUP_WIDTH
    head_of_lane = jnp.arange(gw) // ATTN_HEAD
    spread = (jnp.arange(LANES)[:, None] == head_of_lane[None, :]).astype(BF16)
    return pl.pallas_call(
        _mix_kernel,
        grid=(n_tok // tm,),
        in_specs=[tok, tok, tok,
                  pl.BlockSpec((tm, D_MODEL), lambda i: (i, COL_ZA // D_MODEL)),
                  pl.BlockSpec((tm, D_MODEL), lambda i: (i, COL_ZB // D_MODEL)),
                  tok, att(d0, gw), att(d1, gw), att(d2, gw), att(d0, LANES), att(d1, LANES), att(d2, LANES),
                  pl.BlockSpec((1, 1, D_MODEL), lambda i: (i // tpb, 0, 0)),
                  vec, vec, vec,
                  pl.BlockSpec((D_MODEL, D_MODEL), const, pipeline_mode=pl.Buffered(1)),
                  pl.BlockSpec((gw, D_MODEL), const, pipeline_mode=pl.Buffered(1)),
                  pl.BlockSpec((D_MODEL, D_MODEL), const, pipeline_mode=pl.Buffered(1)),
                  pl.BlockSpec((SEG_TILE, SEG_TILE), const),
                  pl.BlockSpec((LANES, gw), const)],
        out_specs=tok,
        out_shape=jax.ShapeDtypeStruct((n_tok, D_MODEL), F32),
        scratch_shapes=[pltpu.VMEM((gw // LANES, tm, LANES), F32), pltpu.VMEM((1, tm, LANES), F32)] * 2,
        compiler_params=_params("arbitrary"),
        name="mix_out",
    )(y, bonus, g, proj, proj, x2, o0, o1, o2, l0, l1, l2, gt, ln_w, ln_b, g_post,
      w_or, w_oa, w_out, e_seg, spread)


def _ffn_kernel(x_ref, sc_ref, sh_ref, gt_ref, gpre_ref, gpost_ref, wg_ref, wu_ref, wo_ref, out_ref):
    x = x_ref[...]
    h = (_rms(x) * gpre_ref[...] * (1.0 + sc_ref[0]) + sh_ref[0]).astype(BF16)
    acts = []
    for t in range(FFN_HIDDEN // SEG_TILE):
        sl = slice(SEG_TILE * t, SEG_TILE * (t + 1))
        u_gate = jnp.dot(h, wg_ref[:, sl], preferred_element_type=F32)
        u_up = jnp.dot(h, wu_ref[:, sl], preferred_element_type=F32)
        acts.append((u_gate * jax.nn.sigmoid(u_gate) * u_up).astype(BF16))
    y = jnp.dot(jnp.concatenate(acts, axis=1), wo_ref[...], preferred_element_type=F32)
    out_ref[...] = x + gt_ref[0] * (_rms(y) * gpost_ref[...])


def _ffn(x1, sc, sh, gt, g_pre, g_post, w_in, w_out, seq, tm):
    n_tok = x1.shape[0]
    tpb = seq // tm
    row = lambda i: (i, 0)
    const = lambda i: (0, 0)
    modrow = lambda i: (i // tpb, 0, 0)
    tok = pl.BlockSpec((tm, D_MODEL), row)
    mod = pl.BlockSpec((1, 1, D_MODEL), modrow)
    vec = pl.BlockSpec((1, D_MODEL), const)
    resident = pl.Buffered(1)
    return pl.pallas_call(
        _ffn_kernel,
        grid=(n_tok // tm,),
        in_specs=[tok, mod, mod, mod, vec, vec,
                  pl.BlockSpec((D_MODEL, FFN_HIDDEN), lambda i: (0, 0), pipeline_mode=resident),
                  pl.BlockSpec((D_MODEL, FFN_HIDDEN), lambda i: (0, 1), pipeline_mode=resident),
                  pl.BlockSpec((FFN_HIDDEN, D_MODEL), const, pipeline_mode=resident)],
        out_specs=tok,
        out_shape=jax.ShapeDtypeStruct((n_tok, D_MODEL), F32),
        compiler_params=_params("arbitrary"),
        name="swiglu_ffn",
    )(x1, sc, sh, gt, g_pre, g_post, w_in, w_in, w_out)


def _pad_cols(w, n):
    return jnp.pad(w, ((0, 0), (0, n - w.shape[1]))).astype(BF16)


def _pad_rows(w, n):
    return jnp.pad(w, ((0, n - w.shape[0]), (0, 0))).astype(BF16)


def kernel(x, c, w_mod, b_mod, g_pre_mix, g_post_mix, g_pre_ffn, g_post_ffn, w_in, mu_rkv, mu_lora,
           w0, w1, w2, a0, a1, a2, g1, g2, k_k, k_a, r_k, ln_x_w, ln_x_b, w_o_rwkv, w_o_attn, w_out,
           w_ffn_in, w_ffn_out):
    bsz, seq, _ = x.shape
    depth = w_mod.shape[0]
    assert seq % (2 * ATTN_BLK * ATTN_GROUPS[-1][1]) == 0
    n_tok = bsz * seq
    tm_proj = min(512, seq)
    tm_aproj = min(512, seq)
    tm_prep = min(512, seq)
    tm_mix = min(512, seq)
    tm_ffn = min(512, seq)

    slopes = _alibi_slopes(len(ATTN_GROUPS) * ATTN_HEADS_PER_GROUP)
    seg = jnp.arange(SEG_TILE) // RWKV_HEAD
    e_seg = (seg[:, None] == seg[None, :]).astype(BF16)
    row = lambda t: t.reshape(1, -1)

    x2 = x.reshape(n_tok, D_MODEL)
    for l in range(depth):
        mod = _mod(c, w_mod[l], b_mod[l])
        sh_m, sc_m, gt_m, sh_f, sc_f, gt_f = [mod[:, i * D_MODEL:(i + 1) * D_MODEL].reshape(bsz, 1, D_MODEL)
                                               for i in range(6)]
        wl = w_in[l]
        rkv_cols = 3 * D_MODEL
        att_cols = 3 * ATTN_WIDTH
        w_main = jnp.concatenate([wl[:, :rkv_cols], wl[:, rkv_cols + att_cols:]], axis=1).astype(BF16)

        proj, h, lw, la, lg = _inproj(
            x2, sc_m, sh_m, row(g_pre_mix[l]), mu_lora[l],
            _pad_cols(w1[l], LORA_DECAY_PAD), _pad_cols(a1[l], LORA_ICLR_PAD),
            _pad_cols(g1[l], LORA_GATE_PAD), w_main, seq, tm_proj)

        r, lwd, k2, v, kk, kka, g, bonus = _prep(
            proj, lw, la, lg, mu_rkv[l], row(w0[l]), row(a0[l]), row(k_k[l]), row(k_a[l]), row(r_k[l]),
            _pad_rows(w2[l], LORA_DECAY_PAD), _pad_rows(a2[l], LORA_ICLR_PAD),
            _pad_rows(g2[l], LORA_GATE_PAD), e_seg, seq, tm_prep)

        b3 = lambda t: t.reshape(bsz, seq, D_MODEL)
        y = _chunk(b3(r), b3(lwd), b3(k2), b3(v), b3(kk), b3(kka)).reshape(n_tok, D_MODEL)

        o_l = []
        for gi, (window, dilation) in enumerate(ATTN_GROUPS):
            assert window // dilation == ATTN_BLK
            gsl = lambda which: wl[:, rkv_cols + which * ATTN_WIDTH + gi * ATTN_GROUP_WIDTH:
                                   rkv_cols + which * ATTN_WIDTH + (gi + 1) * ATTN_GROUP_WIDTH]
            w_qkv = jnp.concatenate([gsl(1), gsl(2), gsl(0)], axis=1).astype(BF16)
            qkv = _attn_proj(h, w_qkv, dilation, bsz, seq, tm_aproj)
            o_l.append(_attn_group(
                qkv, dilation, slopes[gi * ATTN_HEADS_PER_GROUP:(gi + 1) * ATTN_HEADS_PER_GROUP]))

        x2 = _mix(y, bonus, g, proj, x2, o_l, gt_m, row(ln_x_w[l]), row(ln_x_b[l]), row(g_post_mix[l]),
                  w_o_rwkv[l].astype(BF16), w_o_attn[l].astype(BF16), w_out[l].astype(BF16), e_seg,
                  seq, tm_mix)

        x2 = _ffn(x2, sc_f, sh_f, gt_f, row(g_pre_ffn[l]), row(g_post_ffn[l]),
                  w_ffn_in[l].astype(BF16), w_ffn_out[l].astype(BF16), seq, tm_ffn)
    return x2.reshape(bsz, seq, D_MODEL)
```

```python
import functools
import math

import jax
import jax.numpy as jnp
from jax import lax
from jax.experimental import pallas as pl
from jax.experimental.pallas import tpu as pltpu

F32 = jnp.float32
BF16 = jnp.bfloat16
HIGHEST = lax.Precision.HIGHEST

LANES = 128
SEG_TILE = 256
D_MODEL = 1024
RWKV_HEAD = 64
GN_EPS = 64e-5
RMS_EPS = 1e-6
ATTN_HEAD = 64
ATTN_HEADS_PER_GROUP = 8
ATTN_GROUPS = ((128, 1), (512, 4), (2048, 16))
ATTN_GROUP_WIDTH = ATTN_HEADS_PER_GROUP * ATTN_HEAD
ATTN_WIDTH = len(ATTN_GROUPS) * ATTN_GROUP_WIDTH
ATTN_BLK = 128
FFN_HIDDEN = 2816
LORA_DECAY_PAD = 128
LORA_ICLR_PAD = 128
LORA_GATE_PAD = 256
CHUNK = 64
CHUNKS_PER_STEP = 4
PAIR = 2 * RWKV_HEAD
N_PAIRS = D_MODEL // PAIR

COL_R, COL_K, COL_V, COL_ZA, COL_ZB = 0, 1024, 2048, 3072, 4096
MAIN_COLS = 5120
PROJ_TILE = 5120

VMEM_LIMIT = 48 * 1024 * 1024


def _alibi_slopes(n):
    def pow2(m):
        start = 2.0 ** (-8.0 / m)
        return [start ** (i + 1) for i in range(m)]
    if math.log2(n).is_integer():
        s = pow2(n)
    else:
        p = 2 ** int(math.floor(math.log2(n)))
        s = pow2(p) + pow2(2 * p)[0::2][: n - p]
    return sorted(s, reverse=True)


def _dot(a, b):
    return jnp.dot(a.astype(BF16), b.astype(BF16), preferred_element_type=F32)


def _dot_nt(a, b):
    return lax.dot_general(a.astype(BF16), b.astype(BF16), (((1,), (1,)), ((), ())),
                           preferred_element_type=F32)


def _dot_tn(a, b):
    return lax.dot_general(a.astype(BF16), b.astype(BF16), (((0,), (0,)), ((), ())),
                           preferred_element_type=F32)


def _seg_sum(x, e, pieces=1):
    parts = [x.astype(BF16)]
    if pieces == 2:
        parts.append((x - parts[0].astype(F32)).astype(BF16))
    outs = []
    for t in range(x.shape[1] // SEG_TILE):
        sl = slice(SEG_TILE * t, SEG_TILE * (t + 1))
        outs.append(sum(jnp.dot(part[:, sl], e, preferred_element_type=F32) for part in parts))
    return jnp.concatenate(outs, axis=1)


def _shift_rows(t, prev_row):
    rows = lax.broadcasted_iota(jnp.int32, t.shape, 0)
    return jnp.where(rows == 0, prev_row, pltpu.roll(t, 1, 0))


def _rms(t):
    return t * lax.rsqrt(jnp.mean(t * t, axis=-1, keepdims=True) + RMS_EPS)


def _params(*semantics):
    return pltpu.CompilerParams(dimension_semantics=semantics, vmem_limit_bytes=VMEM_LIMIT)


def _mod_kernel(c_ref, w_ref, b_ref, o_ref):
    o_ref[...] = jnp.dot(c_ref[...], w_ref[...], preferred_element_type=F32,
                         precision=HIGHEST) + b_ref[...]


def _mod(c, w_mod, b_mod):
    bsz = c.shape[0]
    n_out = w_mod.shape[1]
    return pl.pallas_call(
        _mod_kernel,
        grid=(n_out // D_MODEL,),
        in_specs=[pl.BlockSpec((bsz, D_MODEL), lambda j: (0, 0)),
                  pl.BlockSpec((D_MODEL, D_MODEL), lambda j: (0, j)),
                  pl.BlockSpec((1, D_MODEL), lambda j: (0, j))],
        out_specs=pl.BlockSpec((bsz, D_MODEL), lambda j: (0, j)),
        out_shape=jax.ShapeDtypeStruct((bsz, n_out), F32),
        compiler_params=_params("arbitrary"),
        name="adaln_mod",
    )(c, w_mod, b_mod.reshape(1, n_out))


def _inproj_kernel(x_ref, sc_ref, sh_ref, g_ref, mu_ref, w1_ref, a1_ref, g1_ref, win_ref,
                   p_ref, h_ref, lw_ref, la_ref, lg_ref, last_scr, *, tiles_per_batch):
    i = pl.program_id(0)
    j = pl.program_id(1)
    tm = x_ref.shape[0]

    @pl.when(j == 0)
    def _():
        @pl.when(i % tiles_per_batch == 0)
        def _():
            last_scr[...] = jnp.zeros_like(last_scr)

        h = _rms(x_ref[...]) * g_ref[...] * (1.0 + sc_ref[0]) + sh_ref[0]
        hs = _shift_rows(h, last_scr[...])
        last_scr[...] = h[tm - 1:tm, :]
        dh = hs - h
        lw_ref[...] = _dot(h + dh * mu_ref[0:1, :], w1_ref[...])
        la_ref[...] = _dot(h + dh * mu_ref[1:2, :], a1_ref[...])
        lg_ref[...] = _dot(h + dh * mu_ref[2:3, :], g1_ref[...])
        h_ref[...] = h.astype(BF16)

    p_ref[...] = jnp.dot(h_ref[...], win_ref[...], preferred_element_type=F32).astype(BF16)


def _inproj(x2, sc, sh, g_pre, mu_lora, w1p, a1p, g1p, w_main, seq, tm):
    n_tok = x2.shape[0]
    tpb = seq // tm
    row = lambda i, j: (i, 0)
    const = lambda i, j: (0, 0)
    modrow = lambda i, j: (i // tpb, 0, 0)
    return pl.pallas_call(
        functools.partial(_inproj_kernel, tiles_per_batch=tpb),
        grid=(n_tok // tm, MAIN_COLS // PROJ_TILE),
        in_specs=[pl.BlockSpec((tm, D_MODEL), row),
                  pl.BlockSpec((1, 1, D_MODEL), modrow),
                  pl.BlockSpec((1, 1, D_MODEL), modrow),
                  pl.BlockSpec((1, D_MODEL), const),
                  pl.BlockSpec((3, D_MODEL), const),
                  pl.BlockSpec((D_MODEL, LORA_DECAY_PAD), const),
                  pl.BlockSpec((D_MODEL, LORA_ICLR_PAD), const),
                  pl.BlockSpec((D_MODEL, LORA_GATE_PAD), const),
                  pl.BlockSpec((D_MODEL, PROJ_TILE), lambda i, j: (0, j),
                               pipeline_mode=pl.Buffered(1 if PROJ_TILE == MAIN_COLS else 2))],
        out_specs=[pl.BlockSpec((tm, PROJ_TILE), lambda i, j: (i, j)),
                   pl.BlockSpec((tm, D_MODEL), row),
                   pl.BlockSpec((tm, LORA_DECAY_PAD), row),
                   pl.BlockSpec((tm, LORA_ICLR_PAD), row),
                   pl.BlockSpec((tm, LORA_GATE_PAD), row)],
        out_shape=[jax.ShapeDtypeStruct((n_tok, MAIN_COLS), BF16),
                   jax.ShapeDtypeStruct((n_tok, D_MODEL), BF16),
                   jax.ShapeDtypeStruct((n_tok, LORA_DECAY_PAD), F32),
                   jax.ShapeDtypeStruct((n_tok, LORA_ICLR_PAD), F32),
                   jax.ShapeDtypeStruct((n_tok, LORA_GATE_PAD), F32)],
        scratch_shapes=[pltpu.VMEM((1, D_MODEL), F32)],
        compiler_params=_params("arbitrary", "arbitrary"),
        name="in_proj",
    )(x2, sc, sh, g_pre, mu_lora, w1p, a1p, g1p, w_main)


def _attn_proj_kernel(h_ref, w_ref, *refs):
    n_groups = len(ATTN_GROUPS)
    out_refs, scratch = refs[:n_groups], list(refs[n_groups:])
    h = h_ref[...]
    for gi, (_, dilation) in enumerate(ATTN_GROUPS):
        o_ref = out_refs[gi]
        acc = jnp.dot(h, w_ref[gi], preferred_element_type=F32)
        if dilation == 1:
            o_ref[0, 0] = acc.astype(BF16)
            continue
        scr = scratch.pop(0)
        rows = acc.shape[0] // dilation
        for cg in range(scr.shape[0]):
            scr[cg] = acc[:, LANES * cg:LANES * (cg + 1)]
        for r in range(dilation):
            for cg in range(scr.shape[0]):
                o_ref[0, r, :, LANES * cg:LANES * (cg + 1)] = (
                    scr[cg, pl.ds(r, rows, stride=dilation), :].astype(BF16))


def _attn_proj(h, w_kvq, bsz, seq, tm):
    tpb = seq // tm
    width = 3 * ATTN_GROUP_WIDTH
    dilations = [d for _, d in ATTN_GROUPS]
    return pl.pallas_call(
        _attn_proj_kernel,
        grid=(bsz * tpb,),
        in_specs=[pl.BlockSpec((tm, D_MODEL), lambda i: (i, 0)),
                  pl.BlockSpec((len(dilations), D_MODEL, width), lambda i: (0, 0, 0),
                               pipeline_mode=pl.Buffered(1))],
        out_specs=[pl.BlockSpec((1, d, tm // d, width), lambda i: (i // tpb, 0, i % tpb, 0))
                   for d in dilations],
        out_shape=[jax.ShapeDtypeStruct((bsz, d, seq // d, width), BF16) for d in dilations],
        scratch_shapes=[pltpu.VMEM((width // LANES, tm, LANES), F32) for d in dilations if d != 1],
        compiler_params=_params("arbitrary"),
        name="attn_proj",
    )(h, w_kvq)


def _prep_kernel(pr_ref, pk_ref, pv_ref, lw_ref, la_ref, lg_ref, mu_ref, w0_ref, a0_ref, kk_ref,
                 ka_ref, rk_ref, w2_ref, a2_ref, g2_ref, e_ref,
                 r_o, lw_o, k_o, v_o, kk_o, kka_o, g_o, bonus_o, last_scr, *, tiles_per_batch):
    i = pl.program_id(0)
    tm = pr_ref.shape[0]

    @pl.when(i % tiles_per_batch == 0)
    def _():
        last_scr[...] = jnp.zeros_like(last_scr)

    def mix(p_ref, idx):
        p = p_ref[...].astype(F32)
        ps = _shift_rows(p, last_scr[idx:idx + 1, :])
        last_scr[idx:idx + 1, :] = p[tm - 1:tm, :]
        return p + (ps - p) * mu_ref[idx:idx + 1, :]

    r = mix(pr_ref, 0)
    k = mix(pk_ref, 1)
    v = mix(pv_ref, 2)
    e = e_ref[...]

    z = w0_ref[...] + _dot(jnp.tanh(lw_ref[...]), w2_ref[...])
    lw_o[...] = -math.exp(-0.5) * jax.nn.sigmoid(z)
    a = jax.nn.sigmoid(a0_ref[...] + _dot(la_ref[...], a2_ref[...]))
    g_o[...] = _dot(jax.nn.sigmoid(lg_ref[...]), g2_ref[...]).astype(BF16)

    kk = k * kk_ref[...]
    kk = kk * lax.rsqrt(jnp.maximum(_seg_sum(kk * kk, e), 1e-24))
    k2 = k * (1.0 + (a - 1.0) * ka_ref[...])
    r_o[...] = r.astype(BF16)
    k_o[...] = k2.astype(BF16)
    v_o[...] = v.astype(BF16)
    kk_o[...] = kk.astype(BF16)
    kka_o[...] = (kk * a).astype(BF16)
    bonus_o[...] = (_seg_sum(r * k2 * rk_ref[...], e) * v).astype(BF16)


def _prep(proj, lw, la, lg, mu_rkv, w0, a0, k_k, k_a, r_k, w2p, a2p, g2p, e_seg, seq, tm):
    n_tok = proj.shape[0]
    tpb = seq // tm
    row = lambda i: (i, 0)
    const = lambda i: (0, 0)
    vec = pl.BlockSpec((1, D_MODEL), const)
    tok = pl.BlockSpec((tm, D_MODEL), row)
    return pl.pallas_call(
        functools.partial(_prep_kernel, tiles_per_batch=tpb),
        grid=(n_tok // tm,),
        in_specs=[pl.BlockSpec((tm, D_MODEL), lambda i: (i, COL_R // D_MODEL)),
                  pl.BlockSpec((tm, D_MODEL), lambda i: (i, COL_K // D_MODEL)),
                  pl.BlockSpec((tm, D_MODEL), lambda i: (i, COL_V // D_MODEL)),
                  pl.BlockSpec((tm, LORA_DECAY_PAD), row),
                  pl.BlockSpec((tm, LORA_ICLR_PAD), row),
                  pl.BlockSpec((tm, LORA_GATE_PAD), row),
                  pl.BlockSpec((3, D_MODEL), const),
                  vec, vec, vec, vec, vec,
                  pl.BlockSpec((LORA_DECAY_PAD, D_MODEL), const),
                  pl.BlockSpec((LORA_ICLR_PAD, D_MODEL), const),
                  pl.BlockSpec((LORA_GATE_PAD, D_MODEL), const),
                  pl.BlockSpec((SEG_TILE, SEG_TILE), const)],
        out_specs=[tok] * 8,
        out_shape=[jax.ShapeDtypeStruct((n_tok, D_MODEL), F32 if name == "lw" else BF16)
                   for name in ("r", "lw", "k", "v", "kk", "kka", "g", "bonus")],
        scratch_shapes=[pltpu.VMEM((3, D_MODEL), F32)],
        compiler_params=_params("arbitrary"),
        name="rwkv_prep",
    )(proj, proj, proj, lw, la, lg, mu_rkv, w0, a0, k_k, k_a, r_k, w2p, a2p, g2p, e_seg)


def _chunk_kernel(r_ref, lw_ref, k_ref, v_ref, kk_ref, kka_ref, y_ref, s_scr):
    @pl.when(pl.program_id(1) == 0)
    def _():
        s_scr[...] = jnp.zeros_like(s_scr)

    c = CHUNK
    rows = lw_ref.shape[1]
    n_chunks = rows // c
    lw = lw_ref[0]
    ti = lax.broadcasted_iota(jnp.int32, (rows, rows), 0)
    tj = lax.broadcasted_iota(jnp.int32, (rows, rows), 1)
    tril = ((ti >= tj) & (ti // c == tj // c)).astype(F32)
    cum = jnp.dot(tril, lw, preferred_element_type=F32, precision=HIGHEST)
    cum_prev = cum - lw
    cum_end = jnp.concatenate(
        [jnp.broadcast_to(cum[c * (ck + 1) - 1:c * (ck + 1), :], (c, D_MODEL)) for ck in range(n_chunks)], axis=0)
    kk = kk_ref[0].astype(F32)
    kka = kka_ref[0].astype(F32)
    k = k_ref[0].astype(F32)
    e_neg = jnp.exp(-cum)
    e_end = jnp.exp(cum_end - cum)
    at_all = -kk * jnp.exp(cum_prev)
    rt_all = r_ref[0].astype(F32) * jnp.exp(cum)
    bt_all = kka * e_neg
    kt_all = k * e_neg
    bh_all = kka * e_end
    kh_all = k * e_end
    p_end = jnp.exp(cum_end)
    v_all = v_ref[0].astype(F32)

    ri = lax.broadcasted_iota(jnp.int32, (PAIR, PAIR), 0)
    ci = lax.broadcasted_iota(jnp.int32, (PAIR, PAIR), 1)
    same = (ri // RWKV_HEAD) == (ci // RWKV_HEAD)
    same2 = jnp.concatenate([same, same], axis=1)
    eye = ri == ci
    tt = lax.broadcasted_iota(jnp.int32, (c, PAIR), 0)
    ss = lax.broadcasted_iota(jnp.int32, (c, PAIR), 1) % c
    m_strict = tt > ss
    m_incl = tt >= ss
    eye_rows = (tt == ss).astype(F32)
    zeros_c = jnp.zeros((c, PAIR), F32)

    def bd(t):
        mask = same if t.shape[1] == PAIR else same2
        return jnp.where(mask, jnp.concatenate([t, t], axis=0), 0.0).astype(BF16)

    items = [(slice(c * ck, c * (ck + 1)), slice(PAIR * p, PAIR * (p + 1)))
             for ck in range(n_chunks) for p in range(N_PAIRS)]
    pairs = range(len(items))
    at = [at_all[rs, sl] for rs, sl in items]
    rt = [rt_all[rs, sl] for rs, sl in items]
    v_bd = [bd(v_all[rs, sl]) for rs, sl in items]

    quad = [_dot_nt(jnp.concatenate([at[p], rt[p]], axis=0),
                    jnp.concatenate([bd(bt_all[items[p]]), bd(kt_all[items[p]])], axis=0))
            for p in pairs]
    a_ab = [jnp.where(m_strict, q[:c, :PAIR], 0.0) for q in quad]
    a_ak = [jnp.where(m_strict, q[:c, PAIR:], 0.0) for q in quad]
    a_rb = [jnp.where(m_incl, q[c:, :PAIR], 0.0) for q in quad]
    a_rk = [jnp.where(m_incl, q[c:, PAIR:], 0.0) for q in quad]
    av = [_dot(jnp.concatenate([a_ak[p], a_rk[p]], axis=0), v_bd[p]) for p in pairs]
    x0 = [jnp.concatenate([at[p], av[p][:c]], axis=1) for p in pairs]

    t_inv = [eye_rows + a_ab[p] for p in pairs]
    pw = [_dot(a_ab[p], bd(a_ab[p])) for p in pairs]
    t_inv = [t_inv[p] + _dot(pw[p], bd(t_inv[p])) for p in pairs]
    pw = [_dot(pw[p], bd(pw[p])) for p in pairs]
    for _ in range(int(math.log2(c)) - 3):
        both = [_dot(jnp.concatenate([t_inv[p], pw[p]], axis=0), bd(pw[p])) for p in pairs]
        t_inv = [t_inv[p] + both[p][:c] for p in pairs]
        pw = [both[p][c:] for p in pairs]
    t_inv = [t_inv[p] + _dot(t_inv[p], bd(pw[p])) for p in pairs]
    x = [_dot(t_inv[p], bd(x0[p])) for p in pairs]

    top = [_dot(a_rb[p], bd(x[p])) for p in pairs]
    bot = [_dot_tn(jnp.concatenate([bh_all[items[p]], kh_all[items[p]]], axis=0),
                   jnp.concatenate([x[p], jnp.concatenate([zeros_c, v_all[items[p]]], axis=1)], axis=0))
           for p in pairs]

    state = [s_scr[hp] for hp in range(N_PAIRS)]
    for p in pairs:
        rs, sl = items[p]
        hp = p % N_PAIRS
        p_end_row = p_end[rs, sl][c - 1:c, :]
        m_mat = jnp.where(same, bot[p][:, :PAIR], 0.0) + jnp.where(
            eye, jnp.broadcast_to(p_end_row, (PAIR, PAIR)), 0.0)
        lhs = jnp.concatenate([rt[p] + top[p][:, :PAIR], m_mat], axis=0)
        fin = _dot(lhs, state[hp])
        y_ref[0, rs, sl] = fin[:c] + top[p][:, PAIR:] + av[p][c:]
        state[hp] = fin[c:] + jnp.where(same, bot[p][:, PAIR:], 0.0)
    for hp in range(N_PAIRS):
        s_scr[hp] = state[hp]


def _chunk(r, lw, k, v, kk, kka):
    bsz, seq, _ = r.shape
    spec = pl.BlockSpec((1, CHUNKS_PER_STEP * CHUNK, D_MODEL), lambda b, c: (b, c, 0))
    return pl.pallas_call(
        _chunk_kernel,
        grid=(bsz, seq // (CHUNKS_PER_STEP * CHUNK)),
        in_specs=[spec] * 6,
        out_specs=spec,
        out_shape=jax.ShapeDtypeStruct((bsz, seq, D_MODEL), F32),
        scratch_shapes=[pltpu.VMEM((N_PAIRS, PAIR, PAIR), F32)],
        compiler_params=_params("arbitrary", "arbitrary"),
        name="rwkv_chunk",
    )(r, lw, k, v, kk, kka)


def _attn_kernel(cur_ref, prev_ref, o_ref, l_ref, *, dilation, slopes):
    n = pl.program_id(2)
    blk = ATTN_BLK
    gw = ATTN_GROUP_WIDTH
    ii = lax.broadcasted_iota(jnp.int32, (blk, 2 * blk), 0)
    jj = lax.broadcasted_iota(jnp.int32, (blk, 2 * blk), 1)
    delta = ii - jj + blk
    in_window = (delta >= 0) & (delta <= blk)
    dist = delta.astype(F32) * float(dilation)
    lane_head = lax.broadcasted_iota(jnp.int32, (blk, PAIR), 1) // ATTN_HEAD
    lane = lax.broadcasted_iota(jnp.int32, (blk, LANES), 1)
    neg_inf = float("-inf")
    pairs = range(gw // PAIR)
    heads = [(pp, hh) for pp in pairs for hh in range(2)]
    sls = [slice(PAIR * pp, PAIR * (pp + 1)) for pp in pairs]

    for sb in range(2):
        rows = slice(blk * sb, blk * (sb + 1))
        if sb == 0:
            valid = in_window & ((n > 0) | (jj >= blk))
            kcat = [jnp.concatenate([prev_ref[0, 0, :, sl], cur_ref[0, 0, rows, sl]], axis=0) for sl in sls]
            vcat = [jnp.concatenate([prev_ref[0, 0, :, gw + PAIR * pp:gw + PAIR * (pp + 1)],
                                     cur_ref[0, 0, rows, gw + PAIR * pp:gw + PAIR * (pp + 1)]], axis=0)
                    for pp in pairs]
        else:
            valid = in_window
            kcat = [cur_ref[0, 0, :, sl] for sl in sls]
            vcat = [cur_ref[0, 0, :, gw + PAIR * pp:gw + PAIR * (pp + 1)] for pp in pairs]
        q2 = [cur_ref[0, 0, rows, 2 * gw + PAIR * pp:2 * gw + PAIR * (pp + 1)] * (ATTN_HEAD ** -0.5)
              for pp in pairs]
        s = [jnp.where(valid,
                       _dot_nt(jnp.where(lane_head == hh, q2[pp], jnp.zeros_like(q2[pp])), kcat[pp])
                       - float(slopes[2 * pp + hh]) * dist, neg_inf) for pp, hh in heads]
        m = [jnp.max(t, axis=-1, keepdims=True) for t in s]
        p = [jnp.exp(s[i] - m[i]) for i in range(len(heads))]
        den = [jnp.sum(t, axis=-1, keepdims=True) for t in p]
        o = [_dot(p[i], vcat[pp]) * (1.0 / den[i]) for i, (pp, _) in enumerate(heads)]
        for pp in pairs:
            o_ref[0, 0, rows, sls[pp]] = jnp.where(lane_head == 0, o[2 * pp], o[2 * pp + 1]).astype(BF16)
        lse = jnp.zeros((blk, LANES), F32)
        for i in range(len(heads)):
            lse = jnp.where(lane == i, m[i] + jnp.log(den[i]), lse)
        l_ref[0, 0, rows, :] = lse


def _attn_group(kvq, dilation, slopes):
    bsz, _, sub, _ = kvq.shape
    gw = ATTN_GROUP_WIDTH
    step = 2 * ATTN_BLK
    return pl.pallas_call(
        functools.partial(_attn_kernel, dilation=dilation, slopes=tuple(slopes)),
        grid=(bsz, dilation, sub // step),
        in_specs=[pl.BlockSpec((1, 1, step, 3 * gw), lambda b, r, n: (b, r, n, 0)),
                  pl.BlockSpec((1, 1, ATTN_BLK, 2 * gw), lambda b, r, n: (b, r, jnp.maximum(2 * n - 1, 0), 0))],
        out_specs=[pl.BlockSpec((1, 1, step, gw), lambda b, r, n: (b, r, n, 0)),
                   pl.BlockSpec((1, 1, step, LANES), lambda b, r, n: (b, r, n, 0))],
        out_shape=[jax.ShapeDtypeStruct((bsz, dilation, sub, gw), BF16),
                   jax.ShapeDtypeStruct((bsz, dilation, sub, LANES), F32)],
        compiler_params=_params("arbitrary", "arbitrary", "arbitrary"),
        name=f"dilated_attn_d{dilation}",
    )(kvq, kvq)


def _mix_kernel(y_ref, bonus_ref, g_ref, za_ref, zb_ref, x_ref, o0_ref, o1_ref, o2_ref,
                l0_ref, l1_ref, l2_ref, gt_ref, lnw_ref, lnb_ref, gpost_ref,
                wor_ref, woa_ref, wout_ref, e_ref, spread_ref, out_ref, *scratch):
    e = e_ref[...]
    inv_n = 1.0 / RWKV_HEAD
    y = y_ref[...]
    mean = _seg_sum(y, e, pieces=2) * inv_n
    yc = y - mean
    var = _seg_sum(yc * yc, e) * inv_n
    yn = yc * lax.rsqrt(var + GN_EPS) * lnw_ref[...] + lnb_ref[...]
    y_rwkv = _dot((yn + bonus_ref[...].astype(F32)) * g_ref[...].astype(F32), wor_ref[...])

    def token_order(ref, scr):
        dilation, rows = ref.shape[1], ref.shape[2]
        if dilation == 1:
            return ref[0, 0].astype(F32)
        for r in range(dilation):
            for cg in range(ref.shape[3] // LANES):
                scr[cg, pl.ds(r, rows, stride=dilation), :] = (
                    ref[0, r, :, LANES * cg:LANES * (cg + 1)].astype(F32))
        return jnp.concatenate([scr[cg] for cg in range(ref.shape[3] // LANES)], axis=1)

    def spread(w):
        return jnp.dot(w.astype(BF16), spread_ref[...], preferred_element_type=F32)

    o0, l0 = token_order(o0_ref, None), token_order(l0_ref, None)
    o1, l1 = token_order(o1_ref, scratch[0]), token_order(l1_ref, scratch[1])
    o2, l2 = token_order(o2_ref, scratch[2]), token_order(l2_ref, scratch[3])
    mx = jnp.maximum(jnp.maximum(l0, l1), l2)
    e0, e1, e2 = jnp.exp(l0 - mx), jnp.exp(l1 - mx), jnp.exp(l2 - mx)
    inv = 1.0 / (e0 + e1 + e2)
    o = spread(e0 * inv) * o0 + spread(e1 * inv) * o1 + spread(e2 * inv) * o2
    y_attn = _dot(o, woa_ref[...])

    mixed = _dot(jax.nn.sigmoid(za_ref[...].astype(F32)) * y_rwkv
                 + jax.nn.sigmoid(zb_ref[...].astype(F32)) * y_attn,
                 wout_ref[...])
    out_ref[...] = x_ref[...] + gt_ref[0] * (_rms(mixed) * gpost_ref[...])


def _mix(y, bonus, g, proj, x2, o_l, gt, ln_w, ln_b, g_post, w_or, w_oa, w_out, e_seg, seq, tm):
    n_tok = x2.shape[0]
    tpb = seq // tm
    row = lambda i: (i, 0)
    const = lambda i: (0, 0)
    tok = pl.BlockSpec((tm, D_MODEL), row)
    vec = pl.BlockSpec((1, D_MODEL), const)

    def att(dilation, width):
        return pl.BlockSpec((1, dilation, tm // dilation, width), lambda i: (i // tpb, 0, i % tpb, 0))

    (o0, l0), (o1, l1), (o2, l2) = o_l
    d0, d1, d2 = [d for _, d in ATTN_GROUPS]
    assert d0 == 1
    gw = ATTN_GROUP_WIDTH
    head_of_lane = jnp.arange(gw) // ATTN_HEAD
    spread = (jnp.arange(LANES)[:, None] == head_of_lane[None, :]).astype(BF16)
    return pl.pallas_call(
        _mix_kernel,
        grid=(n_tok // tm,),
        in_specs=[tok, tok, tok,
                  pl.BlockSpec((tm, D_MODEL), lambda i: (i, COL_ZA // D_MODEL)),
                  pl.BlockSpec((tm, D_MODEL), lambda i: (i, COL_ZB // D_MODEL)),
                  tok, att(d0, gw), att(d1, gw), att(d2, gw), att(d0, LANES), att(d1, LANES), att(d2, LANES),
                  pl.BlockSpec((1, 1, D_MODEL), lambda i: (i // tpb, 0, 0)),
                  vec, vec, vec,
                  pl.BlockSpec((D_MODEL, D_MODEL), const, pipeline_mode=pl.Buffered(1)),
                  pl.BlockSpec((gw, D_MODEL), const, pipeline_mode=pl.Buffered(1)),
                  pl.BlockSpec((D_MODEL, D_MODEL), const, pipeline_mode=pl.Buffered(1)),
                  pl.BlockSpec((SEG_TILE, SEG_TILE), const),
                  pl.BlockSpec((LANES, gw), const)],
        out_specs=tok,
        out_shape=jax.ShapeDtypeStruct((n_tok, D_MODEL), F32),
        scratch_shapes=[pltpu.VMEM((gw // LANES, tm, LANES), F32), pltpu.VMEM((1, tm, LANES), F32)] * 2,
        compiler_params=_params("arbitrary"),
        name="mix_out",
    )(y, bonus, g, proj, proj, x2, o0, o1, o2, l0, l1, l2, gt, ln_w, ln_b, g_post,
      w_or, w_oa, w_out, e_seg, spread)


def _ffn_kernel(x_ref, sc_ref, sh_ref, gt_ref, gpre_ref, gpost_ref, wg_ref, wu_ref, wo_ref, out_ref):
    x = x_ref[...]
    h = (_rms(x) * gpre_ref[...] * (1.0 + sc_ref[0]) + sh_ref[0]).astype(BF16)
    acts = []
    for t in range(FFN_HIDDEN // SEG_TILE):
        sl = slice(SEG_TILE * t, SEG_TILE * (t + 1))
        u_gate = jnp.dot(h, wg_ref[:, sl], preferred_element_type=F32)
        u_up = jnp.dot(h, wu_ref[:, sl], preferred_element_type=F32)
        acts.append((u_gate * jax.nn.sigmoid(u_gate) * u_up).astype(BF16))
    y = jnp.dot(jnp.concatenate(acts, axis=1), wo_ref[...], preferred_element_type=F32)
    out_ref[...] = x + gt_ref[0] * (_rms(y) * gpost_ref[...])


def _ffn(x1, sc, sh, gt, g_pre, g_post, w_in, w_out, seq, tm):
    n_tok = x1.shape[0]
    tpb = seq // tm
    row = lambda i: (i, 0)
    const = lambda i: (0, 0)
    modrow = lambda i: (i // tpb, 0, 0)
    tok = pl.BlockSpec((tm, D_MODEL), row)
    mod = pl.BlockSpec((1, 1, D_MODEL), modrow)
    vec = pl.BlockSpec((1, D_MODEL), const)
    resident = pl.Buffered(1)
    return pl.pallas_call(
        _ffn_kernel,
        grid=(n_tok // tm,),
        in_specs=[tok, mod, mod, mod, vec, vec,
                  pl.BlockSpec((D_MODEL, FFN_HIDDEN), lambda i: (0, 0), pipeline_mode=resident),
                  pl.BlockSpec((D_MODEL, FFN_HIDDEN), lambda i: (0, 1), pipeline_mode=resident),
                  pl.BlockSpec((FFN_HIDDEN, D_MODEL), const, pipeline_mode=resident)],
        out_specs=tok,
        out_shape=jax.ShapeDtypeStruct((n_tok, D_MODEL), F32),
        compiler_params=_params("arbitrary"),
        name="swiglu_ffn",
    )(x1, sc, sh, gt, g_pre, g_post, w_in, w_in, w_out)


def _pad_cols(w, n):
    return jnp.pad(w, ((0, 0), (0, n - w.shape[1]))).astype(BF16)


def _pad_rows(w, n):
    return jnp.pad(w, ((0, n - w.shape[0]), (0, 0))).astype(BF16)


def kernel(x, c, w_mod, b_mod, g_pre_mix, g_post_mix, g_pre_ffn, g_post_ffn, w_in, mu_rkv, mu_lora,
           w0, w1, w2, a0, a1, a2, g1, g2, k_k, k_a, r_k, ln_x_w, ln_x_b, w_o_rwkv, w_o_attn, w_out,
           w_ffn_in, w_ffn_out):
    bsz, seq, _ = x.shape
    depth = w_mod.shape[0]
    assert seq % (2 * ATTN_BLK * ATTN_GROUPS[-1][1]) == 0
    n_tok = bsz * seq
    tm_proj = min(512, seq)
    tm_aproj = min(512, seq)
    tm_prep = min(512, seq)
    tm_mix = min(512, seq)
    tm_ffn = min(512, seq)

    slopes = _alibi_slopes(len(ATTN_GROUPS) * ATTN_HEADS_PER_GROUP)
    seg = jnp.arange(SEG_TILE) // RWKV_HEAD
    e_seg = (seg[:, None] == seg[None, :]).astype(BF16)
    row = lambda t: t.reshape(1, -1)

    x2 = x.reshape(n_tok, D_MODEL)
    for l in range(depth):
        mod = _mod(c, w_mod[l], b_mod[l])
        sh_m, sc_m, gt_m, sh_f, sc_f, gt_f = [mod[:, i * D_MODEL:(i + 1) * D_MODEL].reshape(bsz, 1, D_MODEL)
                                               for i in range(6)]
        wl = w_in[l]
        rkv_cols = 3 * D_MODEL
        att_cols = 3 * ATTN_WIDTH
        w_main = jnp.concatenate([wl[:, :rkv_cols], wl[:, rkv_cols + att_cols:]], axis=1).astype(BF16)

        proj, h, lw, la, lg = _inproj(
            x2, sc_m, sh_m, row(g_pre_mix[l]), mu_lora[l],
            _pad_cols(w1[l], LORA_DECAY_PAD), _pad_cols(a1[l], LORA_ICLR_PAD),
            _pad_cols(g1[l], LORA_GATE_PAD), w_main, seq, tm_proj)

        r, lwd, k2, v, kk, kka, g, bonus = _prep(
            proj, lw, la, lg, mu_rkv[l], row(w0[l]), row(a0[l]), row(k_k[l]), row(k_a[l]), row(r_k[l]),
            _pad_rows(w2[l], LORA_DECAY_PAD), _pad_rows(a2[l], LORA_ICLR_PAD),
            _pad_rows(g2[l], LORA_GATE_PAD), e_seg, seq, tm_prep)

        b3 = lambda t: t.reshape(bsz, seq, D_MODEL)
        y = _chunk(b3(r), b3(lwd), b3(k2), b3(v), b3(kk), b3(kka)).reshape(n_tok, D_MODEL)

        def group_cols(which, gi):
            lo = rkv_cols + which * ATTN_WIDTH + gi * ATTN_GROUP_WIDTH
            return wl[:, lo:lo + ATTN_GROUP_WIDTH]

        w_kvq = jnp.stack([jnp.concatenate([group_cols(1, gi), group_cols(2, gi), group_cols(0, gi)], axis=1)
                           for gi in range(len(ATTN_GROUPS))]).astype(BF16)
        kvq = _attn_proj(h, w_kvq, bsz, seq, tm_aproj)
        o_l = []
        for gi, (window, dilation) in enumerate(ATTN_GROUPS):
            assert window // dilation == ATTN_BLK
            o_l.append(_attn_group(
                kvq[gi], dilation, slopes[gi * ATTN_HEADS_PER_GROUP:(gi + 1) * ATTN_HEADS_PER_GROUP]))

        x2 = _mix(y, bonus, g, proj, x2, o_l, gt_m, row(ln_x_w[l]), row(ln_x_b[l]), row(g_post_mix[l]),
                  w_o_rwkv[l].astype(BF16), w_o_attn[l].astype(BF16), w_out[l].astype(BF16), e_seg,
                  seq, tm_mix)

        x2 = _ffn(x2, sc_f, sh_f, gt_f, row(g_pre_ffn[l]), row(g_post_ffn[l]),
                  w_ffn_in[l].astype(BF16), w_ffn_out[l].astype(BF16), seq, tm_ffn)
    return x2.reshape(bsz, seq, D_MODEL)
```

```python
import functools
import math

import jax
import jax.numpy as jnp
from jax import lax
from jax.experimental import pallas as pl
from jax.experimental.pallas import tpu as pltpu

F32 = jnp.float32
BF16 = jnp.bfloat16
HIGHEST = lax.Precision.HIGHEST

LANES = 128
SEG_TILE = 256
D_MODEL = 1024
RWKV_HEAD = 64
GN_EPS = 64e-5
RMS_EPS = 1e-6
ATTN_HEAD = 64
ATTN_HEADS_PER_GROUP = 8
ATTN_GROUPS = ((128, 1), (512, 4), (2048, 16))
ATTN_GROUP_WIDTH = ATTN_HEADS_PER_GROUP * ATTN_HEAD
ATTN_WIDTH = len(ATTN_GROUPS) * ATTN_GROUP_WIDTH
ATTN_BLK = 128
FFN_HIDDEN = 2816
LORA_DECAY_PAD = 128
LORA_ICLR_PAD = 128
LORA_GATE_PAD = 256
CHUNK = 64
CHUNKS_PER_STEP = 4
PAIR = 2 * RWKV_HEAD
N_PAIRS = D_MODEL // PAIR

COL_R, COL_K, COL_V, COL_ZA, COL_ZB = 0, 1024, 2048, 3072, 4096
MAIN_COLS = 5120
PROJ_TILE = 5120

VMEM_LIMIT = 48 * 1024 * 1024


def _alibi_slopes(n):
    def pow2(m):
        start = 2.0 ** (-8.0 / m)
        return [start ** (i + 1) for i in range(m)]
    if math.log2(n).is_integer():
        s = pow2(n)
    else:
        p = 2 ** int(math.floor(math.log2(n)))
        s = pow2(p) + pow2(2 * p)[0::2][: n - p]
    return sorted(s, reverse=True)


def _dot(a, b):
    return jnp.dot(a.astype(BF16), b.astype(BF16), preferred_element_type=F32)


def _dot_nt(a, b):
    return lax.dot_general(a.astype(BF16), b.astype(BF16), (((1,), (1,)), ((), ())),
                           preferred_element_type=F32)


def _dot_tn(a, b):
    return lax.dot_general(a.astype(BF16), b.astype(BF16), (((0,), (0,)), ((), ())),
                           preferred_element_type=F32)


def _seg_sum(x, e, pieces=1):
    parts = [x.astype(BF16)]
    if pieces == 2:
        parts.append((x - parts[0].astype(F32)).astype(BF16))
    outs = []
    for t in range(x.shape[1] // SEG_TILE):
        sl = slice(SEG_TILE * t, SEG_TILE * (t + 1))
        outs.append(sum(jnp.dot(part[:, sl], e, preferred_element_type=F32) for part in parts))
    return jnp.concatenate(outs, axis=1)


def _shift_rows(t, prev_row):
    rows = lax.broadcasted_iota(jnp.int32, t.shape, 0)
    return jnp.where(rows == 0, prev_row, pltpu.roll(t, 1, 0))


def _rms(t):
    return t * lax.rsqrt(jnp.mean(t * t, axis=-1, keepdims=True) + RMS_EPS)


def _params(*semantics):
    return pltpu.CompilerParams(dimension_semantics=semantics, vmem_limit_bytes=VMEM_LIMIT)


def _mod_kernel(c_ref, w_ref, b_ref, o_ref):
    o_ref[...] = jnp.dot(c_ref[...], w_ref[...], preferred_element_type=F32,
                         precision=HIGHEST) + b_ref[...]


def _mod(c, w_mod, b_mod):
    bsz = c.shape[0]
    n_out = w_mod.shape[1]
    return pl.pallas_call(
        _mod_kernel,
        grid=(n_out // D_MODEL,),
        in_specs=[pl.BlockSpec((bsz, D_MODEL), lambda j: (0, 0)),
                  pl.BlockSpec((D_MODEL, D_MODEL), lambda j: (0, j)),
                  pl.BlockSpec((1, D_MODEL), lambda j: (0, j))],
        out_specs=pl.BlockSpec((bsz, D_MODEL), lambda j: (0, j)),
        out_shape=jax.ShapeDtypeStruct((bsz, n_out), F32),
        compiler_params=_params("arbitrary"),
        name="adaln_mod",
    )(c, w_mod, b_mod.reshape(1, n_out))


def _inproj_kernel(x_ref, sc_ref, sh_ref, g_ref, mu_ref, w1_ref, a1_ref, g1_ref, win_ref,
                   p_ref, h_ref, lw_ref, la_ref, lg_ref, last_scr, *, tiles_per_batch):
    i = pl.program_id(0)
    j = pl.program_id(1)
    tm = x_ref.shape[0]

    @pl.when(j == 0)
    def _():
        @pl.when(i % tiles_per_batch == 0)
        def _():
            last_scr[...] = jnp.zeros_like(last_scr)

        h = _rms(x_ref[...]) * g_ref[...] * (1.0 + sc_ref[0]) + sh_ref[0]
        hs = _shift_rows(h, last_scr[...])
        last_scr[...] = h[tm - 1:tm, :]
        dh = hs - h
        lw_ref[...] = _dot(h + dh * mu_ref[0:1, :], w1_ref[...])
        la_ref[...] = _dot(h + dh * mu_ref[1:2, :], a1_ref[...])
        lg_ref[...] = _dot(h + dh * mu_ref[2:3, :], g1_ref[...])
        h_ref[...] = h.astype(BF16)

    p_ref[...] = jnp.dot(h_ref[...], win_ref[...], preferred_element_type=F32).astype(BF16)


def _inproj(x2, sc, sh, g_pre, mu_lora, w1p, a1p, g1p, w_main, seq, tm):
    n_tok = x2.shape[0]
    tpb = seq // tm
    row = lambda i, j: (i, 0)
    const = lambda i, j: (0, 0)
    modrow = lambda i, j: (i // tpb, 0, 0)
    return pl.pallas_call(
        functools.partial(_inproj_kernel, tiles_per_batch=tpb),
        grid=(n_tok // tm, MAIN_COLS // PROJ_TILE),
        in_specs=[pl.BlockSpec((tm, D_MODEL), row),
                  pl.BlockSpec((1, 1, D_MODEL), modrow),
                  pl.BlockSpec((1, 1, D_MODEL), modrow),
                  pl.BlockSpec((1, D_MODEL), const),
                  pl.BlockSpec((3, D_MODEL), const),
                  pl.BlockSpec((D_MODEL, LORA_DECAY_PAD), const),
                  pl.BlockSpec((D_MODEL, LORA_ICLR_PAD), const),
                  pl.BlockSpec((D_MODEL, LORA_GATE_PAD), const),
                  pl.BlockSpec((D_MODEL, PROJ_TILE), lambda i, j: (0, j),
                               pipeline_mode=pl.Buffered(1 if PROJ_TILE == MAIN_COLS else 2))],
        out_specs=[pl.BlockSpec((tm, PROJ_TILE), lambda i, j: (i, j)),
                   pl.BlockSpec((tm, D_MODEL), row),
                   pl.BlockSpec((tm, LORA_DECAY_PAD), row),
                   pl.BlockSpec((tm, LORA_ICLR_PAD), row),
                   pl.BlockSpec((tm, LORA_GATE_PAD), row)],
        out_shape=[jax.ShapeDtypeStruct((n_tok, MAIN_COLS), BF16),
                   jax.ShapeDtypeStruct((n_tok, D_MODEL), BF16),
                   jax.ShapeDtypeStruct((n_tok, LORA_DECAY_PAD), F32),
                   jax.ShapeDtypeStruct((n_tok, LORA_ICLR_PAD), F32),
                   jax.ShapeDtypeStruct((n_tok, LORA_GATE_PAD), F32)],
        scratch_shapes=[pltpu.VMEM((1, D_MODEL), F32)],
        compiler_params=_params("arbitrary", "arbitrary"),
        name="in_proj",
    )(x2, sc, sh, g_pre, mu_lora, w1p, a1p, g1p, w_main)


def _attn_proj_kernel(h_ref, w_ref, *refs):
    n_groups = len(ATTN_GROUPS)
    out_refs, scratch = refs[:n_groups], list(refs[n_groups:])
    h = h_ref[...]
    for gi, (_, dilation) in enumerate(ATTN_GROUPS):
        o_ref = out_refs[gi]
        acc = jnp.dot(h, w_ref[gi], preferred_element_type=F32)
        if dilation == 1:
            o_ref[0, 0] = acc.astype(BF16)
            continue
        scr = scratch.pop(0)
        rows = acc.shape[0] // dilation
        for cg in range(scr.shape[0]):
            scr[cg] = acc[:, LANES * cg:LANES * (cg + 1)]
        for r in range(dilation):
            for cg in range(scr.shape[0]):
                o_ref[0, r, :, LANES * cg:LANES * (cg + 1)] = (
                    scr[cg, pl.ds(r, rows, stride=dilation), :].astype(BF16))


def _attn_proj(h, w_kvq, bsz, seq, tm):
    tpb = seq // tm
    width = 3 * ATTN_GROUP_WIDTH
    dilations = [d for _, d in ATTN_GROUPS]
    return pl.pallas_call(
        _attn_proj_kernel,
        grid=(bsz * tpb,),
        in_specs=[pl.BlockSpec((tm, D_MODEL), lambda i: (i, 0)),
                  pl.BlockSpec((len(dilations), D_MODEL, width), lambda i: (0, 0, 0),
                               pipeline_mode=pl.Buffered(1))],
        out_specs=[pl.BlockSpec((1, d, tm // d, width), lambda i: (i // tpb, 0, i % tpb, 0))
                   for d in dilations],
        out_shape=[jax.ShapeDtypeStruct((bsz, d, seq // d, width), BF16) for d in dilations],
        scratch_shapes=[pltpu.VMEM((width // LANES, tm, LANES), F32) for d in dilations if d != 1],
        compiler_params=_params("arbitrary"),
        name="attn_proj",
    )(h, w_kvq)


def _prep_kernel(pr_ref, pk_ref, pv_ref, lw_ref, la_ref, lg_ref, mu_ref, w0_ref, a0_ref, kk_ref,
                 ka_ref, rk_ref, w2_ref, a2_ref, g2_ref, e_ref,
                 r_o, lw_o, k_o, v_o, kk_o, kka_o, g_o, bonus_o, last_scr, *, tiles_per_batch):
    i = pl.program_id(0)
    tm = pr_ref.shape[0]

    @pl.when(i % tiles_per_batch == 0)
    def _():
        last_scr[...] = jnp.zeros_like(last_scr)

    def mix(p_ref, idx):
        p = p_ref[...].astype(F32)
        ps = _shift_rows(p, last_scr[idx:idx + 1, :])
        last_scr[idx:idx + 1, :] = p[tm - 1:tm, :]
        return p + (ps - p) * mu_ref[idx:idx + 1, :]

    r = mix(pr_ref, 0)
    k = mix(pk_ref, 1)
    v = mix(pv_ref, 2)
    e = e_ref[...]

    z = w0_ref[...] + _dot(jnp.tanh(lw_ref[...]), w2_ref[...])
    lw_o[...] = -math.exp(-0.5) * jax.nn.sigmoid(z)
    a = jax.nn.sigmoid(a0_ref[...] + _dot(la_ref[...], a2_ref[...]))
    g_o[...] = _dot(jax.nn.sigmoid(lg_ref[...]), g2_ref[...]).astype(BF16)

    kk = k * kk_ref[...]
    kk = kk * lax.rsqrt(jnp.maximum(_seg_sum(kk * kk, e), 1e-24))
    k2 = k * (1.0 + (a - 1.0) * ka_ref[...])
    r_o[...] = r.astype(BF16)
    k_o[...] = k2.astype(BF16)
    v_o[...] = v.astype(BF16)
    kk_o[...] = kk.astype(BF16)
    kka_o[...] = (kk * a).astype(BF16)
    bonus_o[...] = (_seg_sum(r * k2 * rk_ref[...], e) * v).astype(BF16)


def _prep(proj, lw, la, lg, mu_rkv, w0, a0, k_k, k_a, r_k, w2p, a2p, g2p, e_seg, seq, tm):
    n_tok = proj.shape[0]
    tpb = seq // tm
    row = lambda i: (i, 0)
    const = lambda i: (0, 0)
    vec = pl.BlockSpec((1, D_MODEL), const)
    tok = pl.BlockSpec((tm, D_MODEL), row)
    return pl.pallas_call(
        functools.partial(_prep_kernel, tiles_per_batch=tpb),
        grid=(n_tok // tm,),
        in_specs=[pl.BlockSpec((tm, D_MODEL), lambda i: (i, COL_R // D_MODEL)),
                  pl.BlockSpec((tm, D_MODEL), lambda i: (i, COL_K // D_MODEL)),
                  pl.BlockSpec((tm, D_MODEL), lambda i: (i, COL_V // D_MODEL)),
                  pl.BlockSpec((tm, LORA_DECAY_PAD), row),
                  pl.BlockSpec((tm, LORA_ICLR_PAD), row),
                  pl.BlockSpec((tm, LORA_GATE_PAD), row),
                  pl.BlockSpec((3, D_MODEL), const),
                  vec, vec, vec, vec, vec,
                  pl.BlockSpec((LORA_DECAY_PAD, D_MODEL), const),
                  pl.BlockSpec((LORA_ICLR_PAD, D_MODEL), const),
                  pl.BlockSpec((LORA_GATE_PAD, D_MODEL), const),
                  pl.BlockSpec((SEG_TILE, SEG_TILE), const)],
        out_specs=[tok] * 8,
        out_shape=[jax.ShapeDtypeStruct((n_tok, D_MODEL), F32 if name == "lw" else BF16)
                   for name in ("r", "lw", "k", "v", "kk", "kka", "g", "bonus")],
        scratch_shapes=[pltpu.VMEM((3, D_MODEL), F32)],
        compiler_params=_params("arbitrary"),
        name="rwkv_prep",
    )(proj, proj, proj, lw, la, lg, mu_rkv, w0, a0, k_k, k_a, r_k, w2p, a2p, g2p, e_seg)


def _chunk_kernel(r_ref, lw_ref, k_ref, v_ref, kk_ref, kka_ref, y_ref, s_scr):
    @pl.when(pl.program_id(1) == 0)
    def _():
        s_scr[...] = jnp.zeros_like(s_scr)

    c = CHUNK
    rows = lw_ref.shape[1]
    n_chunks = rows // c
    lw = lw_ref[0]
    ti = lax.broadcasted_iota(jnp.int32, (rows, rows), 0)
    tj = lax.broadcasted_iota(jnp.int32, (rows, rows), 1)
    tril = ((ti >= tj) & (ti // c == tj // c)).astype(BF16)
    lw_1 = lw.astype(BF16)
    lw_2 = (lw - lw_1.astype(F32)).astype(BF16)
    lw_3 = (lw - lw_1.astype(F32) - lw_2.astype(F32)).astype(BF16)
    cum = sum(jnp.dot(tril, piece, preferred_element_type=F32) for piece in (lw_1, lw_2, lw_3))
    cum_prev = cum - lw
    cum_end = jnp.concatenate(
        [jnp.broadcast_to(cum[c * (ck + 1) - 1:c * (ck + 1), :], (c, D_MODEL)) for ck in range(n_chunks)], axis=0)
    kk = kk_ref[0].astype(F32)
    kka = kka_ref[0].astype(F32)
    k = k_ref[0].astype(F32)
    e_neg = jnp.exp(-cum)
    e_end = jnp.exp(cum_end - cum)
    at_all = -kk * jnp.exp(cum_prev)
    rt_all = r_ref[0].astype(F32) * jnp.exp(cum)
    bt_all = kka * e_neg
    kt_all = k * e_neg
    bh_all = kka * e_end
    kh_all = k * e_end
    p_end = jnp.exp(cum_end)
    v_all = v_ref[0].astype(F32)

    ri = lax.broadcasted_iota(jnp.int32, (PAIR, PAIR), 0)
    ci = lax.broadcasted_iota(jnp.int32, (PAIR, PAIR), 1)
    same = (ri // RWKV_HEAD) == (ci // RWKV_HEAD)
    same2 = jnp.concatenate([same, same], axis=1)
    eye = ri == ci
    tt = lax.broadcasted_iota(jnp.int32, (c, PAIR), 0)
    ss = lax.broadcasted_iota(jnp.int32, (c, PAIR), 1) % c
    m_strict = tt > ss
    m_incl = tt >= ss
    eye_rows = (tt == ss).astype(F32)
    zeros_c = jnp.zeros((c, PAIR), F32)

    def bd(t):
        mask = same if t.shape[1] == PAIR else same2
        return jnp.where(mask, jnp.concatenate([t, t], axis=0), 0.0).astype(BF16)

    items = [(slice(c * ck, c * (ck + 1)), slice(PAIR * p, PAIR * (p + 1)))
             for ck in range(n_chunks) for p in range(N_PAIRS)]
    pairs = range(len(items))
    at = [at_all[rs, sl] for rs, sl in items]
    rt = [rt_all[rs, sl] for rs, sl in items]
    v_bd = [bd(v_all[rs, sl]) for rs, sl in items]

    quad = [_dot_nt(jnp.concatenate([at[p], rt[p]], axis=0),
                    jnp.concatenate([bd(bt_all[items[p]]), bd(kt_all[items[p]])], axis=0))
            for p in pairs]
    a_ab = [jnp.where(m_strict, q[:c, :PAIR], 0.0) for q in quad]
    a_ak = [jnp.where(m_strict, q[:c, PAIR:], 0.0) for q in quad]
    a_rb = [jnp.where(m_incl, q[c:, :PAIR], 0.0) for q in quad]
    a_rk = [jnp.where(m_incl, q[c:, PAIR:], 0.0) for q in quad]
    av = [_dot(jnp.concatenate([a_ak[p], a_rk[p]], axis=0), v_bd[p]) for p in pairs]
    x0 = [jnp.concatenate([at[p], av[p][:c]], axis=1) for p in pairs]

    t_inv = [eye_rows + a_ab[p] for p in pairs]
    pw = [_dot(a_ab[p], bd(a_ab[p])) for p in pairs]
    t_inv = [t_inv[p] + _dot(pw[p], bd(t_inv[p])) for p in pairs]
    pw = [_dot(pw[p], bd(pw[p])) for p in pairs]
    for _ in range(int(math.log2(c)) - 3):
        both = [_dot(jnp.concatenate([t_inv[p], pw[p]], axis=0), bd(pw[p])) for p in pairs]
        t_inv = [t_inv[p] + both[p][:c] for p in pairs]
        pw = [both[p][c:] for p in pairs]
    t_inv = [t_inv[p] + _dot(t_inv[p], bd(pw[p])) for p in pairs]
    x = [_dot(t_inv[p], bd(x0[p])) for p in pairs]

    top = [_dot(a_rb[p], bd(x[p])) for p in pairs]
    bot = [_dot_tn(jnp.concatenate([bh_all[items[p]], kh_all[items[p]]], axis=0),
                   jnp.concatenate([x[p], jnp.concatenate([zeros_c, v_all[items[p]]], axis=1)], axis=0))
           for p in pairs]

    state = [s_scr[hp] for hp in range(N_PAIRS)]
    for p in pairs:
        rs, sl = items[p]
        hp = p % N_PAIRS
        p_end_row = p_end[rs, sl][c - 1:c, :]
        m_mat = jnp.where(same, bot[p][:, :PAIR], 0.0) + jnp.where(
            eye, jnp.broadcast_to(p_end_row, (PAIR, PAIR)), 0.0)
        lhs = jnp.concatenate([rt[p] + top[p][:, :PAIR], m_mat], axis=0)
        fin = _dot(lhs, state[hp])
        y_ref[0, rs, sl] = fin[:c] + top[p][:, PAIR:] + av[p][c:]
        state[hp] = fin[c:] + jnp.where(same, bot[p][:, PAIR:], 0.0)
    for hp in range(N_PAIRS):
        s_scr[hp] = state[hp]


def _chunk(r, lw, k, v, kk, kka):
    bsz, seq, _ = r.shape
    spec = pl.BlockSpec((1, CHUNKS_PER_STEP * CHUNK, D_MODEL), lambda b, c: (b, c, 0))
    return pl.pallas_call(
        _chunk_kernel,
        grid=(bsz, seq // (CHUNKS_PER_STEP * CHUNK)),
        in_specs=[spec] * 6,
        out_specs=spec,
        out_shape=jax.ShapeDtypeStruct((bsz, seq, D_MODEL), F32),
        scratch_shapes=[pltpu.VMEM((N_PAIRS, PAIR, PAIR), F32)],
        compiler_params=_params("arbitrary", "arbitrary"),
        name="rwkv_chunk",
    )(r, lw, k, v, kk, kka)


def _attn_kernel(cur_ref, prev_ref, o_ref, l_ref, *, dilation, slopes):
    n = pl.program_id(2)
    blk = ATTN_BLK
    gw = ATTN_GROUP_WIDTH
    ii = lax.broadcasted_iota(jnp.int32, (blk, 2 * blk), 0)
    jj = lax.broadcasted_iota(jnp.int32, (blk, 2 * blk), 1)
    delta = ii - jj + blk
    in_window = (delta >= 0) & (delta <= blk)
    dist = delta.astype(F32) * float(dilation)
    lane_head = lax.broadcasted_iota(jnp.int32, (blk, PAIR), 1) // ATTN_HEAD
    lane = lax.broadcasted_iota(jnp.int32, (blk, LANES), 1)
    neg_inf = float("-inf")
    pairs = range(gw // PAIR)
    heads = [(pp, hh) for pp in pairs for hh in range(2)]
    sls = [slice(PAIR * pp, PAIR * (pp + 1)) for pp in pairs]

    for sb in range(2):
        rows = slice(blk * sb, blk * (sb + 1))
        if sb == 0:
            valid = in_window & ((n > 0) | (jj >= blk))
            kcat = [jnp.concatenate([prev_ref[0, 0, :, sl], cur_ref[0, 0, rows, sl]], axis=0) for sl in sls]
            vcat = [jnp.concatenate([prev_ref[0, 0, :, gw + PAIR * pp:gw + PAIR * (pp + 1)],
                                     cur_ref[0, 0, rows, gw + PAIR * pp:gw + PAIR * (pp + 1)]], axis=0)
                    for pp in pairs]
        else:
            valid = in_window
            kcat = [cur_ref[0, 0, :, sl] for sl in sls]
            vcat = [cur_ref[0, 0, :, gw + PAIR * pp:gw + PAIR * (pp + 1)] for pp in pairs]
        q2 = [cur_ref[0, 0, rows, 2 * gw + PAIR * pp:2 * gw + PAIR * (pp + 1)] * (ATTN_HEAD ** -0.5)
              for pp in pairs]
        s = [jnp.where(valid,
                       _dot_nt(jnp.where(lane_head == hh, q2[pp], jnp.zeros_like(q2[pp])), kcat[pp])
                       - float(slopes[2 * pp + hh]) * dist, neg_inf) for pp, hh in heads]
        m = [jnp.max(t, axis=-1, keepdims=True) for t in s]
        p = [jnp.exp(s[i] - m[i]) for i in range(len(heads))]
        den = [jnp.sum(t, axis=-1, keepdims=True) for t in p]
        o = [_dot(p[i], vcat[pp]) * (1.0 / den[i]) for i, (pp, _) in enumerate(heads)]
        for pp in pairs:
            o_ref[0, 0, rows, sls[pp]] = jnp.where(lane_head == 0, o[2 * pp], o[2 * pp + 1]).astype(BF16)
        lse = jnp.zeros((blk, LANES), F32)
        for i in range(len(heads)):
            lse = jnp.where(lane == i, m[i] + jnp.log(den[i]), lse)
        l_ref[0, 0, rows, :] = lse


def _attn_group(kvq, dilation, slopes):
    bsz, _, sub, _ = kvq.shape
    gw = ATTN_GROUP_WIDTH
    step = 2 * ATTN_BLK
    return pl.pallas_call(
        functools.partial(_attn_kernel, dilation=dilation, slopes=tuple(slopes)),
        grid=(bsz, dilation, sub // step),
        in_specs=[pl.BlockSpec((1, 1, step, 3 * gw), lambda b, r, n: (b, r, n, 0)),
                  pl.BlockSpec((1, 1, ATTN_BLK, 2 * gw), lambda b, r, n: (b, r, jnp.maximum(2 * n - 1, 0), 0))],
        out_specs=[pl.BlockSpec((1, 1, step, gw), lambda b, r, n: (b, r, n, 0)),
                   pl.BlockSpec((1, 1, step, LANES), lambda b, r, n: (b, r, n, 0))],
        out_shape=[jax.ShapeDtypeStruct((bsz, dilation, sub, gw), BF16),
                   jax.ShapeDtypeStruct((bsz, dilation, sub, LANES), F32)],
        compiler_params=_params("arbitrary", "arbitrary", "arbitrary"),
        name=f"dilated_attn_d{dilation}",
    )(kvq, kvq)


def _mix_kernel(y_ref, bonus_ref, g_ref, za_ref, zb_ref, x_ref, o0_ref, o1_ref, o2_ref,
                l0_ref, l1_ref, l2_ref, gt_ref, lnw_ref, lnb_ref, gpost_ref,
                wor_ref, woa_ref, wout_ref, e_ref, spread_ref, out_ref, *scratch):
    e = e_ref[...]
    inv_n = 1.0 / RWKV_HEAD
    y = y_ref[...]
    mean = _seg_sum(y, e, pieces=2) * inv_n
    yc = y - mean
    var = _seg_sum(yc * yc, e) * inv_n
    yn = yc * lax.rsqrt(var + GN_EPS) * lnw_ref[...] + lnb_ref[...]
    y_rwkv = _dot((yn + bonus_ref[...].astype(F32)) * g_ref[...].astype(F32), wor_ref[...])

    def token_order(ref, scr):
        dilation, rows = ref.shape[1], ref.shape[2]
        if dilation == 1:
            return ref[0, 0].astype(F32)
        for r in range(dilation):
            for cg in range(ref.shape[3] // LANES):
                scr[cg, pl.ds(r, rows, stride=dilation), :] = (
                    ref[0, r, :, LANES * cg:LANES * (cg + 1)].astype(F32))
        return jnp.concatenate([scr[cg] for cg in range(ref.shape[3] // LANES)], axis=1)

    def spread(w):
        return jnp.dot(w.astype(BF16), spread_ref[...], preferred_element_type=F32)

    o0, l0 = token_order(o0_ref, None), token_order(l0_ref, None)
    o1, l1 = token_order(o1_ref, scratch[0]), token_order(l1_ref, scratch[1])
    o2, l2 = token_order(o2_ref, scratch[2]), token_order(l2_ref, scratch[3])
    mx = jnp.maximum(jnp.maximum(l0, l1), l2)
    e0, e1, e2 = jnp.exp(l0 - mx), jnp.exp(l1 - mx), jnp.exp(l2 - mx)
    inv = 1.0 / (e0 + e1 + e2)
    o = spread(e0 * inv) * o0 + spread(e1 * inv) * o1 + spread(e2 * inv) * o2
    y_attn = _dot(o, woa_ref[...])

    mixed = _dot(jax.nn.sigmoid(za_ref[...].astype(F32)) * y_rwkv
                 + jax.nn.sigmoid(zb_ref[...].astype(F32)) * y_attn,
                 wout_ref[...])
    out_ref[...] = x_ref[...] + gt_ref[0] * (_rms(mixed) * gpost_ref[...])


def _mix(y, bonus, g, proj, x2, o_l, gt, ln_w, ln_b, g_post, w_or, w_oa, w_out, e_seg, seq, tm):
    n_tok = x2.shape[0]
    tpb = seq // tm
    row = lambda i: (i, 0)
    const = lambda i: (0, 0)
    tok = pl.BlockSpec((tm, D_MODEL), row)
    vec = pl.BlockSpec((1, D_MODEL), const)

    def att(dilation, width):
        return pl.BlockSpec((1, dilation, tm // dilation, width), lambda i: (i // tpb, 0, i % tpb, 0))

    (o0, l0), (o1, l1), (o2, l2) = o_l
    d0, d1, d2 = [d for _, d in ATTN_GROUPS]
    assert d0 == 1
    gw = ATTN_GROUP_WIDTH
    head_of_lane = jnp.arange(gw) // ATTN_HEAD
    spread = (jnp.arange(LANES)[:, None] == head_of_lane[None, :]).astype(BF16)
    return pl.pallas_call(
        _mix_kernel,
        grid=(n_tok // tm,),
        in_specs=[tok, tok, tok,
                  pl.BlockSpec((tm, D_MODEL), lambda i: (i, COL_ZA // D_MODEL)),
                  pl.BlockSpec((tm, D_MODEL), lambda i: (i, COL_ZB // D_MODEL)),
                  tok, att(d0, gw), att(d1, gw), att(d2, gw), att(d0, LANES), att(d1, LANES), att(d2, LANES),
                  pl.BlockSpec((1, 1, D_MODEL), lambda i: (i // tpb, 0, 0)),
                  vec, vec, vec,
                  pl.BlockSpec((D_MODEL, D_MODEL), const, pipeline_mode=pl.Buffered(1)),
                  pl.BlockSpec((gw, D_MODEL), const, pipeline_mode=pl.Buffered(1)),
                  pl.BlockSpec((D_MODEL, D_MODEL), const, pipeline_mode=pl.Buffered(1)),
                  pl.BlockSpec((SEG_TILE, SEG_TILE), const),
                  pl.BlockSpec((LANES, gw), const)],
        out_specs=tok,
        out_shape=jax.ShapeDtypeStruct((n_tok, D_MODEL), F32),
        scratch_shapes=[pltpu.VMEM((gw // LANES, tm, LANES), F32), pltpu.VMEM((1, tm, LANES), F32)] * 2,
        compiler_params=_params("arbitrary"),
        name="mix_out",
    )(y, bonus, g, proj, proj, x2, o0, o1, o2, l0, l1, l2, gt, ln_w, ln_b, g_post,
      w_or, w_oa, w_out, e_seg, spread)


def _ffn_kernel(x_ref, sc_ref, sh_ref, gt_ref, gpre_ref, gpost_ref, wg_ref, wu_ref, wo_ref, out_ref):
    x = x_ref[...]
    h = (_rms(x) * gpre_ref[...] * (1.0 + sc_ref[0]) + sh_ref[0]).astype(BF16)
    acts = []
    for t in range(FFN_HIDDEN // SEG_TILE):
        sl = slice(SEG_TILE * t, SEG_TILE * (t + 1))
        u_gate = jnp.dot(h, wg_ref[:, sl], preferred_element_type=F32)
        u_up = jnp.dot(h, wu_ref[:, sl], preferred_element_type=F32)
        acts.append((u_gate * jax.nn.sigmoid(u_gate) * u_up).astype(BF16))
    y = jnp.dot(jnp.concatenate(acts, axis=1), wo_ref[...], preferred_element_type=F32)
    out_ref[...] = x + gt_ref[0] * (_rms(y) * gpost_ref[...])


def _ffn(x1, sc, sh, gt, g_pre, g_post, w_in, w_out, seq, tm):
    n_tok = x1.shape[0]
    tpb = seq // tm
    row = lambda i: (i, 0)
    const = lambda i: (0, 0)
    modrow = lambda i: (i // tpb, 0, 0)
    tok = pl.BlockSpec((tm, D_MODEL), row)
    mod = pl.BlockSpec((1, 1, D_MODEL), modrow)
    vec = pl.BlockSpec((1, D_MODEL), const)
    resident = pl.Buffered(1)
    return pl.pallas_call(
        _ffn_kernel,
        grid=(n_tok // tm,),
        in_specs=[tok, mod, mod, mod, vec, vec,
                  pl.BlockSpec((D_MODEL, FFN_HIDDEN), lambda i: (0, 0), pipeline_mode=resident),
                  pl.BlockSpec((D_MODEL, FFN_HIDDEN), lambda i: (0, 1), pipeline_mode=resident),
                  pl.BlockSpec((FFN_HIDDEN, D_MODEL), const, pipeline_mode=resident)],
        out_specs=tok,
        out_shape=jax.ShapeDtypeStruct((n_tok, D_MODEL), F32),
        compiler_params=_params("arbitrary"),
        name="swiglu_ffn",
    )(x1, sc, sh, gt, g_pre, g_post, w_in, w_in, w_out)


def _pad_cols(w, n):
    return jnp.pad(w, ((0, 0), (0, n - w.shape[1]))).astype(BF16)


def _pad_rows(w, n):
    return jnp.pad(w, ((0, n - w.shape[0]), (0, 0))).astype(BF16)


def kernel(x, c, w_mod, b_mod, g_pre_mix, g_post_mix, g_pre_ffn, g_post_ffn, w_in, mu_rkv, mu_lora,
           w0, w1, w2, a0, a1, a2, g1, g2, k_k, k_a, r_k, ln_x_w, ln_x_b, w_o_rwkv, w_o_attn, w_out,
           w_ffn_in, w_ffn_out):
    bsz, seq, _ = x.shape
    depth = w_mod.shape[0]
    assert seq % (2 * ATTN_BLK * ATTN_GROUPS[-1][1]) == 0
    n_tok = bsz * seq
    tm_proj = min(512, seq)
    tm_aproj = min(1024, seq)
    tm_prep = min(512, seq)
    tm_mix = min(512, seq)
    tm_ffn = min(512, seq)

    slopes = _alibi_slopes(len(ATTN_GROUPS) * ATTN_HEADS_PER_GROUP)
    seg = jnp.arange(SEG_TILE) // RWKV_HEAD
    e_seg = (seg[:, None] == seg[None, :]).astype(BF16)
    row = lambda t: t.reshape(1, -1)

    x2 = x.reshape(n_tok, D_MODEL)
    for l in range(depth):
        mod = _mod(c, w_mod[l], b_mod[l])
        sh_m, sc_m, gt_m, sh_f, sc_f, gt_f = [mod[:, i * D_MODEL:(i + 1) * D_MODEL].reshape(bsz, 1, D_MODEL)
                                               for i in range(6)]
        wl = w_in[l]
        rkv_cols = 3 * D_MODEL
        att_cols = 3 * ATTN_WIDTH
        w_main = jnp.concatenate([wl[:, :rkv_cols], wl[:, rkv_cols + att_cols:]], axis=1).astype(BF16)

        proj, h, lw, la, lg = _inproj(
            x2, sc_m, sh_m, row(g_pre_mix[l]), mu_lora[l],
            _pad_cols(w1[l], LORA_DECAY_PAD), _pad_cols(a1[l], LORA_ICLR_PAD),
            _pad_cols(g1[l], LORA_GATE_PAD), w_main, seq, tm_proj)

        r, lwd, k2, v, kk, kka, g, bonus = _prep(
            proj, lw, la, lg, mu_rkv[l], row(w0[l]), row(a0[l]), row(k_k[l]), row(k_a[l]), row(r_k[l]),
            _pad_rows(w2[l], LORA_DECAY_PAD), _pad_rows(a2[l], LORA_ICLR_PAD),
            _pad_rows(g2[l], LORA_GATE_PAD), e_seg, seq, tm_prep)

        b3 = lambda t: t.reshape(bsz, seq, D_MODEL)
        y = _chunk(b3(r), b3(lwd), b3(k2), b3(v), b3(kk), b3(kka)).reshape(n_tok, D_MODEL)

        def group_cols(which, gi):
            lo = rkv_cols + which * ATTN_WIDTH + gi * ATTN_GROUP_WIDTH
            return wl[:, lo:lo + ATTN_GROUP_WIDTH]

        w_kvq = jnp.stack([jnp.concatenate([group_cols(1, gi), group_cols(2, gi), group_cols(0, gi)], axis=1)
                           for gi in range(len(ATTN_GROUPS))]).astype(BF16)
        kvq = _attn_proj(h, w_kvq, bsz, seq, tm_aproj)
        o_l = []
        for gi, (window, dilation) in enumerate(ATTN_GROUPS):
            assert window // dilation == ATTN_BLK
            o_l.append(_attn_group(
                kvq[gi], dilation, slopes[gi * ATTN_HEADS_PER_GROUP:(gi + 1) * ATTN_HEADS_PER_GROUP]))

        x2 = _mix(y, bonus, g, proj, x2, o_l, gt_m, row(ln_x_w[l]), row(ln_x_b[l]), row(g_post_mix[l]),
                  w_o_rwkv[l].astype(BF16), w_o_attn[l].astype(BF16), w_out[l].astype(BF16), e_seg,
                  seq, tm_mix)

        x2 = _ffn(x2, sc_f, sh_f, gt_f, row(g_pre_ffn[l]), row(g_post_ffn[l]),
                  w_ffn_in[l].astype(BF16), w_ffn_out[l].astype(BF16), seq, tm_ffn)
    return x2.reshape(bsz, seq, D_MODEL)
```

```python
import functools
import math

import jax
import jax.numpy as jnp
from jax import lax
from jax.experimental import pallas as pl
from jax.experimental.pallas import tpu as pltpu

F32 = jnp.float32
BF16 = jnp.bfloat16
HIGHEST = lax.Precision.HIGHEST

LANES = 128
SEG_TILE = 256
D_MODEL = 1024
RWKV_HEAD = 64
GN_EPS = 64e-5
RMS_EPS = 1e-6
ATTN_HEAD = 64
ATTN_HEADS_PER_GROUP = 8
ATTN_GROUPS = ((128, 1), (512, 4), (2048, 16))
ATTN_GROUP_WIDTH = ATTN_HEADS_PER_GROUP * ATTN_HEAD
ATTN_WIDTH = len(ATTN_GROUPS) * ATTN_GROUP_WIDTH
ATTN_BLK = 128
FFN_HIDDEN = 2816
LORA_DECAY_PAD = 128
LORA_ICLR_PAD = 128
LORA_GATE_PAD = 256
CHUNK = 64
CHUNKS_PER_STEP = 4
PAIR = 2 * RWKV_HEAD
N_PAIRS = D_MODEL // PAIR

COL_R, COL_K, COL_V, COL_ZA, COL_ZB = 0, 1024, 2048, 3072, 4096
MAIN_COLS = 5120
PROJ_TILE = 5120

VMEM_LIMIT = 48 * 1024 * 1024


def _alibi_slopes(n):
    def pow2(m):
        start = 2.0 ** (-8.0 / m)
        return [start ** (i + 1) for i in range(m)]
    if math.log2(n).is_integer():
        s = pow2(n)
    else:
        p = 2 ** int(math.floor(math.log2(n)))
        s = pow2(p) + pow2(2 * p)[0::2][: n - p]
    return sorted(s, reverse=True)


def _dot(a, b):
    return jnp.dot(a.astype(BF16), b.astype(BF16), preferred_element_type=F32)


def _dot_nt(a, b):
    return lax.dot_general(a.astype(BF16), b.astype(BF16), (((1,), (1,)), ((), ())),
                           preferred_element_type=F32)


def _dot_tn(a, b):
    return lax.dot_general(a.astype(BF16), b.astype(BF16), (((0,), (0,)), ((), ())),
                           preferred_element_type=F32)


def _seg_sum(x, e, pieces=1):
    parts = [x.astype(BF16)]
    if pieces == 2:
        parts.append((x - parts[0].astype(F32)).astype(BF16))
    outs = []
    for t in range(x.shape[1] // SEG_TILE):
        sl = slice(SEG_TILE * t, SEG_TILE * (t + 1))
        outs.append(sum(jnp.dot(part[:, sl], e, preferred_element_type=F32) for part in parts))
    return jnp.concatenate(outs, axis=1)


def _shift_rows(t, prev_row):
    rows = lax.broadcasted_iota(jnp.int32, t.shape, 0)
    return jnp.where(rows == 0, prev_row, pltpu.roll(t, 1, 0))


def _rms(t):
    return t * lax.rsqrt(jnp.mean(t * t, axis=-1, keepdims=True) + RMS_EPS)


def _params(*semantics):
    return pltpu.CompilerParams(dimension_semantics=semantics, vmem_limit_bytes=VMEM_LIMIT)


def _mod_kernel(c_ref, w_ref, b_ref, o_ref):
    o_ref[...] = jnp.dot(c_ref[...], w_ref[...], preferred_element_type=F32,
                         precision=HIGHEST) + b_ref[...]


def _mod(c, w_mod, b_mod):
    bsz = c.shape[0]
    n_out = w_mod.shape[1]
    return pl.pallas_call(
        _mod_kernel,
        grid=(n_out // D_MODEL,),
        in_specs=[pl.BlockSpec((bsz, D_MODEL), lambda j: (0, 0)),
                  pl.BlockSpec((D_MODEL, D_MODEL), lambda j: (0, j)),
                  pl.BlockSpec((1, D_MODEL), lambda j: (0, j))],
        out_specs=pl.BlockSpec((bsz, D_MODEL), lambda j: (0, j)),
        out_shape=jax.ShapeDtypeStruct((bsz, n_out), F32),
        compiler_params=_params("arbitrary"),
        name="adaln_mod",
    )(c, w_mod, b_mod.reshape(1, n_out))


def _inproj_kernel(x_ref, sc_ref, sh_ref, g_ref, mu_ref, w1_ref, a1_ref, g1_ref, win_ref,
                   p_ref, h_ref, lw_ref, la_ref, lg_ref, last_scr, *, tiles_per_batch):
    i = pl.program_id(0)
    j = pl.program_id(1)
    tm = x_ref.shape[0]

    @pl.when(j == 0)
    def _():
        @pl.when(i % tiles_per_batch == 0)
        def _():
            last_scr[...] = jnp.zeros_like(last_scr)

        h = _rms(x_ref[...]) * g_ref[...] * (1.0 + sc_ref[0]) + sh_ref[0]
        hs = _shift_rows(h, last_scr[...])
        last_scr[...] = h[tm - 1:tm, :]
        dh = hs - h
        lw_ref[...] = _dot(h + dh * mu_ref[0:1, :], w1_ref[...])
        la_ref[...] = _dot(h + dh * mu_ref[1:2, :], a1_ref[...])
        lg_ref[...] = _dot(h + dh * mu_ref[2:3, :], g1_ref[...])
        h_ref[...] = h.astype(BF16)

    p_ref[...] = jnp.dot(h_ref[...], win_ref[...], preferred_element_type=F32).astype(BF16)


def _inproj(x2, sc, sh, g_pre, mu_lora, w1p, a1p, g1p, w_main, seq, tm):
    n_tok = x2.shape[0]
    tpb = seq // tm
    row = lambda i, j: (i, 0)
    const = lambda i, j: (0, 0)
    modrow = lambda i, j: (i // tpb, 0, 0)
    return pl.pallas_call(
        functools.partial(_inproj_kernel, tiles_per_batch=tpb),
        grid=(n_tok // tm, MAIN_COLS // PROJ_TILE),
        in_specs=[pl.BlockSpec((tm, D_MODEL), row),
                  pl.BlockSpec((1, 1, D_MODEL), modrow),
                  pl.BlockSpec((1, 1, D_MODEL), modrow),
                  pl.BlockSpec((1, D_MODEL), const),
                  pl.BlockSpec((3, D_MODEL), const),
                  pl.BlockSpec((D_MODEL, LORA_DECAY_PAD), const),
                  pl.BlockSpec((D_MODEL, LORA_ICLR_PAD), const),
                  pl.BlockSpec((D_MODEL, LORA_GATE_PAD), const),
                  pl.BlockSpec((D_MODEL, PROJ_TILE), lambda i, j: (0, j),
                               pipeline_mode=pl.Buffered(1 if PROJ_TILE == MAIN_COLS else 2))],
        out_specs=[pl.BlockSpec((tm, PROJ_TILE), lambda i, j: (i, j)),
                   pl.BlockSpec((tm, D_MODEL), row),
                   pl.BlockSpec((tm, LORA_DECAY_PAD), row),
                   pl.BlockSpec((tm, LORA_ICLR_PAD), row),
                   pl.BlockSpec((tm, LORA_GATE_PAD), row)],
        out_shape=[jax.ShapeDtypeStruct((n_tok, MAIN_COLS), BF16),
                   jax.ShapeDtypeStruct((n_tok, D_MODEL), BF16),
                   jax.ShapeDtypeStruct((n_tok, LORA_DECAY_PAD), F32),
                   jax.ShapeDtypeStruct((n_tok, LORA_ICLR_PAD), F32),
                   jax.ShapeDtypeStruct((n_tok, LORA_GATE_PAD), F32)],
        scratch_shapes=[pltpu.VMEM((1, D_MODEL), F32)],
        compiler_params=_params("arbitrary", "arbitrary"),
        name="in_proj",
    )(x2, sc, sh, g_pre, mu_lora, w1p, a1p, g1p, w_main)


def _attn_proj_kernel(h_ref, w_ref, *refs):
    n_groups = len(ATTN_GROUPS)
    out_refs, scratch = refs[:n_groups], list(refs[n_groups:])
    h = h_ref[...]
    for gi, (_, dilation) in enumerate(ATTN_GROUPS):
        o_ref = out_refs[gi]
        acc = jnp.dot(h, w_ref[gi], preferred_element_type=F32)
        if dilation == 1:
            o_ref[0, 0] = acc.astype(BF16)
            continue
        scr = scratch.pop(0)
        rows = acc.shape[0] // dilation
        for cg in range(scr.shape[0]):
            scr[cg] = acc[:, LANES * cg:LANES * (cg + 1)]
        for r in range(dilation):
            for cg in range(scr.shape[0]):
                o_ref[0, r, :, LANES * cg:LANES * (cg + 1)] = (
                    scr[cg, pl.ds(r, rows, stride=dilation), :].astype(BF16))


def _attn_proj(h, w_kvq, bsz, seq, tm):
    tpb = seq // tm
    width = 3 * ATTN_GROUP_WIDTH
    dilations = [d for _, d in ATTN_GROUPS]
    return pl.pallas_call(
        _attn_proj_kernel,
        grid=(bsz * tpb,),
        in_specs=[pl.BlockSpec((tm, D_MODEL), lambda i: (i, 0)),
                  pl.BlockSpec((len(dilations), D_MODEL, width), lambda i: (0, 0, 0),
                               pipeline_mode=pl.Buffered(1))],
        out_specs=[pl.BlockSpec((1, d, tm // d, width), lambda i: (i // tpb, 0, i % tpb, 0))
                   for d in dilations],
        out_shape=[jax.ShapeDtypeStruct((bsz, d, seq // d, width), BF16) for d in dilations],
        scratch_shapes=[pltpu.VMEM((width // LANES, tm, LANES), F32) for d in dilations if d != 1],
        compiler_params=_params("arbitrary"),
        name="attn_proj",
    )(h, w_kvq)


def _prep_kernel(pr_ref, pk_ref, pv_ref, lw_ref, la_ref, lg_ref, mu_ref, w0_ref, a0_ref, kk_ref,
                 ka_ref, rk_ref, w2_ref, a2_ref, g2_ref, e_ref,
                 r_o, lw_o, k_o, v_o, kk_o, kka_o, g_o, bonus_o, last_scr, *, tiles_per_batch):
    i = pl.program_id(0)
    tm = pr_ref.shape[0]

    @pl.when(i % tiles_per_batch == 0)
    def _():
        last_scr[...] = jnp.zeros_like(last_scr)

    def mix(p_ref, idx):
        p = p_ref[...].astype(F32)
        ps = _shift_rows(p, last_scr[idx:idx + 1, :])
        last_scr[idx:idx + 1, :] = p[tm - 1:tm, :]
        return p + (ps - p) * mu_ref[idx:idx + 1, :]

    r = mix(pr_ref, 0)
    k = mix(pk_ref, 1)
    v = mix(pv_ref, 2)
    e = e_ref[...]

    z = w0_ref[...] + _dot(jnp.tanh(lw_ref[...]), w2_ref[...])
    lw_o[...] = -math.exp(-0.5) * jax.nn.sigmoid(z)
    a = jax.nn.sigmoid(a0_ref[...] + _dot(la_ref[...], a2_ref[...]))
    g_o[...] = _dot(jax.nn.sigmoid(lg_ref[...]), g2_ref[...]).astype(BF16)

    kk = k * kk_ref[...]
    kk = kk * lax.rsqrt(jnp.maximum(_seg_sum(kk * kk, e), 1e-24))
    k2 = k * (1.0 + (a - 1.0) * ka_ref[...])
    r_o[...] = r.astype(BF16)
    k_o[...] = k2.astype(BF16)
    v_o[...] = v.astype(BF16)
    kk_o[...] = kk.astype(BF16)
    kka_o[...] = (kk * a).astype(BF16)
    bonus_o[...] = (_seg_sum(r * k2 * rk_ref[...], e) * v).astype(BF16)


def _prep(proj, lw, la, lg, mu_rkv, w0, a0, k_k, k_a, r_k, w2p, a2p, g2p, e_seg, seq, tm):
    n_tok = proj.shape[0]
    tpb = seq // tm
    row = lambda i: (i, 0)
    const = lambda i: (0, 0)
    vec = pl.BlockSpec((1, D_MODEL), const)
    tok = pl.BlockSpec((tm, D_MODEL), row)
    return pl.pallas_call(
        functools.partial(_prep_kernel, tiles_per_batch=tpb),
        grid=(n_tok // tm,),
        in_specs=[pl.BlockSpec((tm, D_MODEL), lambda i: (i, COL_R // D_MODEL)),
                  pl.BlockSpec((tm, D_MODEL), lambda i: (i, COL_K // D_MODEL)),
                  pl.BlockSpec((tm, D_MODEL), lambda i: (i, COL_V // D_MODEL)),
                  pl.BlockSpec((tm, LORA_DECAY_PAD), row),
                  pl.BlockSpec((tm, LORA_ICLR_PAD), row),
                  pl.BlockSpec((tm, LORA_GATE_PAD), row),
                  pl.BlockSpec((3, D_MODEL), const),
                  vec, vec, vec, vec, vec,
                  pl.BlockSpec((LORA_DECAY_PAD, D_MODEL), const),
                  pl.BlockSpec((LORA_ICLR_PAD, D_MODEL), const),
                  pl.BlockSpec((LORA_GATE_PAD, D_MODEL), const),
                  pl.BlockSpec((SEG_TILE, SEG_TILE), const)],
        out_specs=[tok] * 8,
        out_shape=[jax.ShapeDtypeStruct((n_tok, D_MODEL), F32 if name == "lw" else BF16)
                   for name in ("r", "lw", "k", "v", "kk", "kka", "g", "bonus")],
        scratch_shapes=[pltpu.VMEM((3, D_MODEL), F32)],
        compiler_params=_params("arbitrary"),
        name="rwkv_prep",
    )(proj, proj, proj, lw, la, lg, mu_rkv, w0, a0, k_k, k_a, r_k, w2p, a2p, g2p, e_seg)


def _chunk_kernel(r_ref, lw_ref, k_ref, v_ref, kk_ref, kka_ref, y_ref, s_scr):
    @pl.when(pl.program_id(1) == 0)
    def _():
        s_scr[...] = jnp.zeros_like(s_scr)

    c = CHUNK
    rows = lw_ref.shape[1]
    n_chunks = rows // c
    lw = lw_ref[0]
    ti = lax.broadcasted_iota(jnp.int32, (rows, rows), 0)
    tj = lax.broadcasted_iota(jnp.int32, (rows, rows), 1)
    tril = ((ti >= tj) & (ti // c == tj // c)).astype(BF16)
    lw_1 = lw.astype(BF16)
    lw_2 = (lw - lw_1.astype(F32)).astype(BF16)
    lw_3 = (lw - lw_1.astype(F32) - lw_2.astype(F32)).astype(BF16)
    cum = sum(jnp.dot(tril, piece, preferred_element_type=F32) for piece in (lw_1, lw_2, lw_3))
    cum_prev = cum - lw
    cum_end = jnp.concatenate(
        [jnp.broadcast_to(cum[c * (ck + 1) - 1:c * (ck + 1), :], (c, D_MODEL)) for ck in range(n_chunks)], axis=0)
    kk = kk_ref[0].astype(F32)
    kka = kka_ref[0].astype(F32)
    k = k_ref[0].astype(F32)
    e_neg = jnp.exp(-cum)
    e_end = jnp.exp(cum_end - cum)
    at_all = -kk * jnp.exp(cum_prev)
    rt_all = r_ref[0].astype(F32) * jnp.exp(cum)
    bt_all = kka * e_neg
    kt_all = k * e_neg
    bh_all = kka * e_end
    kh_all = k * e_end
    p_end = jnp.exp(cum_end)
    v_all = v_ref[0].astype(F32)

    ri = lax.broadcasted_iota(jnp.int32, (PAIR, PAIR), 0)
    ci = lax.broadcasted_iota(jnp.int32, (PAIR, PAIR), 1)
    same = (ri // RWKV_HEAD) == (ci // RWKV_HEAD)
    same2 = jnp.concatenate([same, same], axis=1)
    eye = ri == ci
    tt = lax.broadcasted_iota(jnp.int32, (c, PAIR), 0)
    ss = lax.broadcasted_iota(jnp.int32, (c, PAIR), 1) % c
    m_strict = tt > ss
    m_incl = tt >= ss
    eye_rows = (tt == ss).astype(F32)
    zeros_c = jnp.zeros((c, PAIR), F32)

    def bd(t):
        mask = same if t.shape[1] == PAIR else same2
        return jnp.where(mask, jnp.concatenate([t, t], axis=0), 0.0).astype(BF16)

    items = [(slice(c * ck, c * (ck + 1)), slice(PAIR * p, PAIR * (p + 1)))
             for ck in range(n_chunks) for p in range(N_PAIRS)]
    pairs = range(len(items))
    at = [at_all[rs, sl] for rs, sl in items]
    rt = [rt_all[rs, sl] for rs, sl in items]
    v_bd = [bd(v_all[rs, sl]) for rs, sl in items]

    quad = [_dot_nt(jnp.concatenate([at[p], rt[p]], axis=0),
                    jnp.concatenate([bd(bt_all[items[p]]), bd(kt_all[items[p]])], axis=0))
            for p in pairs]
    a_ab = [jnp.where(m_strict, q[:c, :PAIR], 0.0) for q in quad]
    a_ak = [jnp.where(m_strict, q[:c, PAIR:], 0.0) for q in quad]
    a_rb = [jnp.where(m_incl, q[c:, :PAIR], 0.0) for q in quad]
    a_rk = [jnp.where(m_incl, q[c:, PAIR:], 0.0) for q in quad]
    av = [_dot(jnp.concatenate([a_ak[p], a_rk[p]], axis=0), v_bd[p]) for p in pairs]
    x0 = [jnp.concatenate([at[p], av[p][:c]], axis=1) for p in pairs]

    t_inv = [eye_rows + a_ab[p] for p in pairs]
    pw = [_dot(a_ab[p], bd(a_ab[p])) for p in pairs]
    for _ in range(int(math.log2(c)) - 2):
        both = [_dot(jnp.concatenate([t_inv[p], pw[p]], axis=0), bd(pw[p])) for p in pairs]
        t_inv = [t_inv[p] + both[p][:c] for p in pairs]
        pw = [both[p][c:] for p in pairs]
    t_inv = [t_inv[p] + _dot(t_inv[p], bd(pw[p])) for p in pairs]
    x = [_dot(t_inv[p], bd(x0[p])) for p in pairs]

    top = [_dot(a_rb[p], bd(x[p])) for p in pairs]
    bot = [_dot_tn(jnp.concatenate([bh_all[items[p]], kh_all[items[p]]], axis=0),
                   jnp.concatenate([x[p], jnp.concatenate([zeros_c, v_all[items[p]]], axis=1)], axis=0))
           for p in pairs]

    state = [s_scr[hp] for hp in range(N_PAIRS)]
    for p in pairs:
        rs, sl = items[p]
        hp = p % N_PAIRS
        p_end_row = p_end[rs, sl][c - 1:c, :]
        m_mat = jnp.where(same, bot[p][:, :PAIR], 0.0) + jnp.where(
            eye, jnp.broadcast_to(p_end_row, (PAIR, PAIR)), 0.0)
        lhs = jnp.concatenate([rt[p] + top[p][:, :PAIR], m_mat], axis=0)
        fin = _dot(lhs, state[hp])
        y_ref[0, rs, sl] = fin[:c] + top[p][:, PAIR:] + av[p][c:]
        state[hp] = fin[c:] + jnp.where(same, bot[p][:, PAIR:], 0.0)
    for hp in range(N_PAIRS):
        s_scr[hp] = state[hp]


def _chunk(r, lw, k, v, kk, kka):
    bsz, seq, _ = r.shape
    spec = pl.BlockSpec((1, CHUNKS_PER_STEP * CHUNK, D_MODEL), lambda b, c: (b, c, 0))
    return pl.pallas_call(
        _chunk_kernel,
        grid=(bsz, seq // (CHUNKS_PER_STEP * CHUNK)),
        in_specs=[spec] * 6,
        out_specs=spec,
        out_shape=jax.ShapeDtypeStruct((bsz, seq, D_MODEL), F32),
        scratch_shapes=[pltpu.VMEM((N_PAIRS, PAIR, PAIR), F32)],
        compiler_params=_params("arbitrary", "arbitrary"),
        name="rwkv_chunk",
    )(r, lw, k, v, kk, kka)


def _attn_kernel(cur_ref, prev_ref, o_ref, l_ref, *, dilation, slopes):
    n = pl.program_id(2)
    blk = ATTN_BLK
    gw = ATTN_GROUP_WIDTH
    ii = lax.broadcasted_iota(jnp.int32, (blk, 2 * blk), 0)
    jj = lax.broadcasted_iota(jnp.int32, (blk, 2 * blk), 1)
    delta = ii - jj + blk
    in_window = (delta >= 0) & (delta <= blk)
    dist = delta.astype(F32) * float(dilation)
    lane_head = lax.broadcasted_iota(jnp.int32, (blk, PAIR), 1) // ATTN_HEAD
    lane = lax.broadcasted_iota(jnp.int32, (blk, LANES), 1)
    neg_inf = float("-inf")
    pairs = range(gw // PAIR)
    heads = [(pp, hh) for pp in pairs for hh in range(2)]
    sls = [slice(PAIR * pp, PAIR * (pp + 1)) for pp in pairs]

    for sb in range(2):
        rows = slice(blk * sb, blk * (sb + 1))
        if sb == 0:
            valid = in_window & ((n > 0) | (jj >= blk))
            kcat = [jnp.concatenate([prev_ref[0, 0, :, sl], cur_ref[0, 0, rows, sl]], axis=0) for sl in sls]
            vcat = [jnp.concatenate([prev_ref[0, 0, :, gw + PAIR * pp:gw + PAIR * (pp + 1)],
                                     cur_ref[0, 0, rows, gw + PAIR * pp:gw + PAIR * (pp + 1)]], axis=0)
                    for pp in pairs]
        else:
            valid = in_window
            kcat = [cur_ref[0, 0, :, sl] for sl in sls]
            vcat = [cur_ref[0, 0, :, gw + PAIR * pp:gw + PAIR * (pp + 1)] for pp in pairs]
        q2 = [cur_ref[0, 0, rows, 2 * gw + PAIR * pp:2 * gw + PAIR * (pp + 1)] * (ATTN_HEAD ** -0.5)
              for pp in pairs]
        s = [jnp.where(valid,
                       _dot_nt(jnp.where(lane_head == hh, q2[pp], jnp.zeros_like(q2[pp])), kcat[pp])
                       - float(slopes[2 * pp + hh]) * dist, neg_inf) for pp, hh in heads]
        m = [jnp.max(t, axis=-1, keepdims=True) for t in s]
        p = [jnp.exp(s[i] - m[i]) for i in range(len(heads))]
        den = [jnp.sum(t, axis=-1, keepdims=True) for t in p]
        o = [_dot(p[i], vcat[pp]) * (1.0 / den[i]) for i, (pp, _) in enumerate(heads)]
        for pp in pairs:
            o_ref[0, 0, rows, sls[pp]] = jnp.where(lane_head == 0, o[2 * pp], o[2 * pp + 1]).astype(BF16)
        lse = jnp.zeros((blk, LANES), F32)
        for i in range(len(heads)):
            lse = jnp.where(lane == i, m[i] + jnp.log(den[i]), lse)
        l_ref[0, 0, rows, :] = lse


def _attn_group(kvq, dilation, slopes):
    bsz, _, sub, _ = kvq.shape
    gw = ATTN_GROUP_WIDTH
    step = 2 * ATTN_BLK
    return pl.pallas_call(
        functools.partial(_attn_kernel, dilation=dilation, slopes=tuple(slopes)),
        grid=(bsz, dilation, sub // step),
        in_specs=[pl.BlockSpec((1, 1, step, 3 * gw), lambda b, r, n: (b, r, n, 0)),
                  pl.BlockSpec((1, 1, ATTN_BLK, 2 * gw), lambda b, r, n: (b, r, jnp.maximum(2 * n - 1, 0), 0))],
        out_specs=[pl.BlockSpec((1, 1, step, gw), lambda b, r, n: (b, r, n, 0)),
                   pl.BlockSpec((1, 1, step, LANES), lambda b, r, n: (b, r, n, 0))],
        out_shape=[jax.ShapeDtypeStruct((bsz, dilation, sub, gw), BF16),
                   jax.ShapeDtypeStruct((bsz, dilation, sub, LANES), F32)],
        compiler_params=_params("arbitrary", "arbitrary", "arbitrary"),
        name=f"dilated_attn_d{dilation}",
    )(kvq, kvq)


def _mix_kernel(y_ref, bonus_ref, g_ref, za_ref, zb_ref, x_ref, o0_ref, o1_ref, o2_ref,
                l0_ref, l1_ref, l2_ref, gt_ref, lnw_ref, lnb_ref, gpost_ref,
                wor_ref, woa_ref, wout_ref, e_ref, spread_ref, out_ref, *scratch):
    e = e_ref[...]
    inv_n = 1.0 / RWKV_HEAD
    y = y_ref[...]
    mean = _seg_sum(y, e, pieces=2) * inv_n
    yc = y - mean
    var = _seg_sum(yc * yc, e) * inv_n
    yn = yc * lax.rsqrt(var + GN_EPS) * lnw_ref[...] + lnb_ref[...]
    y_rwkv = _dot((yn + bonus_ref[...].astype(F32)) * g_ref[...].astype(F32), wor_ref[...])

    def token_order(ref, scr):
        dilation, rows = ref.shape[1], ref.shape[2]
        if dilation == 1:
            return ref[0, 0].astype(F32)
        for r in range(dilation):
            for cg in range(ref.shape[3] // LANES):
                scr[cg, pl.ds(r, rows, stride=dilation), :] = (
                    ref[0, r, :, LANES * cg:LANES * (cg + 1)].astype(F32))
        return jnp.concatenate([scr[cg] for cg in range(ref.shape[3] // LANES)], axis=1)

    def spread(w):
        return jnp.dot(w.astype(BF16), spread_ref[...], preferred_element_type=F32)

    o0, l0 = token_order(o0_ref, None), token_order(l0_ref, None)
    o1, l1 = token_order(o1_ref, scratch[0]), token_order(l1_ref, scratch[1])
    o2, l2 = token_order(o2_ref, scratch[2]), token_order(l2_ref, scratch[3])
    mx = jnp.maximum(jnp.maximum(l0, l1), l2)
    e0, e1, e2 = jnp.exp(l0 - mx), jnp.exp(l1 - mx), jnp.exp(l2 - mx)
    inv = 1.0 / (e0 + e1 + e2)
    o = spread(e0 * inv) * o0 + spread(e1 * inv) * o1 + spread(e2 * inv) * o2
    y_attn = _dot(o, woa_ref[...])

    mixed = _dot(jax.nn.sigmoid(za_ref[...].astype(F32)) * y_rwkv
                 + jax.nn.sigmoid(zb_ref[...].astype(F32)) * y_attn,
                 wout_ref[...])
    out_ref[...] = x_ref[...] + gt_ref[0] * (_rms(mixed) * gpost_ref[...])


def _mix(y, bonus, g, proj, x2, o_l, gt, ln_w, ln_b, g_post, w_or, w_oa, w_out, e_seg, seq, tm):
    n_tok = x2.shape[0]
    tpb = seq // tm
    row = lambda i: (i, 0)
    const = lambda i: (0, 0)
    tok = pl.BlockSpec((tm, D_MODEL), row)
    vec = pl.BlockSpec((1, D_MODEL), const)

    def att(dilation, width):
        return pl.BlockSpec((1, dilation, tm // dilation, width), lambda i: (i // tpb, 0, i % tpb, 0))

    (o0, l0), (o1, l1), (o2, l2) = o_l
    d0, d1, d2 = [d for _, d in ATTN_GROUPS]
    assert d0 == 1
    gw = ATTN_GROUP_WIDTH
    head_of_lane = jnp.arange(gw) // ATTN_HEAD
    spread = (jnp.arange(LANES)[:, None] == head_of_lane[None, :]).astype(BF16)
    return pl.pallas_call(
        _mix_kernel,
        grid=(n_tok // tm,),
        in_specs=[tok, tok, tok,
                  pl.BlockSpec((tm, D_MODEL), lambda i: (i, COL_ZA // D_MODEL)),
                  pl.BlockSpec((tm, D_MODEL), lambda i: (i, COL_ZB // D_MODEL)),
                  tok, att(d0, gw), att(d1, gw), att(d2, gw), att(d0, LANES), att(d1, LANES), att(d2, LANES),
                  pl.BlockSpec((1, 1, D_MODEL), lambda i: (i // tpb, 0, 0)),
                  vec, vec, vec,
                  pl.BlockSpec((D_MODEL, D_MODEL), const, pipeline_mode=pl.Buffered(1)),
                  pl.BlockSpec((gw, D_MODEL), const, pipeline_mode=pl.Buffered(1)),
                  pl.BlockSpec((D_MODEL, D_MODEL), const, pipeline_mode=pl.Buffered(1)),
                  pl.BlockSpec((SEG_TILE, SEG_TILE), const),
                  pl.BlockSpec((LANES, gw), const)],
        out_specs=tok,
        out_shape=jax.ShapeDtypeStruct((n_tok, D_MODEL), F32),
        scratch_shapes=[pltpu.VMEM((gw // LANES, tm, LANES), F32), pltpu.VMEM((1, tm, LANES), F32)] * 2,
        compiler_params=_params("arbitrary"),
        name="mix_out",
    )(y, bonus, g, proj, proj, x2, o0, o1, o2, l0, l1, l2, gt, ln_w, ln_b, g_post,
      w_or, w_oa, w_out, e_seg, spread)


def _ffn_kernel(x_ref, sc_ref, sh_ref, gt_ref, gpre_ref, gpost_ref, wg_ref, wu_ref, wo_ref, out_ref):
    x = x_ref[...]
    h = (_rms(x) * gpre_ref[...] * (1.0 + sc_ref[0]) + sh_ref[0]).astype(BF16)
    acts = []
    for t in range(FFN_HIDDEN // SEG_TILE):
        sl = slice(SEG_TILE * t, SEG_TILE * (t + 1))
        u_gate = jnp.dot(h, wg_ref[:, sl], preferred_element_type=F32)
        u_up = jnp.dot(h, wu_ref[:, sl], preferred_element_type=F32)
        acts.append((u_gate * jax.nn.sigmoid(u_gate) * u_up).astype(BF16))
    y = jnp.dot(jnp.concatenate(acts, axis=1), wo_ref[...], preferred_element_type=F32)
    out_ref[...] = x + gt_ref[0] * (_rms(y) * gpost_ref[...])


def _ffn(x1, sc, sh, gt, g_pre, g_post, w_in, w_out, seq, tm):
    n_tok = x1.shape[0]
    tpb = seq // tm
    row = lambda i: (i, 0)
    const = lambda i: (0, 0)
    modrow = lambda i: (i // tpb, 0, 0)
    tok = pl.BlockSpec((tm, D_MODEL), row)
    mod = pl.BlockSpec((1, 1, D_MODEL), modrow)
    vec = pl.BlockSpec((1, D_MODEL), const)
    resident = pl.Buffered(1)
    return pl.pallas_call(
        _ffn_kernel,
        grid=(n_tok // tm,),
        in_specs=[tok, mod, mod, mod, vec, vec,
                  pl.BlockSpec((D_MODEL, FFN_HIDDEN), lambda i: (0, 0), pipeline_mode=resident),
                  pl.BlockSpec((D_MODEL, FFN_HIDDEN), lambda i: (0, 1), pipeline_mode=resident),
                  pl.BlockSpec((FFN_HIDDEN, D_MODEL), const, pipeline_mode=resident)],
        out_specs=tok,
        out_shape=jax.ShapeDtypeStruct((n_tok, D_MODEL), F32),
        compiler_params=_params("arbitrary"),
        name="swiglu_ffn",
    )(x1, sc, sh, gt, g_pre, g_post, w_in, w_in, w_out)


def _pad_cols(w, n):
    return jnp.pad(w, ((0, 0), (0, n - w.shape[1]))).astype(BF16)


def _pad_rows(w, n):
    return jnp.pad(w, ((0, n - w.shape[0]), (0, 0))).astype(BF16)


def kernel(x, c, w_mod, b_mod, g_pre_mix, g_post_mix, g_pre_ffn, g_post_ffn, w_in, mu_rkv, mu_lora,
           w0, w1, w2, a0, a1, a2, g1, g2, k_k, k_a, r_k, ln_x_w, ln_x_b, w_o_rwkv, w_o_attn, w_out,
           w_ffn_in, w_ffn_out):
    bsz, seq, _ = x.shape
    depth = w_mod.shape[0]
    assert seq % (2 * ATTN_BLK * ATTN_GROUPS[-1][1]) == 0
    n_tok = bsz * seq
    tm_proj = min(512, seq)
    tm_aproj = min(1024, seq)
    tm_prep = min(512, seq)
    tm_mix = min(512, seq)
    tm_ffn = min(512, seq)

    slopes = _alibi_slopes(len(ATTN_GROUPS) * ATTN_HEADS_PER_GROUP)
    seg = jnp.arange(SEG_TILE) // RWKV_HEAD
    e_seg = (seg[:, None] == seg[None, :]).astype(BF16)
    row = lambda t: t.reshape(1, -1)

    x2 = x.reshape(n_tok, D_MODEL)
    for l in range(depth):
        mod = _mod(c, w_mod[l], b_mod[l])
        sh_m, sc_m, gt_m, sh_f, sc_f, gt_f = [mod[:, i * D_MODEL:(i + 1) * D_MODEL].reshape(bsz, 1, D_MODEL)
                                               for i in range(6)]
        wl = w_in[l]
        rkv_cols = 3 * D_MODEL
        att_cols = 3 * ATTN_WIDTH
        w_main = jnp.concatenate([wl[:, :rkv_cols], wl[:, rkv_cols + att_cols:]], axis=1).astype(BF16)

        proj, h, lw, la, lg = _inproj(
            x2, sc_m, sh_m, row(g_pre_mix[l]), mu_lora[l],
            _pad_cols(w1[l], LORA_DECAY_PAD), _pad_cols(a1[l], LORA_ICLR_PAD),
            _pad_cols(g1[l], LORA_GATE_PAD), w_main, seq, tm_proj)

        r, lwd, k2, v, kk, kka, g, bonus = _prep(
            proj, lw, la, lg, mu_rkv[l], row(w0[l]), row(a0[l]), row(k_k[l]), row(k_a[l]), row(r_k[l]),
            _pad_rows(w2[l], LORA_DECAY_PAD), _pad_rows(a2[l], LORA_ICLR_PAD),
            _pad_rows(g2[l], LORA_GATE_PAD), e_seg, seq, tm_prep)

        b3 = lambda t: t.reshape(bsz, seq, D_MODEL)
        y = _chunk(b3(r), b3(lwd), b3(k2), b3(v), b3(kk), b3(kka)).reshape(n_tok, D_MODEL)

        def group_cols(which, gi):
            lo = rkv_cols + which * ATTN_WIDTH + gi * ATTN_GROUP_WIDTH
            return wl[:, lo:lo + ATTN_GROUP_WIDTH]

        w_kvq = jnp.stack([jnp.concatenate([group_cols(1, gi), group_cols(2, gi), group_cols(0, gi)], axis=1)
                           for gi in range(len(ATTN_GROUPS))]).astype(BF16)
        kvq = _attn_proj(h, w_kvq, bsz, seq, tm_aproj)
        o_l = []
        for gi, (window, dilation) in enumerate(ATTN_GROUPS):
            assert window // dilation == ATTN_BLK
            o_l.append(_attn_group(
                kvq[gi], dilation, slopes[gi * ATTN_HEADS_PER_GROUP:(gi + 1) * ATTN_HEADS_PER_GROUP]))

        x2 = _mix(y, bonus, g, proj, x2, o_l, gt_m, row(ln_x_w[l]), row(ln_x_b[l]), row(g_post_mix[l]),
                  w_o_rwkv[l].astype(BF16), w_o_attn[l].astype(BF16), w_out[l].astype(BF16), e_seg,
                  seq, tm_mix)

        x2 = _ffn(x2, sc_f, sh_f, gt_f, row(g_pre_ffn[l]), row(g_post_ffn[l]),
                  w_ffn_in[l].astype(BF16), w_ffn_out[l].astype(BF16), seq, tm_ffn)
    return x2.reshape(bsz, seq, D_MODEL)
```

```python
import functools
import math

import jax
import jax.numpy as jnp
from jax import lax
from jax.experimental import pallas as pl
from jax.experimental.pallas import tpu as pltpu

F32 = jnp.float32
BF16 = jnp.bfloat16
HIGHEST = lax.Precision.HIGHEST

LANES = 128
SEG_TILE = 256
D_MODEL = 1024
RWKV_HEAD = 64
GN_EPS = 64e-5
RMS_EPS = 1e-6
ATTN_HEAD = 64
ATTN_HEADS_PER_GROUP = 8
ATTN_GROUPS = ((128, 1), (512, 4), (2048, 16))
ATTN_GROUP_WIDTH = ATTN_HEADS_PER_GROUP * ATTN_HEAD
ATTN_WIDTH = len(ATTN_GROUPS) * ATTN_GROUP_WIDTH
ATTN_BLK = 128
FFN_HIDDEN = 2816
LORA_DECAY_PAD = 128
LORA_ICLR_PAD = 128
LORA_GATE_PAD = 256
CHUNK = 64
CHUNKS_PER_STEP = 4
PAIR = 2 * RWKV_HEAD
N_PAIRS = D_MODEL // PAIR

COL_R, COL_K, COL_V, COL_ZA, COL_ZB = 0, 1024, 2048, 3072, 4096
MAIN_COLS = 5120
PROJ_TILE = 5120

VMEM_LIMIT = 48 * 1024 * 1024


def _alibi_slopes(n):
    def pow2(m):
        start = 2.0 ** (-8.0 / m)
        return [start ** (i + 1) for i in range(m)]
    if math.log2(n).is_integer():
        s = pow2(n)
    else:
        p = 2 ** int(math.floor(math.log2(n)))
        s = pow2(p) + pow2(2 * p)[0::2][: n - p]
    return sorted(s, reverse=True)


def _dot(a, b):
    return jnp.dot(a.astype(BF16), b.astype(BF16), preferred_element_type=F32)


def _dot_nt(a, b):
    return lax.dot_general(a.astype(BF16), b.astype(BF16), (((1,), (1,)), ((), ())),
                           preferred_element_type=F32)


def _dot_tn(a, b):
    return lax.dot_general(a.astype(BF16), b.astype(BF16), (((0,), (0,)), ((), ())),
                           preferred_element_type=F32)


def _seg_sum(x, e, pieces=1):
    parts = [x.astype(BF16)]
    if pieces == 2:
        parts.append((x - parts[0].astype(F32)).astype(BF16))
    outs = []
    for t in range(x.shape[1] // SEG_TILE):
        sl = slice(SEG_TILE * t, SEG_TILE * (t + 1))
        outs.append(sum(jnp.dot(part[:, sl], e, preferred_element_type=F32) for part in parts))
    return jnp.concatenate(outs, axis=1)


def _shift_rows(t, prev_row):
    rows = lax.broadcasted_iota(jnp.int32, t.shape, 0)
    return jnp.where(rows == 0, prev_row, pltpu.roll(t, 1, 0))


def _rms(t):
    return t * lax.rsqrt(jnp.mean(t * t, axis=-1, keepdims=True) + RMS_EPS)


def _params(*semantics):
    return pltpu.CompilerParams(dimension_semantics=semantics, vmem_limit_bytes=VMEM_LIMIT)


def _mod_kernel(c_ref, w_ref, b_ref, o_ref):
    o_ref[...] = jnp.dot(c_ref[...], w_ref[...], preferred_element_type=F32,
                         precision=HIGHEST) + b_ref[...]


def _mod(c, w_mod, b_mod):
    bsz = c.shape[0]
    n_out = w_mod.shape[1]
    return pl.pallas_call(
        _mod_kernel,
        grid=(n_out // D_MODEL,),
        in_specs=[pl.BlockSpec((bsz, D_MODEL), lambda j: (0, 0)),
                  pl.BlockSpec((D_MODEL, D_MODEL), lambda j: (0, j)),
                  pl.BlockSpec((1, D_MODEL), lambda j: (0, j))],
        out_specs=pl.BlockSpec((bsz, D_MODEL), lambda j: (0, j)),
        out_shape=jax.ShapeDtypeStruct((bsz, n_out), F32),
        compiler_params=_params("arbitrary"),
        name="adaln_mod",
    )(c, w_mod, b_mod.reshape(1, n_out))


def _shift_proj_kernel(sh_ref, w_ref, o_ref):
    o_ref[...] = _dot(sh_ref[...], w_ref[...])


def _shift_proj(sh, w_main):
    bsz = sh.shape[0]
    return pl.pallas_call(
        _shift_proj_kernel,
        grid=(MAIN_COLS // D_MODEL,),
        in_specs=[pl.BlockSpec((bsz, D_MODEL), lambda j: (0, 0)),
                  pl.BlockSpec((D_MODEL, D_MODEL), lambda j: (0, j))],
        out_specs=pl.BlockSpec((bsz, D_MODEL), lambda j: (0, j)),
        out_shape=jax.ShapeDtypeStruct((bsz, MAIN_COLS), F32),
        compiler_params=_params("arbitrary"),
        name="shift_proj",
    )(sh, w_main)


def _inproj_kernel(x_ref, sc_ref, sh_ref, shw_ref, g_ref, mu_ref, w1_ref, a1_ref, g1_ref, win_ref,
                   p_ref, h_ref, lw_ref, la_ref, lg_ref, last_scr, *, tiles_per_batch):
    i = pl.program_id(0)
    j = pl.program_id(1)
    tm = x_ref.shape[0]

    @pl.when(j == 0)
    def _():
        @pl.when(i % tiles_per_batch == 0)
        def _():
            last_scr[...] = jnp.zeros_like(last_scr)

        x = x_ref[...]
        xg = x * (g_ref[...] * (1.0 + sc_ref[0]))
        rstd = lax.rsqrt(jnp.mean(x * x, axis=-1, keepdims=True) + RMS_EPS)
        p_ref[...] = (jnp.dot(xg.astype(BF16), win_ref[...], preferred_element_type=F32) * rstd
                      + shw_ref[0]).astype(BF16)
        h = xg * rstd + sh_ref[0]
        hs = _shift_rows(h, last_scr[...])
        last_scr[...] = h[tm - 1:tm, :]
        dh = hs - h
        lw_ref[...] = _dot(h + dh * mu_ref[0:1, :], w1_ref[...])
        la_ref[...] = _dot(h + dh * mu_ref[1:2, :], a1_ref[...])
        lg_ref[...] = _dot(h + dh * mu_ref[2:3, :], g1_ref[...])
        h_ref[...] = h.astype(BF16)


def _inproj(x2, sc, sh, g_pre, mu_lora, w1p, a1p, g1p, w_main, seq, tm):
    assert PROJ_TILE == MAIN_COLS
    n_tok = x2.shape[0]
    tpb = seq // tm
    shw = _shift_proj(sh.reshape(-1, D_MODEL), w_main).reshape(-1, 1, MAIN_COLS)
    row = lambda i, j: (i, 0)
    const = lambda i, j: (0, 0)
    modrow = lambda i, j: (i // tpb, 0, 0)
    return pl.pallas_call(
        functools.partial(_inproj_kernel, tiles_per_batch=tpb),
        grid=(n_tok // tm, MAIN_COLS // PROJ_TILE),
        in_specs=[pl.BlockSpec((tm, D_MODEL), row),
                  pl.BlockSpec((1, 1, D_MODEL), modrow),
                  pl.BlockSpec((1, 1, D_MODEL), modrow),
                  pl.BlockSpec((1, 1, MAIN_COLS), modrow),
                  pl.BlockSpec((1, D_MODEL), const),
                  pl.BlockSpec((3, D_MODEL), const),
                  pl.BlockSpec((D_MODEL, LORA_DECAY_PAD), const),
                  pl.BlockSpec((D_MODEL, LORA_ICLR_PAD), const),
                  pl.BlockSpec((D_MODEL, LORA_GATE_PAD), const),
                  pl.BlockSpec((D_MODEL, PROJ_TILE), lambda i, j: (0, j),
                               pipeline_mode=pl.Buffered(1 if PROJ_TILE == MAIN_COLS else 2))],
        out_specs=[pl.BlockSpec((tm, PROJ_TILE), lambda i, j: (i, j)),
                   pl.BlockSpec((tm, D_MODEL), row),
                   pl.BlockSpec((tm, LORA_DECAY_PAD), row),
                   pl.BlockSpec((tm, LORA_ICLR_PAD), row),
                   pl.BlockSpec((tm, LORA_GATE_PAD), row)],
        out_shape=[jax.ShapeDtypeStruct((n_tok, MAIN_COLS), BF16),
                   jax.ShapeDtypeStruct((n_tok, D_MODEL), BF16),
                   jax.ShapeDtypeStruct((n_tok, LORA_DECAY_PAD), F32),
                   jax.ShapeDtypeStruct((n_tok, LORA_ICLR_PAD), F32),
                   jax.ShapeDtypeStruct((n_tok, LORA_GATE_PAD), F32)],
        scratch_shapes=[pltpu.VMEM((1, D_MODEL), F32)],
        compiler_params=_params("arbitrary", "arbitrary"),
        name="in_proj",
    )(x2, sc, sh, shw, g_pre, mu_lora, w1p, a1p, g1p, w_main)


def _attn_proj_kernel(h_ref, w_ref, *refs):
    n_groups = len(ATTN_GROUPS)
    out_refs, scratch = refs[:n_groups], list(refs[n_groups:])
    h = h_ref[...]
    for gi, (_, dilation) in enumerate(ATTN_GROUPS):
        o_ref = out_refs[gi]
        acc = jnp.dot(h, w_ref[gi], preferred_element_type=F32)
        if dilation == 1:
            o_ref[0, 0] = acc.astype(BF16)
            continue
        scr = scratch.pop(0)
        rows = acc.shape[0] // dilation
        for cg in range(scr.shape[0]):
            scr[cg] = acc[:, LANES * cg:LANES * (cg + 1)]
        for r in range(dilation):
            for cg in range(scr.shape[0]):
                o_ref[0, r, :, LANES * cg:LANES * (cg + 1)] = (
                    scr[cg, pl.ds(r, rows, stride=dilation), :].astype(BF16))


def _attn_proj(h, w_kvq, bsz, seq, tm):
    tpb = seq // tm
    width = 3 * ATTN_GROUP_WIDTH
    dilations = [d for _, d in ATTN_GROUPS]
    return pl.pallas_call(
        _attn_proj_kernel,
        grid=(bsz * tpb,),
        in_specs=[pl.BlockSpec((tm, D_MODEL), lambda i: (i, 0)),
                  pl.BlockSpec((len(dilations), D_MODEL, width), lambda i: (0, 0, 0),
                               pipeline_mode=pl.Buffered(1))],
        out_specs=[pl.BlockSpec((1, d, tm // d, width), lambda i: (i // tpb, 0, i % tpb, 0))
                   for d in dilations],
        out_shape=[jax.ShapeDtypeStruct((bsz, d, seq // d, width), BF16) for d in dilations],
        scratch_shapes=[pltpu.VMEM((width // LANES, tm, LANES), F32) for d in dilations if d != 1],
        compiler_params=_params("arbitrary"),
        name="attn_proj",
    )(h, w_kvq)


def _prep_kernel(pr_ref, pk_ref, pv_ref, lw_ref, la_ref, lg_ref, mu_ref, w0_ref, a0_ref, kk_ref,
                 ka_ref, rk_ref, w2_ref, a2_ref, g2_ref, e_ref,
                 r_o, lw_o, k_o, v_o, kk_o, kka_o, g_o, bonus_o, last_scr, *, tiles_per_batch):
    i = pl.program_id(0)
    tm = pr_ref.shape[0]

    @pl.when(i % tiles_per_batch == 0)
    def _():
        last_scr[...] = jnp.zeros_like(last_scr)

    def mix(p_ref, idx):
        p = p_ref[...].astype(F32)
        ps = _shift_rows(p, last_scr[idx:idx + 1, :])
        last_scr[idx:idx + 1, :] = p[tm - 1:tm, :]
        return p + (ps - p) * mu_ref[idx:idx + 1, :]

    r = mix(pr_ref, 0)
    k = mix(pk_ref, 1)
    v = mix(pv_ref, 2)
    e = e_ref[...]

    z = w0_ref[...] + _dot(jnp.tanh(lw_ref[...]), w2_ref[...])
    lw_o[...] = -math.exp(-0.5) * jax.nn.sigmoid(z)
    a = jax.nn.sigmoid(a0_ref[...] + _dot(la_ref[...], a2_ref[...]))
    g_o[...] = _dot(jax.nn.sigmoid(lg_ref[...]), g2_ref[...]).astype(BF16)

    kk = k * kk_ref[...]
    kk = kk * lax.rsqrt(jnp.maximum(_seg_sum(kk * kk, e), 1e-24))
    k2 = k * (1.0 + (a - 1.0) * ka_ref[...])
    r_o[...] = r.astype(BF16)
    k_o[...] = k2.astype(BF16)
    v_o[...] = v.astype(BF16)
    kk_o[...] = kk.astype(BF16)
    kka_o[...] = (kk * a).astype(BF16)
    bonus_o[...] = (_seg_sum(r * k2 * rk_ref[...], e) * v).astype(BF16)


def _prep(proj, lw, la, lg, mu_rkv, w0, a0, k_k, k_a, r_k, w2p, a2p, g2p, e_seg, seq, tm):
    n_tok = proj.shape[0]
    tpb = seq // tm
    row = lambda i: (i, 0)
    const = lambda i: (0, 0)
    vec = pl.BlockSpec((1, D_MODEL), const)
    tok = pl.BlockSpec((tm, D_MODEL), row)
    return pl.pallas_call(
        functools.partial(_prep_kernel, tiles_per_batch=tpb),
        grid=(n_tok // tm,),
        in_specs=[pl.BlockSpec((tm, D_MODEL), lambda i: (i, COL_R // D_MODEL)),
                  pl.BlockSpec((tm, D_MODEL), lambda i: (i, COL_K // D_MODEL)),
                  pl.BlockSpec((tm, D_MODEL), lambda i: (i, COL_V // D_MODEL)),
                  pl.BlockSpec((tm, LORA_DECAY_PAD), row),
                  pl.BlockSpec((tm, LORA_ICLR_PAD), row),
                  pl.BlockSpec((tm, LORA_GATE_PAD), row),
                  pl.BlockSpec((3, D_MODEL), const),
                  vec, vec, vec, vec, vec,
                  pl.BlockSpec((LORA_DECAY_PAD, D_MODEL), const),
                  pl.BlockSpec((LORA_ICLR_PAD, D_MODEL), const),
                  pl.BlockSpec((LORA_GATE_PAD, D_MODEL), const),
                  pl.BlockSpec((SEG_TILE, SEG_TILE), const)],
        out_specs=[tok] * 8,
        out_shape=[jax.ShapeDtypeStruct((n_tok, D_MODEL), F32 if name == "lw" else BF16)
                   for name in ("r", "lw", "k", "v", "kk", "kka", "g", "bonus")],
        scratch_shapes=[pltpu.VMEM((3, D_MODEL), F32)],
        compiler_params=_params("arbitrary"),
        name="rwkv_prep",
    )(proj, proj, proj, lw, la, lg, mu_rkv, w0, a0, k_k, k_a, r_k, w2p, a2p, g2p, e_seg)


def _chunk_kernel(r_ref, lw_ref, k_ref, v_ref, kk_ref, kka_ref, y_ref, s_scr):
    @pl.when(pl.program_id(1) == 0)
    def _():
        s_scr[...] = jnp.zeros_like(s_scr)

    c = CHUNK
    rows = lw_ref.shape[1]
    n_chunks = rows // c
    lw = lw_ref[0]
    ti = lax.broadcasted_iota(jnp.int32, (rows, rows), 0)
    tj = lax.broadcasted_iota(jnp.int32, (rows, rows), 1)
    tril = ((ti >= tj) & (ti // c == tj // c)).astype(BF16)
    lw_1 = lw.astype(BF16)
    lw_2 = (lw - lw_1.astype(F32)).astype(BF16)
    lw_3 = (lw - lw_1.astype(F32) - lw_2.astype(F32)).astype(BF16)
    cum = sum(jnp.dot(tril, piece, preferred_element_type=F32) for piece in (lw_1, lw_2, lw_3))
    cum_prev = cum - lw
    cum_end = jnp.concatenate(
        [jnp.broadcast_to(cum[c * (ck + 1) - 1:c * (ck + 1), :], (c, D_MODEL)) for ck in range(n_chunks)], axis=0)
    kk = kk_ref[0].astype(F32)
    kka = kka_ref[0].astype(F32)
    k = k_ref[0].astype(F32)
    e_neg = jnp.exp(-cum)
    e_end = jnp.exp(cum_end - cum)
    at_all = -kk * jnp.exp(cum_prev)
    rt_all = r_ref[0].astype(F32) * jnp.exp(cum)
    bt_all = kka * e_neg
    kt_all = k * e_neg
    bh_all = kka * e_end
    kh_all = k * e_end
    p_end = jnp.exp(cum_end)
    v_all = v_ref[0].astype(F32)

    ri = lax.broadcasted_iota(jnp.int32, (PAIR, PAIR), 0)
    ci = lax.broadcasted_iota(jnp.int32, (PAIR, PAIR), 1)
    same = (ri // RWKV_HEAD) == (ci // RWKV_HEAD)
    same2 = jnp.concatenate([same, same], axis=1)
    eye = ri == ci
    tt = lax.broadcasted_iota(jnp.int32, (c, PAIR), 0)
    ss = lax.broadcasted_iota(jnp.int32, (c, PAIR), 1) % c
    m_strict = tt > ss
    m_incl = tt >= ss
    eye_rows = (tt == ss).astype(F32)
    zeros_c = jnp.zeros((c, PAIR), F32)

    def bd(t):
        mask = same if t.shape[1] == PAIR else same2
        return jnp.where(mask, jnp.concatenate([t, t], axis=0), 0.0).astype(BF16)

    items = [(slice(c * ck, c * (ck + 1)), slice(PAIR * p, PAIR * (p + 1)))
             for ck in range(n_chunks) for p in range(N_PAIRS)]
    pairs = range(len(items))
    at = [at_all[rs, sl] for rs, sl in items]
    rt = [rt_all[rs, sl] for rs, sl in items]
    v_bd = [bd(v_all[rs, sl]) for rs, sl in items]

    quad = [_dot_nt(jnp.concatenate([at[p], rt[p]], axis=0),
                    jnp.concatenate([bd(bt_all[items[p]]), bd(kt_all[items[p]])], axis=0))
            for p in pairs]
    a_ab = [jnp.where(m_strict, q[:c, :PAIR], 0.0) for q in quad]
    a_ak = [jnp.where(m_strict, q[:c, PAIR:], 0.0) for q in quad]
    a_rb = [jnp.where(m_incl, q[c:, :PAIR], 0.0) for q in quad]
    a_rk = [jnp.where(m_incl, q[c:, PAIR:], 0.0) for q in quad]
    av = [_dot(jnp.concatenate([a_ak[p], a_rk[p]], axis=0), v_bd[p]) for p in pairs]
    x0 = [jnp.concatenate([at[p], av[p][:c]], axis=1) for p in pairs]

    t_inv = [eye_rows + a_ab[p] for p in pairs]
    pw = [_dot(a_ab[p], bd(a_ab[p])) for p in pairs]
    for _ in range(int(math.log2(c)) - 2):
        both = [_dot(jnp.concatenate([t_inv[p], pw[p]], axis=0), bd(pw[p])) for p in pairs]
        t_inv = [t_inv[p] + both[p][:c] for p in pairs]
        pw = [both[p][c:] for p in pairs]
    t_inv = [t_inv[p] + _dot(t_inv[p], bd(pw[p])) for p in pairs]
    x = [_dot(t_inv[p], bd(x0[p])) for p in pairs]

    top = [_dot(a_rb[p], bd(x[p])) for p in pairs]
    bot = [_dot_tn(jnp.concatenate([bh_all[items[p]], kh_all[items[p]]], axis=0),
                   jnp.concatenate([x[p], jnp.concatenate([zeros_c, v_all[items[p]]], axis=1)], axis=0))
           for p in pairs]

    state = [s_scr[hp] for hp in range(N_PAIRS)]
    for p in pairs:
        rs, sl = items[p]
        hp = p % N_PAIRS
        p_end_row = p_end[rs, sl][c - 1:c, :]
        m_mat = jnp.where(same, bot[p][:, :PAIR], 0.0) + jnp.where(
            eye, jnp.broadcast_to(p_end_row, (PAIR, PAIR)), 0.0)
        lhs = jnp.concatenate([rt[p] + top[p][:, :PAIR], m_mat], axis=0)
        fin = _dot(lhs, state[hp])
        y_ref[0, rs, sl] = fin[:c] + top[p][:, PAIR:] + av[p][c:]
        state[hp] = fin[c:] + jnp.where(same, bot[p][:, PAIR:], 0.0)
    for hp in range(N_PAIRS):
        s_scr[hp] = state[hp]


def _chunk(r, lw, k, v, kk, kka):
    bsz, seq, _ = r.shape
    spec = pl.BlockSpec((1, CHUNKS_PER_STEP * CHUNK, D_MODEL), lambda b, c: (b, c, 0))
    return pl.pallas_call(
        _chunk_kernel,
        grid=(bsz, seq // (CHUNKS_PER_STEP * CHUNK)),
        in_specs=[spec] * 6,
        out_specs=spec,
        out_shape=jax.ShapeDtypeStruct((bsz, seq, D_MODEL), F32),
        scratch_shapes=[pltpu.VMEM((N_PAIRS, PAIR, PAIR), F32)],
        compiler_params=_params("arbitrary", "arbitrary"),
        name="rwkv_chunk",
    )(r, lw, k, v, kk, kka)


def _attn_kernel(cur_ref, prev_ref, o_ref, l_ref, *, dilation, slopes):
    n = pl.program_id(2)
    blk = ATTN_BLK
    gw = ATTN_GROUP_WIDTH
    ii = lax.broadcasted_iota(jnp.int32, (blk, 2 * blk), 0)
    jj = lax.broadcasted_iota(jnp.int32, (blk, 2 * blk), 1)
    delta = ii - jj + blk
    in_window = (delta >= 0) & (delta <= blk)
    dist = delta.astype(F32) * float(dilation)
    lane_head = lax.broadcasted_iota(jnp.int32, (blk, PAIR), 1) // ATTN_HEAD
    lane = lax.broadcasted_iota(jnp.int32, (blk, LANES), 1)
    neg_inf = float("-inf")
    pairs = range(gw // PAIR)
    heads = [(pp, hh) for pp in pairs for hh in range(2)]
    sls = [slice(PAIR * pp, PAIR * (pp + 1)) for pp in pairs]

    for sb in range(2):
        rows = slice(blk * sb, blk * (sb + 1))
        if sb == 0:
            valid = in_window & ((n > 0) | (jj >= blk))
            kcat = [jnp.concatenate([prev_ref[0, 0, :, sl], cur_ref[0, 0, rows, sl]], axis=0) for sl in sls]
            vcat = [jnp.concatenate([prev_ref[0, 0, :, gw + PAIR * pp:gw + PAIR * (pp + 1)],
                                     cur_ref[0, 0, rows, gw + PAIR * pp:gw + PAIR * (pp + 1)]], axis=0)
                    for pp in pairs]
        else:
            valid = in_window
            kcat = [cur_ref[0, 0, :, sl] for sl in sls]
            vcat = [cur_ref[0, 0, :, gw + PAIR * pp:gw + PAIR * (pp + 1)] for pp in pairs]
        q2 = [cur_ref[0, 0, rows, 2 * gw + PAIR * pp:2 * gw + PAIR * (pp + 1)] * (ATTN_HEAD ** -0.5)
              for pp in pairs]
        s = [jnp.where(valid,
                       _dot_nt(jnp.where(lane_head == hh, q2[pp], jnp.zeros_like(q2[pp])), kcat[pp])
                       - float(slopes[2 * pp + hh]) * dist, neg_inf) for pp, hh in heads]
        m = [jnp.max(t, axis=-1, keepdims=True) for t in s]
        p = [jnp.exp(s[i] - m[i]) for i in range(len(heads))]
        den = [jnp.sum(t, axis=-1, keepdims=True) for t in p]
        o = [_dot(p[i], vcat[pp]) * (1.0 / den[i]) for i, (pp, _) in enumerate(heads)]
        for pp in pairs:
            o_ref[0, 0, rows, sls[pp]] = jnp.where(lane_head == 0, o[2 * pp], o[2 * pp + 1]).astype(BF16)
        lse = jnp.zeros((blk, LANES), F32)
        for i in range(len(heads)):
            lse = jnp.where(lane == i, m[i] + jnp.log(den[i]), lse)
        l_ref[0, 0, rows, :] = lse


def _attn_group(kvq, dilation, slopes):
    bsz, _, sub, _ = kvq.shape
    gw = ATTN_GROUP_WIDTH
    step = 2 * ATTN_BLK
    return pl.pallas_call(
        functools.partial(_attn_kernel, dilation=dilation, slopes=tuple(slopes)),
        grid=(bsz, dilation, sub // step),
        in_specs=[pl.BlockSpec((1, 1, step, 3 * gw), lambda b, r, n: (b, r, n, 0)),
                  pl.BlockSpec((1, 1, ATTN_BLK, 2 * gw), lambda b, r, n: (b, r, jnp.maximum(2 * n - 1, 0), 0))],
        out_specs=[pl.BlockSpec((1, 1, step, gw), lambda b, r, n: (b, r, n, 0)),
                   pl.BlockSpec((1, 1, step, LANES), lambda b, r, n: (b, r, n, 0))],
        out_shape=[jax.ShapeDtypeStruct((bsz, dilation, sub, gw), BF16),
                   jax.ShapeDtypeStruct((bsz, dilation, sub, LANES), F32)],
        compiler_params=_params("arbitrary", "arbitrary", "arbitrary"),
        name=f"dilated_attn_d{dilation}",
    )(kvq, kvq)


def _mix_kernel(y_ref, bonus_ref, g_ref, za_ref, zb_ref, x_ref, o0_ref, o1_ref, o2_ref,
                l0_ref, l1_ref, l2_ref, gt_ref, lnw_ref, lnb_ref, gpost_ref,
                wor_ref, woa_ref, wout_ref, e_ref, spread_ref, out_ref, *scratch):
    e = e_ref[...]
    inv_n = 1.0 / RWKV_HEAD
    y = y_ref[...]
    mean = _seg_sum(y, e, pieces=2) * inv_n
    yc = y - mean
    var = _seg_sum(yc * yc, e) * inv_n
    yn = yc * lax.rsqrt(var + GN_EPS) * lnw_ref[...] + lnb_ref[...]
    y_rwkv = _dot((yn + bonus_ref[...].astype(F32)) * g_ref[...].astype(F32), wor_ref[...])

    def token_order(ref, scr):
        dilation, rows = ref.shape[1], ref.shape[2]
        if dilation == 1:
            return ref[0, 0].astype(F32)
        for r in range(dilation):
            for cg in range(ref.shape[3] // LANES):
                scr[cg, pl.ds(r, rows, stride=dilation), :] = (
                    ref[0, r, :, LANES * cg:LANES * (cg + 1)].astype(F32))
        return jnp.concatenate([scr[cg] for cg in range(ref.shape[3] // LANES)], axis=1)

    def spread(w):
        return jnp.dot(w.astype(BF16), spread_ref[...], preferred_element_type=F32)

    o0, l0 = token_order(o0_ref, None), token_order(l0_ref, None)
    o1, l1 = token_order(o1_ref, scratch[0]), token_order(l1_ref, scratch[1])
    o2, l2 = token_order(o2_ref, scratch[2]), token_order(l2_ref, scratch[3])
    mx = jnp.maximum(jnp.maximum(l0, l1), l2)
    e0, e1, e2 = jnp.exp(l0 - mx), jnp.exp(l1 - mx), jnp.exp(l2 - mx)
    inv = 1.0 / (e0 + e1 + e2)
    o = spread(e0 * inv) * o0 + spread(e1 * inv) * o1 + spread(e2 * inv) * o2
    y_attn = _dot(o, woa_ref[...])

    mixed = _dot(jax.nn.sigmoid(za_ref[...].astype(F32)) * y_rwkv
                 + jax.nn.sigmoid(zb_ref[...].astype(F32)) * y_attn,
                 wout_ref[...])
    out_ref[...] = x_ref[...] + gt_ref[0] * (_rms(mixed) * gpost_ref[...])


def _mix(y, bonus, g, proj, x2, o_l, gt, ln_w, ln_b, g_post, w_or, w_oa, w_out, e_seg, seq, tm):
    n_tok = x2.shape[0]
    tpb = seq // tm
    row = lambda i: (i, 0)
    const = lambda i: (0, 0)
    tok = pl.BlockSpec((tm, D_MODEL), row)
    vec = pl.BlockSpec((1, D_MODEL), const)

    def att(dilation, width):
        return pl.BlockSpec((1, dilation, tm // dilation, width), lambda i: (i // tpb, 0, i % tpb, 0))

    (o0, l0), (o1, l1), (o2, l2) = o_l
    d0, d1, d2 = [d for _, d in ATTN_GROUPS]
    assert d0 == 1
    gw = ATTN_GROUP_WIDTH
    head_of_lane = jnp.arange(gw) // ATTN_HEAD
    spread = (jnp.arange(LANES)[:, None] == head_of_lane[None, :]).astype(BF16)
    return pl.pallas_call(
        _mix_kernel,
        grid=(n_tok // tm,),
        in_specs=[tok, tok, tok,
                  pl.BlockSpec((tm, D_MODEL), lambda i: (i, COL_ZA // D_MODEL)),
                  pl.BlockSpec((tm, D_MODEL), lambda i: (i, COL_ZB // D_MODEL)),
                  tok, att(d0, gw), att(d1, gw), att(d2, gw), att(d0, LANES), att(d1, LANES), att(d2, LANES),
                  pl.BlockSpec((1, 1, D_MODEL), lambda i: (i // tpb, 0, 0)),
                  vec, vec, vec,
                  pl.BlockSpec((D_MODEL, D_MODEL), const, pipeline_mode=pl.Buffered(1)),
                  pl.BlockSpec((gw, D_MODEL), const, pipeline_mode=pl.Buffered(1)),
                  pl.BlockSpec((D_MODEL, D_MODEL), const, pipeline_mode=pl.Buffered(1)),
                  pl.BlockSpec((SEG_TILE, SEG_TILE), const),
                  pl.BlockSpec((LANES, gw), const)],
        out_specs=tok,
        out_shape=jax.ShapeDtypeStruct((n_tok, D_MODEL), F32),
        scratch_shapes=[pltpu.VMEM((gw // LANES, tm, LANES), F32), pltpu.VMEM((1, tm, LANES), F32)] * 2,
        compiler_params=_params("arbitrary"),
        name="mix_out",
    )(y, bonus, g, proj, proj, x2, o0, o1, o2, l0, l1, l2, gt, ln_w, ln_b, g_post,
      w_or, w_oa, w_out, e_seg, spread)


def _ffn_kernel(x_ref, sc_ref, sh_ref, gt_ref, gpre_ref, gpost_ref, wg_ref, wu_ref, wo_ref, out_ref):
    x = x_ref[...]
    h = (_rms(x) * gpre_ref[...] * (1.0 + sc_ref[0]) + sh_ref[0]).astype(BF16)
    acts = []
    for t in range(FFN_HIDDEN // SEG_TILE):
        sl = slice(SEG_TILE * t, SEG_TILE * (t + 1))
        u_gate = jnp.dot(h, wg_ref[:, sl], preferred_element_type=F32)
        u_up = jnp.dot(h, wu_ref[:, sl], preferred_element_type=F32)
        acts.append((u_gate * jax.nn.sigmoid(u_gate) * u_up).astype(BF16))
    y = jnp.dot(jnp.concatenate(acts, axis=1), wo_ref[...], preferred_element_type=F32)
    out_ref[...] = x + gt_ref[0] * (_rms(y) * gpost_ref[...])


def _ffn(x1, sc, sh, gt, g_pre, g_post, w_in, w_out, seq, tm):
    n_tok = x1.shape[0]
    tpb = seq // tm
    row = lambda i: (i, 0)
    const = lambda i: (0, 0)
    modrow = lambda i: (i // tpb, 0, 0)
    tok = pl.BlockSpec((tm, D_MODEL), row)
    mod = pl.BlockSpec((1, 1, D_MODEL), modrow)
    vec = pl.BlockSpec((1, D_MODEL), const)
    resident = pl.Buffered(1)
    return pl.pallas_call(
        _ffn_kernel,
        grid=(n_tok // tm,),
        in_specs=[tok, mod, mod, mod, vec, vec,
                  pl.BlockSpec((D_MODEL, FFN_HIDDEN), lambda i: (0, 0), pipeline_mode=resident),
                  pl.BlockSpec((D_MODEL, FFN_HIDDEN), lambda i: (0, 1), pipeline_mode=resident),
                  pl.BlockSpec((FFN_HIDDEN, D_MODEL), const, pipeline_mode=resident)],
        out_specs=tok,
        out_shape=jax.ShapeDtypeStruct((n_tok, D_MODEL), F32),
        compiler_params=_params("arbitrary"),
        name="swiglu_ffn",
    )(x1, sc, sh, gt, g_pre, g_post, w_in, w_in, w_out)


def _pad_cols(w, n):
    return jnp.pad(w, ((0, 0), (0, n - w.shape[1]))).astype(BF16)


def _pad_rows(w, n):
    return jnp.pad(w, ((0, n - w.shape[0]), (0, 0))).astype(BF16)


def kernel(x, c, w_mod, b_mod, g_pre_mix, g_post_mix, g_pre_ffn, g_post_ffn, w_in, mu_rkv, mu_lora,
           w0, w1, w2, a0, a1, a2, g1, g2, k_k, k_a, r_k, ln_x_w, ln_x_b, w_o_rwkv, w_o_attn, w_out,
           w_ffn_in, w_ffn_out):
    bsz, seq, _ = x.shape
    depth = w_mod.shape[0]
    assert seq % (2 * ATTN_BLK * ATTN_GROUPS[-1][1]) == 0
    n_tok = bsz * seq
    tm_proj = min(512, seq)
    tm_aproj = min(1024, seq)
    tm_prep = min(512, seq)
    tm_mix = min(512, seq)
    tm_ffn = min(512, seq)

    slopes = _alibi_slopes(len(ATTN_GROUPS) * ATTN_HEADS_PER_GROUP)
    seg = jnp.arange(SEG_TILE) // RWKV_HEAD
    e_seg = (seg[:, None] == seg[None, :]).astype(BF16)
    row = lambda t: t.reshape(1, -1)

    x2 = x.reshape(n_tok, D_MODEL)
    for l in range(depth):
        mod = _mod(c, w_mod[l], b_mod[l])
        sh_m, sc_m, gt_m, sh_f, sc_f, gt_f = [mod[:, i * D_MODEL:(i + 1) * D_MODEL].reshape(bsz, 1, D_MODEL)
                                               for i in range(6)]
        wl = w_in[l]
        rkv_cols = 3 * D_MODEL
        att_cols = 3 * ATTN_WIDTH
        w_main = jnp.concatenate([wl[:, :rkv_cols], wl[:, rkv_cols + att_cols:]], axis=1).astype(BF16)

        proj, h, lw, la, lg = _inproj(
            x2, sc_m, sh_m, row(g_pre_mix[l]), mu_lora[l],
            _pad_cols(w1[l], LORA_DECAY_PAD), _pad_cols(a1[l], LORA_ICLR_PAD),
            _pad_cols(g1[l], LORA_GATE_PAD), w_main, seq, tm_proj)

        r, lwd, k2, v, kk, kka, g, bonus = _prep(
            proj, lw, la, lg, mu_rkv[l], row(w0[l]), row(a0[l]), row(k_k[l]), row(k_a[l]), row(r_k[l]),
            _pad_rows(w2[l], LORA_DECAY_PAD), _pad_rows(a2[l], LORA_ICLR_PAD),
            _pad_rows(g2[l], LORA_GATE_PAD), e_seg, seq, tm_prep)

        b3 = lambda t: t.reshape(bsz, seq, D_MODEL)
        y = _chunk(b3(r), b3(lwd), b3(k2), b3(v), b3(kk), b3(kka)).reshape(n_tok, D_MODEL)

        def group_cols(which, gi):
            lo = rkv_cols + which * ATTN_WIDTH + gi * ATTN_GROUP_WIDTH
            return wl[:, lo:lo + ATTN_GROUP_WIDTH]

        w_kvq = jnp.stack([jnp.concatenate([group_cols(1, gi), group_cols(2, gi), group_cols(0, gi)], axis=1)
                           for gi in range(len(ATTN_GROUPS))]).astype(BF16)
        kvq = _attn_proj(h, w_kvq, bsz, seq, tm_aproj)
        o_l = []
        for gi, (window, dilation) in enumerate(ATTN_GROUPS):
            assert window // dilation == ATTN_BLK
            o_l.append(_attn_group(
                kvq[gi], dilation, slopes[gi * ATTN_HEADS_PER_GROUP:(gi + 1) * ATTN_HEADS_PER_GROUP]))

        x2 = _mix(y, bonus, g, proj, x2, o_l, gt_m, row(ln_x_w[l]), row(ln_x_b[l]), row(g_post_mix[l]),
                  w_o_rwkv[l].astype(BF16), w_o_attn[l].astype(BF16), w_out[l].astype(BF16), e_seg,
                  seq, tm_mix)

        x2 = _ffn(x2, sc_f, sh_f, gt_f, row(g_pre_ffn[l]), row(g_post_ffn[l]),
                  w_ffn_in[l].astype(BF16), w_ffn_out[l].astype(BF16), seq, tm_ffn)
    return x2.reshape(bsz, seq, D_MODEL)
```
